```python
import math
import jax, jax.numpy as jnp
from jax import lax
import numpy as np

D_MODEL = 1024
BATCH = 4
SEQ = 8192
DEPTH = 4

CONV_CH = D_MODEL // 2
CONV_K = 3
SSM_WIDTH = D_MODEL // 2
SSM_GROUP = 16
SSM_GROUPS = SSM_WIDTH // SSM_GROUP
SSM_STATE = 64
DT_MIN = 1e-3
DT_MAX = 1e-1
N_HEADS = 16
HEAD_DIM = D_MODEL // N_HEADS
ROPE_THETA = 10000.0
DILATED_PATTERNS = ((128, 1), (512, 4), (2048, 16))
ATT_BLOCK = 128
N_EXPERTS = 32
TOP_K = 4
D_EXPERT = D_MODEL
SWIGLU_LIMIT = 7.0
SWIGLU_ALPHA = 1.702
MOE_BLOCK = 256
DN_ALPHA = (2 * DEPTH) ** 0.25
DN_BETA = (8 * DEPTH) ** -0.25
LN_EPS = 1e-5
N_EVEN = (DEPTH + 1) // 2
N_ODD = DEPTH // 2

kernel_name = "hybrid_conv_s5_dilated_moe_deepnorm"


def layer_norm(x, g, b):
    xf = x.astype(jnp.float32)
    mu = jnp.mean(xf, axis=-1, keepdims=True)
    xc = xf - mu
    var = jnp.mean(xc * xc, axis=-1, keepdims=True)
    return (xc * lax.rsqrt(var + LN_EPS) * g.astype(jnp.float32) + b.astype(jnp.float32)).astype(x.dtype)


def causal_depthwise_conv(v, w):
    c = v.shape[-1]
    return lax.conv_general_dilated(v, w[:, None, :].astype(v.dtype), window_strides=(1,),
                                    padding=[(CONV_K - 1, 0)],
                                    dimension_numbers=("NWC", "WIO", "NWC"),
                                    feature_group_count=c)


def s5_layer(u, a_re, a_im, log_dt, b_re, b_im, c_re, c_im, d_skip, w_glu, b_glu):
    bsz, L, _ = u.shape
    f32 = jnp.float32
    uf = u.astype(f32).reshape(bsz, L, SSM_GROUPS, SSM_GROUP)
    lam_re = jnp.minimum(a_re.astype(f32), -1e-4)
    lam_im = a_im.astype(f32)
    dt = jnp.exp(log_dt.astype(f32))[:, None]
    mag = jnp.exp(lam_re * dt)
    ab_re = mag * jnp.cos(lam_im * dt)
    ab_im = mag * jnp.sin(lam_im * dt)
    nr, ni = ab_re - 1.0, ab_im
    den = lam_re * lam_re + lam_im * lam_im
    coef_re = ((nr * lam_re + ni * lam_im) / den)[..., None]
    coef_im = ((ni * lam_re - nr * lam_im) / den)[..., None]
    br, bi = b_re.astype(f32), b_im.astype(f32)
    bb_re = coef_re * br - coef_im * bi
    bb_im = coef_re * bi + coef_im * br
    bu_re = jnp.einsum("blgh,gph->blgp", uf, bb_re)
    bu_im = jnp.einsum("blgh,gph->blgp", uf, bb_im)
    a_re_t = jnp.broadcast_to(ab_re, bu_re.shape)
    a_im_t = jnp.broadcast_to(ab_im, bu_re.shape)

    def combine(e1, e2):
        a1r, a1i, b1r, b1i = e1
        a2r, a2i, b2r, b2i = e2
        return (a2r * a1r - a2i * a1i, a2r * a1i + a2i * a1r,
                a2r * b1r - a2i * b1i + b2r, a2r * b1i + a2i * b1r + b2i)

    _, _, xr, xi = lax.associative_scan(combine, (a_re_t, a_im_t, bu_re, bu_im), axis=1)
    y = (jnp.einsum("blgp,ghp->blgh", xr, c_re.astype(f32))
         - jnp.einsum("blgp,ghp->blgh", xi, c_im.astype(f32))
         + d_skip.astype(f32) * uf)
    y = y.reshape(bsz, L, SSM_WIDTH)
    z = jax.nn.gelu(y)
    z = z * jax.nn.sigmoid(z @ w_glu.astype(f32) + b_glu.astype(f32))
    return z.astype(u.dtype)


def conv_ssm_mixer(x, w_in, conv_w, a_re, a_im, log_dt, b_re, b_im, c_re, c_im, d_skip,
                   w_glu, b_glu, w_out):
    proj = x @ w_in
    gate_b = proj[..., :CONV_CH]
    gate_c = proj[..., CONV_CH:2 * CONV_CH]
    h = proj[..., 2 * CONV_CH:3 * CONV_CH]
    u = proj[..., 3 * CONV_CH:]
    y_conv = gate_b * causal_depthwise_conv(gate_c * h, conv_w)
    y_ssm = s5_layer(u, a_re, a_im, log_dt, b_re, b_im, c_re, c_im, d_skip, w_glu, b_glu)
    return jnp.concatenate([y_conv, y_ssm], axis=-1) @ w_out


def rope(x):
    L = x.shape[1]
    half = HEAD_DIM // 2
    inv = ROPE_THETA ** (-jnp.arange(half, dtype=jnp.float32) / half)
    ang = jnp.arange(L, dtype=jnp.float32)[:, None] * inv[None, :]
    cos = jnp.cos(ang)[None, :, None, :]
    sin = jnp.sin(ang)[None, :, None, :]
    xf = x.astype(jnp.float32)
    x1, x2 = xf[..., :half], xf[..., half:]
    return jnp.concatenate([x1 * cos - x2 * sin, x2 * cos + x1 * sin], axis=-1).astype(x.dtype)


def dilated_window_attention(q, k, v, window, dilation):
    bsz, L, H, Dh = q.shape
    span = dilation * ATT_BLOCK
    Lp = -(-L // span) * span
    M = Lp // dilation
    nb = M // ATT_BLOCK
    reach = window // dilation

    def to_blocks(a):
        a = jnp.pad(a, ((0, 0), (0, Lp - L), (0, 0), (0, 0)))
        a = a.reshape(bsz, M, dilation, H, Dh).transpose(0, 2, 1, 3, 4)
        return a.reshape(bsz, dilation, nb, ATT_BLOCK, H, Dh)

    def with_prev(a):
        prev = jnp.pad(a, ((0, 0), (0, 0), (1, 0), (0, 0), (0, 0), (0, 0)))[:, :, :-1]
        return jnp.concatenate([prev, a], axis=3)

    qb = to_blocks(q)
    kk = with_prev(to_blocks(k))
    vv = with_prev(to_blocks(v))
    s = jnp.einsum("brnqhe,brnkhe->brnhqk", qb, kk, preferred_element_type=jnp.float32)
    qi = jnp.arange(ATT_BLOCK)[:, None]
    kj = jnp.arange(2 * ATT_BLOCK)[None, :]
    dist = ATT_BLOCK + qi - kj
    band = (dist >= 0) & (dist <= reach)
    first = (jnp.arange(nb) == 0)[:, None, None]
    valid = band[None] & ~(first & (kj < ATT_BLOCK)[None])
    s = jnp.where(valid[:, None], s, -jnp.inf)
    m = jnp.max(s, axis=-1, keepdims=True)
    p = jnp.exp(s - m)
    den = jnp.sum(p, axis=-1, keepdims=True)
    o = jnp.einsum("brnhqk,brnkhe->brnhqe", p, vv.astype(jnp.float32)) / den
    lse = (m + jnp.log(den))[..., 0]
    o = o.transpose(0, 1, 2, 4, 3, 5).reshape(bsz, dilation, M, H, Dh)
    o = o.transpose(0, 2, 1, 3, 4).reshape(bsz, Lp, H, Dh)[:, :L]
    lse = lse.transpose(0, 1, 2, 4, 3).reshape(bsz, dilation, M, H)
    lse = lse.transpose(0, 2, 1, 3).reshape(bsz, Lp, H)[:, :L]
    return o, lse


def dilated_attention_mixer(x, w_qkv, w_o):
    bsz, L, _ = x.shape
    qkv = (x @ w_qkv).reshape(bsz, L, 3, N_HEADS, HEAD_DIM)
    q = rope(qkv[:, :, 0]) * (HEAD_DIM ** -0.5)
    k = rope(qkv[:, :, 1])
    v = qkv[:, :, 2]
    outs, lses = [], []
    for window, dilation in DILATED_PATTERNS:
        o, lse = dilated_window_attention(q, k, v, window, dilation)
        outs.append(o)
        lses.append(lse)
    wts = jax.nn.softmax(jnp.stack(lses, axis=0), axis=0)
    o = jnp.einsum("pblh,pblhe->blhe", wts, jnp.stack(outs, axis=0))
    return o.reshape(bsz, L, D_MODEL).astype(x.dtype) @ w_o


def moe_ffn(x, router_w, router_b, w_gu, b_gu, w_down, b_down):
    bsz, L, D = x.shape
    T = bsz * L
    xt = x.reshape(T, D)
    logits = (xt @ router_w).astype(jnp.float32) + router_b.astype(jnp.float32)
    top_val, top_idx = lax.top_k(logits, TOP_K)
    gates = jax.nn.softmax(top_val, axis=-1)
    n_assign = T * TOP_K
    flat_e = top_idx.reshape(-1)
    flat_tok = jnp.arange(n_assign, dtype=jnp.int32) // TOP_K
    flat_gate = gates.reshape(-1)
    order = jnp.argsort(flat_e)
    se = flat_e[order]
    counts = jnp.bincount(flat_e, length=N_EXPERTS)
    padded = (counts + MOE_BLOCK - 1) // MOE_BLOCK * MOE_BLOCK
    pad_end = jnp.cumsum(padded)
    pad_start = pad_end - padded
    start = jnp.cumsum(counts) - counts
    rank = jnp.arange(n_assign, dtype=jnp.int32) - start[se]
    dest = pad_start[se] + rank
    n_blocks = -(-n_assign // MOE_BLOCK) + N_EXPERTS
    n_rows = n_blocks * MOE_BLOCK
    row_tok = jnp.full((n_rows,), T, jnp.int32).at[dest].set(flat_tok[order])
    row_gate = jnp.zeros((n_rows,), jnp.float32).at[dest].set(flat_gate[order])
    block_start = jnp.arange(n_blocks, dtype=pad_end.dtype) * MOE_BLOCK
    block_e = jnp.minimum(jnp.searchsorted(pad_end, block_start, side="right"), N_EXPERTS - 1)
    x_pad = jnp.concatenate([xt, jnp.zeros((1, D), xt.dtype)], axis=0)

    def expert_block(args):
        tok, gate, e = args
        h = x_pad[tok] @ w_gu[e] + b_gu[e]
        g = jnp.minimum(h[:, :D_EXPERT], SWIGLU_LIMIT)
        lin = jnp.clip(h[:, D_EXPERT:], -SWIGLU_LIMIT, SWIGLU_LIMIT)
        act = (lin + 1.0) * (g * jax.nn.sigmoid(SWIGLU_ALPHA * g))
        y = act @ w_down[e] + b_down[e]
        return y * gate[:, None].astype(y.dtype)

    ys = lax.map(expert_block, (row_tok.reshape(n_blocks, MOE_BLOCK),
                                row_gate.reshape(n_blocks, MOE_BLOCK), block_e))
    out = jnp.zeros((T + 1, D), ys.dtype).at[row_tok].add(ys.reshape(n_rows, D))[:T]
    return out.reshape(bsz, L, D).astype(x.dtype)


def setup_inputs(seed: int = 0) -> dict:
    key = jax.random.key(seed)
    ks = jax.random.split(key, 32)
    f32 = jnp.float32

    def nrm(k, shape, scale):
        return jax.random.normal(k, shape, f32) * scale

    NE, NO = N_EVEN, N_ODD
    G, P, H = SSM_GROUPS, SSM_STATE, SSM_GROUP
    mix_in = 3 * CONV_CH + SSM_WIDTH
    mix_out = CONV_CH + SSM_WIDTH
    return {
        "x": nrm(ks[0], (BATCH, SEQ, D_MODEL), 1.0),
        "hy_w_in": nrm(ks[1], (NE, D_MODEL, mix_in), D_MODEL ** -0.5),
        "conv_w": nrm(ks[2], (NE, CONV_K, CONV_CH), CONV_K ** -0.5),
        "ssm_a_re": -0.5 + nrm(ks[3], (NE, G, P), 0.01),
        "ssm_a_im": jnp.pi * jnp.arange(P, dtype=f32) + nrm(ks[4], (NE, G, P), 0.01),
        "ssm_log_dt": jax.random.uniform(ks[5], (NE, G), f32, math.log(DT_MIN), math.log(DT_MAX)),
        "ssm_b_re": nrm(ks[6], (NE, G, P, H), (2 * H) ** -0.5),
        "ssm_b_im": nrm(ks[7], (NE, G, P, H), (2 * H) ** -0.5),
        "ssm_c_re": nrm(ks[8], (NE, G, H, P), (2 * P) ** -0.5),
        "ssm_c_im": nrm(ks[9], (NE, G, H, P), (2 * P) ** -0.5),
        "ssm_d": nrm(ks[10], (NE, G, H), 1.0),
        "ssm_w_glu": nrm(ks[11], (NE, SSM_WIDTH, SSM_WIDTH), SSM_WIDTH ** -0.5),
        "ssm_b_glu": nrm(ks[12], (NE, SSM_WIDTH), 0.01),
        "hy_w_out": nrm(ks[13], (NE, mix_out, D_MODEL), DN_BETA * mix_out ** -0.5),
        "att_w_qkv": nrm(ks[14], (NO, D_MODEL, 3 * D_MODEL), D_MODEL ** -0.5),
        "att_w_o": nrm(ks[15], (NO, D_MODEL, D_MODEL), DN_BETA * D_MODEL ** -0.5),
        "ln_mix_g": 1.0 + nrm(ks[16], (DEPTH, D_MODEL), 0.01),
        "ln_mix_b": nrm(ks[17], (DEPTH, D_MODEL), 0.01),
        "ln_ffn_g": 1.0 + nrm(ks[18], (DEPTH, D_MODEL), 0.01),
        "ln_ffn_b": nrm(ks[19], (DEPTH, D_MODEL), 0.01),
        "router_w": nrm(ks[20], (DEPTH, D_MODEL, N_EXPERTS), D_MODEL ** -0.5),
        "router_b": nrm(ks[21], (DEPTH, N_EXPERTS), 0.01),
        "expert_w_gu": nrm(ks[22], (DEPTH, N_EXPERTS, D_MODEL, 2 * D_EXPERT), D_MODEL ** -0.5),
        "expert_b_gu": nrm(ks[23], (DEPTH, N_EXPERTS, 2 * D_EXPERT), 0.01),
        "expert_w_down": nrm(ks[24], (DEPTH, N_EXPERTS, D_EXPERT, D_MODEL), DN_BETA * D_EXPERT ** -0.5),
        "expert_b_down": nrm(ks[25], (DEPTH, N_EXPERTS, D_MODEL), 0.01),
    }


def reference(x, hy_w_in, conv_w, ssm_a_re, ssm_a_im, ssm_log_dt, ssm_b_re, ssm_b_im,
              ssm_c_re, ssm_c_im, ssm_d, ssm_w_glu, ssm_b_glu, hy_w_out, att_w_qkv, att_w_o,
              ln_mix_g, ln_mix_b, ln_ffn_g, ln_ffn_b, router_w, router_b, expert_w_gu,
              expert_b_gu, expert_w_down, expert_b_down):
    h = x
    for layer in range(DEPTH):
        i = layer // 2
        if layer % 2 == 0:
            mix = conv_ssm_mixer(h, hy_w_in[i], conv_w[i], ssm_a_re[i], ssm_a_im[i], ssm_log_dt[i],
                                 ssm_b_re[i], ssm_b_im[i], ssm_c_re[i], ssm_c_im[i], ssm_d[i],
                                 ssm_w_glu[i], ssm_b_glu[i], hy_w_out[i])
        else:
            mix = dilated_attention_mixer(h, att_w_qkv[i], att_w_o[i])
        h = layer_norm(DN_ALPHA * h + mix, ln_mix_g[layer], ln_mix_b[layer])
        ffn = moe_ffn(h, router_w[layer], router_b[layer], expert_w_gu[layer], expert_b_gu[layer],
                      expert_w_down[layer], expert_b_down[layer])
        h = layer_norm(DN_ALPHA * h + ffn, ln_ffn_g[layer], ln_ffn_b[layer])
    return h
```

```python
import functools
import math

import jax
import jax.numpy as jnp
from jax import lax
from jax.experimental import pallas as pl
from jax.experimental.pallas import tpu as pltpu

F32 = jnp.float32
BF16 = jnp.bfloat16

D_MODEL = 1024
DEPTH = 4
CONV_CH = 512
CONV_K = 3
SSM_WIDTH = 512
SSM_GROUP = 16
SSM_GROUPS = 32
SSM_STATE = 64
N_HEADS = 16
HEAD_DIM = 64
ROPE_THETA = 10000.0
DILATIONS = (1, 4, 16)
ATT_BLOCK = 128
N_EXPERTS = 32
TOP_K = 4
SWIGLU_LIMIT = 7.0
SWIGLU_ALPHA = 1.702
DN_ALPHA = (2 * DEPTH) ** 0.25
LN_EPS = 1e-5

LANES = 128
ROW_TILE = 512
SSM_CHUNK = 32
MOE_ROWS = 512
VMEM_LIMIT = 56 * 1024 * 1024
NEG_BIG = -1e30


def _params(n_axes=1):
    return pltpu.CompilerParams(dimension_semantics=("arbitrary",) * n_axes,
                                vmem_limit_bytes=VMEM_LIMIT)


def _full(shape):
    return pl.BlockSpec(shape, lambda *_: (0,) * len(shape))


def _layer_norm(x, g, b):
    mu = jnp.mean(x, axis=-1, keepdims=True)
    xc = x - mu
    var = jnp.mean(xc * xc, axis=-1, keepdims=True)
    return xc * lax.rsqrt(var + LN_EPS) * g + b


def _sigmoid(x):
    return 1.0 / (1.0 + jnp.exp(-x))


def _route(hn, rw_hi_ref, rw_lo_ref, rb_ref, idx_ref, gate_ref):
    hi = hn.astype(BF16)
    lo = (hn - hi.astype(F32)).astype(BF16)
    logits = (jnp.dot(hi, rw_hi_ref[...], preferred_element_type=F32)
              + jnp.dot(hi, rw_lo_ref[...], preferred_element_type=F32)
              + jnp.dot(lo, rw_hi_ref[...], preferred_element_type=F32)
              + rb_ref[...])
    lane = lax.broadcasted_iota(jnp.int32, logits.shape, 1)
    idx_tile = jnp.zeros(logits.shape, jnp.int32)
    val_tile = jnp.zeros(logits.shape, F32)
    top0 = None
    den = None
    for k in range(TOP_K):
        mx = jnp.max(logits, axis=-1, keepdims=True)
        first = jnp.min(jnp.where(logits == mx, lane, LANES), axis=-1, keepdims=True)
        if k == 0:
            top0 = mx
        e = jnp.exp(mx - top0)
        den = e if k == 0 else den + e
        idx_tile = jnp.where(lane == k, first, idx_tile)
        val_tile = jnp.where(lane == k, e, val_tile)
        logits = jnp.where(lane == first, -jnp.inf, logits)
    idx_ref[...] = idx_tile
    gate_ref[...] = val_tile / den


def _inproj_conv_kernel(x_ref, w_ref, cw_ref, yconv_ref, u_ref, carry_ref, *, tiles_per_seq):
    @pl.when(pl.program_id(0) % tiles_per_seq == 0)
    def _():
        carry_ref[...] = jnp.zeros_like(carry_ref)

    proj = jnp.dot(x_ref[...].astype(BF16), w_ref[...], preferred_element_type=F32)
    gate_b = proj[:, :CONV_CH]
    gate_c = proj[:, CONV_CH:2 * CONV_CH]
    hid = proj[:, 2 * CONV_CH:3 * CONV_CH]
    v = gate_c * hid
    rows = v.shape[0]
    row = lax.broadcasted_iota(jnp.int32, v.shape, 0)
    prev1 = carry_ref[7:8, :]
    prev2 = carry_ref[6:7, :]
    vm1 = jnp.where(row == 0, prev1, pltpu.roll(v, 1, 0))
    vm2 = jnp.where(row == 0, prev2, jnp.where(row == 1, prev1, pltpu.roll(v, 2, 0)))
    conv = cw_ref[0:1, :] * vm2 + cw_ref[1:2, :] * vm1 + cw_ref[2:3, :] * v
    yconv_ref[...] = gate_b * conv
    u_ref[...] = proj[:, 3 * CONV_CH:]
    carry_ref[...] = v[rows - 8:, :]


def _inproj_conv(h, w_in_bf, conv_w, seq_len):
    t = h.shape[0]
    tiles_per_seq = seq_len // ROW_TILE
    return pl.pallas_call(
        functools.partial(_inproj_conv_kernel, tiles_per_seq=tiles_per_seq),
        grid=(t // ROW_TILE,),
        in_specs=[pl.BlockSpec((ROW_TILE, D_MODEL), lambda i: (i, 0)),
                  _full(w_in_bf.shape), _full(conv_w.shape)],
        out_specs=[pl.BlockSpec((ROW_TILE, CONV_CH), lambda i: (i, 0)),
                   pl.BlockSpec((ROW_TILE, SSM_WIDTH), lambda i: (i, 0))],
        out_shape=[jax.ShapeDtypeStruct((t, CONV_CH), F32),
                   jax.ShapeDtypeStruct((t, SSM_WIDTH), F32)],
        scratch_shapes=[pltpu.VMEM((8, CONV_CH), F32)],
        compiler_params=_params(),
    )(h, w_in_bf, conv_w)


def _s5_tables(a_re, a_im, log_dt, b_re, b_im, c_re, c_im):
    q = SSM_CHUNK
    lam_re = jnp.minimum(a_re, -1e-4)
    lam_im = a_im
    dt = jnp.exp(log_dt)[:, None]
    mag = jnp.exp(lam_re * dt)
    ab_re = mag * jnp.cos(lam_im * dt)
    ab_im = mag * jnp.sin(lam_im * dt)
    nr, ni = ab_re - 1.0, ab_im
    den = lam_re * lam_re + lam_im * lam_im
    coef_re = ((nr * lam_re + ni * lam_im) / den)[..., None]
    coef_im = ((ni * lam_re - nr * lam_im) / den)[..., None]
    bb_re = coef_re * b_re - coef_im * b_im
    bb_im = coef_re * b_im + coef_im * b_re
    j = jnp.arange(q + 1, dtype=F32)
    pmag = jnp.exp((lam_re * dt)[..., None] * j)
    pang = (lam_im * dt)[..., None] * j
    pw_re = pmag * jnp.cos(pang)
    pw_im = pmag * jnp.sin(pang)
    ca_re = c_re[..., None] * pw_re[:, None] - c_im[..., None] * pw_im[:, None]
    ca_im = c_re[..., None] * pw_im[:, None] + c_im[..., None] * pw_re[:, None]
    kern = (jnp.einsum("gapj,gph->gjah", ca_re, bb_re, precision="highest")
            - jnp.einsum("gapj,gph->gjah", ca_im, bb_im, precision="highest"))[:, :q]
    lag = jnp.arange(q)[None, :] - jnp.arange(q)[:, None]
    toep = kern[:, jnp.clip(lag, 0, q - 1)]
    toep = jnp.where((lag >= 0)[None, :, :, None, None], toep, 0.0)
    toep = toep.transpose(0, 1, 4, 2, 3).reshape(SSM_GROUPS, q * SSM_GROUP, q * SSM_GROUP)
    rev = q - 1 - jnp.arange(q)
    pr = pw_re[:, :, rev]
    pi = pw_im[:, :, rev]
    inj_re = pr[..., None] * bb_re[:, :, None] - pi[..., None] * bb_im[:, :, None]
    inj_im = pr[..., None] * bb_im[:, :, None] + pi[..., None] * bb_re[:, :, None]
    inj_re = inj_re.transpose(0, 2, 3, 1).reshape(SSM_GROUPS, q * SSM_GROUP, SSM_STATE)
    inj_im = inj_im.transpose(0, 2, 3, 1).reshape(SSM_GROUPS, q * SSM_GROUP, SSM_STATE)
    out_re = ca_re[..., 1:].transpose(0, 2, 3, 1).reshape(SSM_GROUPS, SSM_STATE, q * SSM_GROUP)
    out_im = (-ca_im[..., 1:]).transpose(0, 2, 3, 1).reshape(SSM_GROUPS, SSM_STATE, q * SSM_GROUP)
    aq = jnp.stack([pw_re[..., q], pw_im[..., q]], axis=1)
    return (toep.astype(BF16), inj_re.astype(BF16), inj_im.astype(BF16),
            out_re.astype(BF16), out_im.astype(BF16), aq)


def _s5_kernel(u_ref, toep_ref, injr_ref, inji_ref, outr_ref, outi_ref, aq_ref, y_ref,
               ir_ref, ii_ref, xr_ref, xi_ref, *, n_chunks, batch):
    u = u_ref[0]
    ir_ref[...] = jnp.dot(u, injr_ref[0], preferred_element_type=F32)
    ii_ref[...] = jnp.dot(u, inji_ref[0], preferred_element_type=F32)
    aq_re = aq_ref[0, 0:1, :]
    aq_im = aq_ref[0, 1:2, :]
    xr = jnp.zeros((batch, SSM_STATE), F32)
    xi = jnp.zeros((batch, SSM_STATE), F32)
    for c in range(n_chunks):
        rows = slice(c * batch, (c + 1) * batch)
        xr_ref[rows, :] = xr
        xi_ref[rows, :] = xi
        xr, xi = (aq_re * xr - aq_im * xi + ir_ref[rows, :],
                  aq_re * xi + aq_im * xr + ii_ref[rows, :])
    y = jnp.dot(u, toep_ref[0], preferred_element_type=F32)
    y = y + jnp.dot(xr_ref[...].astype(BF16), outr_ref[0], preferred_element_type=F32)
    y = y + jnp.dot(xi_ref[...].astype(BF16), outi_ref[0], preferred_element_type=F32)
    y_ref[0] = y


def _s5_scan(u_g, tables, n_chunks, batch):
    toep, inj_re, inj_im, out_re, out_im, aq = tables
    rows = n_chunks * batch
    width = SSM_CHUNK * SSM_GROUP

    def grp(shape):
        return pl.BlockSpec((1,) + shape, lambda g: (g, 0, 0))

    return pl.pallas_call(
        functools.partial(_s5_kernel, n_chunks=n_chunks, batch=batch),
        grid=(SSM_GROUPS,),
        in_specs=[grp((rows, width)), grp((width, width)), grp((width, SSM_STATE)),
                  grp((width, SSM_STATE)), grp((SSM_STATE, width)), grp((SSM_STATE, width)),
                  grp((2, SSM_STATE))],
        out_specs=grp((rows, width)),
        out_shape=jax.ShapeDtypeStruct((SSM_GROUPS, rows, width), F32),
        scratch_shapes=[pltpu.VMEM((rows, SSM_STATE), F32)] * 4,
        compiler_params=_params(),
    )(u_g, toep, inj_re, inj_im, out_re, out_im, aq)


def _even_post_kernel(h_ref, yc_ref, ys_ref, u_ref, d_ref, wglu_ref, bglu_ref, woc_ref, wos_ref,
                      g_ref, b_ref, rwh_ref, rwl_ref, rb_ref, hn_ref, idx_ref, gate_ref):
    y = ys_ref[...] + d_ref[...] * u_ref[...]
    z = jax.nn.gelu(y)
    glu = jnp.dot(z.astype(BF16), wglu_ref[...], preferred_element_type=F32) + bglu_ref[...]
    z = z * _sigmoid(glu)
    mix = (jnp.dot(yc_ref[...].astype(BF16), woc_ref[...], preferred_element_type=F32)
           + jnp.dot(z.astype(BF16), wos_ref[...], preferred_element_type=F32))
    hn = _layer_norm(DN_ALPHA * h_ref[...] + mix, g_ref[...], b_ref[...])
    hn_ref[...] = hn
    _route(hn, rwh_ref, rwl_ref, rb_ref, idx_ref, gate_ref)


def _odd_post_kernel(h_ref, o_ref, wo_ref, g_ref, b_ref, rwh_ref, rwl_ref, rb_ref,
                     hn_ref, idx_ref, gate_ref):
    mix = jnp.dot(o_ref[...], wo_ref[...], preferred_element_type=F32)
    hn = _layer_norm(DN_ALPHA * h_ref[...] + mix, g_ref[...], b_ref[...])
    hn_ref[...] = hn
    _route(hn, rwh_ref, rwl_ref, rb_ref, idx_ref, gate_ref)


def _post_call(kernel_fn, row_inputs, const_inputs, t):
    def rows(a):
        return pl.BlockSpec((ROW_TILE, a.shape[1]), lambda i: (i, 0))

    return pl.pallas_call(
        kernel_fn,
        grid=(t // ROW_TILE,),
        in_specs=[rows(a) for a in row_inputs] + [_full(a.shape) for a in const_inputs],
        out_specs=[pl.BlockSpec((ROW_TILE, D_MODEL), lambda i: (i, 0)),
                   pl.BlockSpec((ROW_TILE, LANES), lambda i: (i, 0)),
                   pl.BlockSpec((ROW_TILE, LANES), lambda i: (i, 0))],
        out_shape=[jax.ShapeDtypeStruct((t, D_MODEL), F32),
                   jax.ShapeDtypeStruct((t, LANES), jnp.int32),
                   jax.ShapeDtypeStruct((t, LANES), F32)],
        compiler_params=_params(),
    )(*row_inputs, *const_inputs)


def _qkv_rope_kernel(x_ref, w_ref, cos_ref, sin_ref, qkv_ref):
    qkv = jnp.dot(x_ref[...].astype(BF16), w_ref[...], preferred_element_type=F32)
    rows = qkv.shape[0]
    cos = jnp.concatenate([cos_ref[...]] * (D_MODEL // LANES), axis=1)
    sin = jnp.concatenate([sin_ref[...]] * (D_MODEL // LANES), axis=1)
    lane = lax.broadcasted_iota(jnp.int32, (rows, D_MODEL), 1)
    low_half = (lane % HEAD_DIM) < (HEAD_DIM // 2)

    def rope(xs):
        up = pltpu.roll(xs, D_MODEL - HEAD_DIM // 2, 1)
        down = pltpu.roll(xs, HEAD_DIM // 2, 1)
        return xs * cos + jnp.where(low_half, up, down) * sin

    qkv_ref[:, :D_MODEL] = (rope(qkv[:, :D_MODEL]) * (HEAD_DIM ** -0.5)).astype(BF16)
    qkv_ref[:, D_MODEL:2 * D_MODEL] = rope(qkv[:, D_MODEL:2 * D_MODEL]).astype(BF16)
    qkv_ref[:, 2 * D_MODEL:] = qkv[:, 2 * D_MODEL:].astype(BF16)


def _rope_tables(seq_len):
    half = HEAD_DIM // 2
    inv = ROPE_THETA ** (-jnp.arange(half, dtype=F32) / half)
    ang = jnp.arange(seq_len, dtype=F32)[:, None] * inv[None, :]
    cos = jnp.tile(jnp.cos(ang), (1, LANES // half))
    sin = jnp.sin(ang)
    sin = jnp.tile(jnp.concatenate([-sin, sin], axis=1), (1, LANES // HEAD_DIM))
    return cos, sin


def _qkv_rope(h, w_qkv_bf, cos, sin, seq_len):
    t = h.shape[0]
    tiles_per_seq = seq_len // ROW_TILE
    return pl.pallas_call(
        _qkv_rope_kernel,
        grid=(t // ROW_TILE,),
        in_specs=[pl.BlockSpec((ROW_TILE, D_MODEL), lambda i: (i, 0)),
                  _full(w_qkv_bf.shape),
                  pl.BlockSpec((ROW_TILE, LANES), lambda i: (i % tiles_per_seq, 0)),
                  pl.BlockSpec((ROW_TILE, LANES), lambda i: (i % tiles_per_seq, 0))],
        out_specs=pl.BlockSpec((ROW_TILE, 3 * D_MODEL), lambda i: (i, 0)),
        out_shape=jax.ShapeDtypeStruct((t, 3 * D_MODEL), BF16),
        compiler_params=_params(),
    )(h, w_qkv_bf, cos, sin)


def _attn_kernel(*refs, first, last):
    if first:
        q_ref, kp_ref, kc_ref, vp_ref, vc_ref, acc_out, ml_out = refs
    elif last:
        q_ref, kp_ref, kc_ref, vp_ref, vc_ref, acc_in, ml_in, o_out = refs
    else:
        q_ref, kp_ref, kc_ref, vp_ref, vc_ref, acc_in, ml_in, acc_out, ml_out = refs
    blk = ATT_BLOCK
    n = pl.program_id(2)
    qi = lax.broadcasted_iota(jnp.int32, (blk, 2 * blk), 0)
    kj = lax.broadcasted_iota(jnp.int32, (blk, 2 * blk), 1)
    dist = blk + qi - kj
    k_min = jnp.where(n == 0, blk, 0)
    bias = jnp.where(dist >= 0, jnp.where(dist <= blk, jnp.where(kj >= k_min, 0.0, NEG_BIG),
                                          NEG_BIG), NEG_BIG)
    lane = lax.broadcasted_iota(jnp.int32, (blk, LANES), 1)
    low = lane < HEAD_DIM
    ml_prev = None if first else ml_in[0]
    ml_new = jnp.zeros((blk, LANES), F32)
    for hp in range(N_HEADS // 2):
        cols = slice(hp * LANES, (hp + 1) * LANES)
        q2 = q_ref[0, :, cols]
        kk = jnp.concatenate([kp_ref[0, :, cols], kc_ref[0, :, cols]], axis=0)
        vv = jnp.concatenate([vp_ref[0, :, cols], vc_ref[0, :, cols]], axis=0)
        acc_prev = None if first else acc_in[0, :, cols]
        halves = []
        for hh in range(2):
            head = 2 * hp + hh
            qm = jnp.where(low if hh == 0 else jnp.logical_not(low), q2, jnp.zeros_like(q2))
            s = lax.dot_general(qm, kk, (((1,), (1,)), ((), ())),
                                preferred_element_type=F32) + bias
            row_max = jnp.max(s, axis=-1, keepdims=True)
            if first:
                m_new = row_max
            else:
                m_prev = ml_prev[:, head:head + 1]
                l_prev = ml_prev[:, N_HEADS + head:N_HEADS + head + 1]
                m_new = jnp.maximum(m_prev, row_max)
            p = jnp.exp(s - m_new)
            l_new = jnp.sum(p, axis=-1, keepdims=True)
            pv = jnp.dot(p.astype(BF16), vv, preferred_element_type=F32)
            if not first:
                corr = jnp.exp(m_prev - m_new)
                l_new = l_new + corr * l_prev
                pv = pv + corr * acc_prev
            if last:
                pv = pv / l_new
            else:
                ml_new = jnp.where(lane == head, m_new, ml_new)
                ml_new = jnp.where(lane == N_HEADS + head, l_new, ml_new)
            halves.append(pv)
        out = jnp.where(low, halves[0], halves[1])
        if last:
            o_out[0, :, cols] = out.astype(BF16)
        else:
            acc_out[0, :, cols] = out
    if not last:
        ml_out[0] = ml_new


def _attn_pattern(qkv, acc, ml, batch, seq_len, dil, first, last):
    m = seq_len // dil
    nb = m // ATT_BLOCK
    qkv_r = qkv.reshape(batch, m, dil * 3 * D_MODEL)

    def col(which):
        return lambda b, r, n: (b, n, 3 * r + which)

    def col_prev(which):
        return lambda b, r, n: (b, jnp.maximum(n - 1, 0), 3 * r + which)

    blk = (1, ATT_BLOCK, D_MODEL)
    in_specs = [pl.BlockSpec(blk, col(0)), pl.BlockSpec(blk, col_prev(1)), pl.BlockSpec(blk, col(1)),
                pl.BlockSpec(blk, col_prev(2)), pl.BlockSpec(blk, col(2))]
    inputs = [qkv_r] * 5
    acc_spec = pl.BlockSpec(blk, lambda b, r, n: (b, n, r))
    ml_spec = pl.BlockSpec((1, ATT_BLOCK, LANES), lambda b, r, n: (b, n, r))
    if not first:
        in_specs += [acc_spec, ml_spec]
        inputs += [acc.reshape(batch, m, dil * D_MODEL), ml.reshape(batch, m, dil * LANES)]
    if last:
        out_specs = acc_spec
        out_shape = jax.ShapeDtypeStruct((batch, m, dil * D_MODEL), BF16)
    else:
        out_specs = [acc_spec, ml_spec]
        out_shape = [jax.ShapeDtypeStruct((batch, m, dil * D_MODEL), F32),
                     jax.ShapeDtypeStruct((batch, m, dil * LANES), F32)]
    res = pl.pallas_call(
        functools.partial(_attn_kernel, first=first, last=last),
        grid=(batch, dil, nb),
        in_specs=in_specs, out_specs=out_specs, out_shape=out_shape,
        compiler_params=_params(3),
    )(*inputs)
    if last:
        return res.reshape(batch * seq_len, D_MODEL)
    return res[0].reshape(batch * seq_len, D_MODEL), res[1].reshape(batch * seq_len, LANES)


def _moe_kernel(be_ref, nused_ref, x_ref, wgu_ref, bgu_ref, wd_ref, bd_ref, y_ref):
    del be_ref

    @pl.when(pl.program_id(0) < nused_ref[0])
    def _():
        hid = jnp.dot(x_ref[...], wgu_ref[0], preferred_element_type=F32) + bgu_ref[0]
        gate = jnp.minimum(hid[:, :D_MODEL], SWIGLU_LIMIT)
        lin = jnp.clip(hid[:, D_MODEL:], -SWIGLU_LIMIT, SWIGLU_LIMIT)
        act = (lin + 1.0) * (gate * _sigmoid(SWIGLU_ALPHA * gate))
        y_ref[...] = jnp.dot(act.astype(BF16), wd_ref[0], preferred_element_type=F32) + bd_ref[0]

    @pl.when(pl.program_id(0) >= nused_ref[0])
    def _():
        y_ref[...] = jnp.zeros_like(y_ref)


def _moe_experts(xs, block_e, n_used, w_gu_bf, b_gu, w_down_bf, b_down):
    n_rows = xs.shape[0]
    n_blocks = n_rows // MOE_ROWS
    grid_spec = pltpu.PrefetchScalarGridSpec(
        num_scalar_prefetch=2,
        grid=(n_blocks,),
        in_specs=[pl.BlockSpec((MOE_ROWS, D_MODEL), lambda i, be, nu: (i, 0)),
                  pl.BlockSpec((1, D_MODEL, 2 * D_MODEL), lambda i, be, nu: (be[i], 0, 0)),
                  pl.BlockSpec((1, 1, 2 * D_MODEL), lambda i, be, nu: (be[i], 0, 0)),
                  pl.BlockSpec((1, D_MODEL, D_MODEL), lambda i, be, nu: (be[i], 0, 0)),
                  pl.BlockSpec((1, 1, D_MODEL), lambda i, be, nu: (be[i], 0, 0))],
        out_specs=pl.BlockSpec((MOE_ROWS, D_MODEL), lambda i, be, nu: (i, 0)),
    )
    return pl.pallas_call(
        _moe_kernel,
        grid_spec=grid_spec,
        out_shape=jax.ShapeDtypeStruct((n_rows, D_MODEL), F32),
        compiler_params=_params(),
    )(block_e, n_used, xs, w_gu_bf, b_gu[:, None, :], w_down_bf, b_down[:, None, :])


def _routing_tables(top_idx, t):
    n_assign = t * TOP_K
    flat_e = top_idx.reshape(-1)
    onehot = (flat_e[:, None] == jnp.arange(N_EXPERTS, dtype=jnp.int32)[None, :]).astype(jnp.int32)
    csum = jnp.cumsum(onehot, axis=0)
    rank = jnp.take_along_axis(csum, flat_e[:, None], axis=1)[:, 0] - 1
    counts = csum[-1]
    padded = (counts + MOE_ROWS - 1) // MOE_ROWS * MOE_ROWS
    pad_end = jnp.cumsum(padded)
    pad_start = pad_end - padded
    dest = pad_start[flat_e] + rank
    n_blocks = n_assign // MOE_ROWS + N_EXPERTS
    flat_tok = jnp.arange(n_assign, dtype=jnp.int32) // TOP_K
    row_tok = jnp.zeros((n_blocks * MOE_ROWS,), jnp.int32).at[dest].set(flat_tok)
    block_start = jnp.arange(n_blocks, dtype=jnp.int32) * MOE_ROWS
    block_e = jnp.minimum(jnp.searchsorted(pad_end, block_start, side="right"),
                          N_EXPERTS - 1).astype(jnp.int32)
    n_used = (pad_end[-1:] // MOE_ROWS).astype(jnp.int32)
    return row_tok, dest.reshape(t, TOP_K), block_e, n_used


def _ffn_ln_kernel(h_ref, y4_ref, gate_ref, g_ref, b_ref, o_ref):
    gates = gate_ref[...]
    acc = DN_ALPHA * h_ref[...]
    for k in range(TOP_K):
        acc = acc + gates[:, k:k + 1] * y4_ref[:, k * D_MODEL:(k + 1) * D_MODEL]
    o_ref[...] = _layer_norm(acc, g_ref[...], b_ref[...])


def _ffn_ln(h, y4, gates, g, b):
    t = h.shape[0]
    return pl.pallas_call(
        _ffn_ln_kernel,
        grid=(t // ROW_TILE,),
        in_specs=[pl.BlockSpec((ROW_TILE, D_MODEL), lambda i: (i, 0)),
                  pl.BlockSpec((ROW_TILE, TOP_K * D_MODEL), lambda i: (i, 0)),
                  pl.BlockSpec((ROW_TILE, LANES), lambda i: (i, 0)),
                  _full(g.shape), _full(b.shape)],
        out_specs=pl.BlockSpec((ROW_TILE, D_MODEL), lambda i: (i, 0)),
        out_shape=jax.ShapeDtypeStruct((t, D_MODEL), F32),
        compiler_params=_params(),
    )(h, y4, gates, g, b)


def _moe_layer(h, hn, idx_tile, gate_tile, w_gu, b_gu, w_down, b_down, ln_g, ln_b):
    del h
    t = hn.shape[0]
    row_tok, dest, block_e, n_used = _routing_tables(idx_tile[:, :TOP_K], t)
    xs = jnp.take(hn.astype(BF16), row_tok, axis=0)
    ys = _moe_experts(xs, block_e, n_used, w_gu.astype(BF16), b_gu, w_down.astype(BF16), b_down)
    y4 = jnp.take(ys, dest.reshape(-1), axis=0).reshape(t, TOP_K * D_MODEL)
    return _ffn_ln(hn, y4, gate_tile, ln_g[None, :], ln_b[None, :])


def _router_consts(router_w, router_b):
    w = jnp.zeros((D_MODEL, LANES), F32).at[:, :N_EXPERTS].set(router_w)
    w_hi = w.astype(BF16)
    w_lo = (w - w_hi.astype(F32)).astype(BF16)
    b = jnp.full((1, LANES), NEG_BIG, F32).at[0, :N_EXPERTS].set(router_b)
    return w_hi, w_lo, b


def kernel(x, hy_w_in, conv_w, ssm_a_re, ssm_a_im, ssm_log_dt, ssm_b_re, ssm_b_im, ssm_c_re,
           ssm_c_im, ssm_d, ssm_w_glu, ssm_b_glu, hy_w_out, att_w_qkv, att_w_o, ln_mix_g,
           ln_mix_b, ln_ffn_g, ln_ffn_b, router_w, router_b, expert_w_gu, expert_b_gu,
           expert_w_down, expert_b_down):
    batch, seq_len, _ = x.shape
    t = batch * seq_len
    n_chunks = seq_len // SSM_CHUNK
    h = x.reshape(t, D_MODEL)
    cos, sin = _rope_tables(seq_len)
    for layer in range(DEPTH):
        i = layer // 2
        route = _router_consts(router_w[layer], router_b[layer])
        ln = (ln_mix_g[layer][None, :], ln_mix_b[layer][None, :])
        if layer % 2 == 0:
            y_conv, u = _inproj_conv(h, hy_w_in[i].astype(BF16), conv_w[i], seq_len)
            tables = _s5_tables(ssm_a_re[i], ssm_a_im[i], ssm_log_dt[i], ssm_b_re[i], ssm_b_im[i],
                                ssm_c_re[i], ssm_c_im[i])
            u_g = u.astype(BF16).reshape(batch, n_chunks, SSM_CHUNK, SSM_GROUPS, SSM_GROUP)
            u_g = u_g.transpose(3, 1, 0, 2, 4).reshape(SSM_GROUPS, n_chunks * batch,
                                                       SSM_CHUNK * SSM_GROUP)
            y_g = _s5_scan(u_g, tables, n_chunks, batch)
            y_s = y_g.reshape(SSM_GROUPS, n_chunks, batch, SSM_CHUNK, SSM_GROUP)
            y_s = y_s.transpose(2, 1, 3, 0, 4).reshape(t, SSM_WIDTH)
            w_out = hy_w_out[i].astype(BF16)
            hn, idx_tile, gate_tile = _post_call(
                _even_post_kernel, [h, y_conv, y_s, u],
                [ssm_d[i].reshape(1, SSM_WIDTH), ssm_w_glu[i].astype(BF16),
                 ssm_b_glu[i][None, :], w_out[:CONV_CH], w_out[CONV_CH:], *ln, *route], t)
        else:
            qkv = _qkv_rope(h, att_w_qkv[i].astype(BF16), cos, sin, seq_len)
            acc, ml = _attn_pattern(qkv, None, None, batch, seq_len, DILATIONS[0], True, False)
            acc, ml = _attn_pattern(qkv, acc, ml, batch, seq_len, DILATIONS[1], False, False)
            o = _attn_pattern(qkv, acc, ml, batch, seq_len, DILATIONS[2], False, True)
            hn, idx_tile, gate_tile = _post_call(
                _odd_post_kernel, [h, o], [att_w_o[i].astype(BF16), *ln, *route], t)
        h = _moe_layer(h, hn, idx_tile, gate_tile, expert_w_gu[layer], expert_b_gu[layer],
                       expert_w_down[layer], expert_b_down[layer], ln_ffn_g[layer], ln_ffn_b[layer])
    return h.reshape(batch, seq_len, D_MODEL)
```

```python
import functools

import jax
import jax.numpy as jnp
from jax import lax
from jax.experimental import pallas as pl
from jax.experimental.pallas import tpu as pltpu

F32 = jnp.float32
BF16 = jnp.bfloat16

D_MODEL = 1024
DEPTH = 4
CONV_CH = 512
SSM_WIDTH = 512
SSM_GROUP = 16
SSM_GROUPS = 32
SSM_STATE = 64
N_HEADS = 16
HEAD_DIM = 64
ROPE_THETA = 10000.0
DILATIONS = (1, 4, 16)
ATT_BLOCK = 128
N_EXPERTS = 32
TOP_K = 4
SWIGLU_LIMIT = 7.0
SWIGLU_ALPHA = 1.702
DN_ALPHA = (2 * DEPTH) ** 0.25
LN_EPS = 1e-5

LANES = 128
ROW_TILE = 512
QKV_TILE = 256
SSM_CHUNK = 32
MOE_ROWS = 512
VMEM_LIMIT = 56 * 1024 * 1024
NEG_BIG = -1e30


def _params(n_axes=1):
    return pltpu.CompilerParams(dimension_semantics=("arbitrary",) * n_axes,
                                vmem_limit_bytes=VMEM_LIMIT)


def _full(shape):
    return pl.BlockSpec(shape, lambda *_: (0,) * len(shape))


def _rows(width, tile=ROW_TILE):
    return pl.BlockSpec((tile, width), lambda i: (i, 0))


def _layer_norm(x, g, b):
    mu = jnp.mean(x, axis=-1, keepdims=True)
    xc = x - mu
    var = jnp.mean(xc * xc, axis=-1, keepdims=True)
    return xc * lax.rsqrt(var + LN_EPS) * g + b


def _sigmoid(x):
    return 1.0 / (1.0 + jnp.exp(-x))


def _split_bf16(x):
    hi = x.astype(BF16)
    return hi, (x - hi.astype(F32)).astype(BF16)


def _route(hn, rw_hi_ref, rw_lo_ref, rb_ref, idx_ref, gate_ref):
    hi, lo = _split_bf16(hn)
    logits = (jnp.dot(hi, rw_hi_ref[...], preferred_element_type=F32)
              + jnp.dot(hi, rw_lo_ref[...], preferred_element_type=F32)
              + jnp.dot(lo, rw_hi_ref[...], preferred_element_type=F32)
              + rb_ref[...])
    lane = lax.broadcasted_iota(jnp.int32, logits.shape, 1)
    idx_tile = jnp.zeros(logits.shape, jnp.int32)
    val_tile = jnp.zeros(logits.shape, F32)
    top0 = None
    den = None
    for k in range(TOP_K):
        mx = jnp.max(logits, axis=-1, keepdims=True)
        first = jnp.min(jnp.where(logits == mx, lane, LANES), axis=-1, keepdims=True)
        if k == 0:
            top0 = mx
        e = jnp.exp(mx - top0)
        den = e if k == 0 else den + e
        idx_tile = jnp.where(lane == k, first, idx_tile)
        val_tile = jnp.where(lane == k, e, val_tile)
        logits = jnp.where(lane == first, -jnp.inf, logits)
    idx_ref[...] = idx_tile
    gate_ref[...] = val_tile / den


def _post_outputs(t):
    specs = [_rows(D_MODEL), _rows(D_MODEL), _rows(LANES), _rows(LANES)]
    shapes = [jax.ShapeDtypeStruct((t, D_MODEL), F32), jax.ShapeDtypeStruct((t, D_MODEL), BF16),
              jax.ShapeDtypeStruct((t, LANES), jnp.int32), jax.ShapeDtypeStruct((t, LANES), F32)]
    return specs, shapes


def _norm_and_route(pre, g_ref, b_ref, rwh_ref, rwl_ref, rb_ref, hn_ref, hnb_ref, idx_ref, gate_ref):
    hn = _layer_norm(pre, g_ref[...], b_ref[...])
    hn_ref[...] = hn
    hnb_ref[...] = hn.astype(BF16)
    _route(hn, rwh_ref, rwl_ref, rb_ref, idx_ref, gate_ref)


def _inproj_conv_kernel(x_ref, w_ref, cw_ref, yconv_ref, u_ref, carry_ref, *, tiles_per_seq):
    @pl.when(pl.program_id(0) % tiles_per_seq == 0)
    def _():
        carry_ref[...] = jnp.zeros_like(carry_ref)

    proj = jnp.dot(x_ref[...].astype(BF16), w_ref[...], preferred_element_type=F32)
    gate_b = proj[:, :CONV_CH]
    gate_c = proj[:, CONV_CH:2 * CONV_CH]
    hid = proj[:, 2 * CONV_CH:3 * CONV_CH]
    v = gate_c * hid
    rows = v.shape[0]
    row = lax.broadcasted_iota(jnp.int32, v.shape, 0)
    prev1 = carry_ref[7:8, :]
    prev2 = carry_ref[6:7, :]
    vm1 = jnp.where(row == 0, prev1, pltpu.roll(v, 1, 0))
    vm2 = jnp.where(row == 0, prev2, jnp.where(row == 1, prev1, pltpu.roll(v, 2, 0)))
    conv = cw_ref[0:1, :] * vm2 + cw_ref[1:2, :] * vm1 + cw_ref[2:3, :] * v
    yconv_ref[...] = gate_b * conv
    u_ref[...] = proj[:, 3 * CONV_CH:]
    carry_ref[...] = v[rows - 8:, :]


def _inproj_conv(h, w_in_bf, conv_w, seq_len):
    t = h.shape[0]
    return pl.pallas_call(
        functools.partial(_inproj_conv_kernel, tiles_per_seq=seq_len // ROW_TILE),
        grid=(t // ROW_TILE,),
        in_specs=[_rows(D_MODEL), _full(w_in_bf.shape), _full(conv_w.shape)],
        out_specs=[_rows(CONV_CH), _rows(SSM_WIDTH)],
        out_shape=[jax.ShapeDtypeStruct((t, CONV_CH), F32),
                   jax.ShapeDtypeStruct((t, SSM_WIDTH), F32)],
        scratch_shapes=[pltpu.VMEM((8, CONV_CH), F32)],
        compiler_params=_params(),
    )(h, w_in_bf, conv_w)


def _s5_tables(a_re, a_im, log_dt, b_re, b_im, c_re, c_im):
    q = SSM_CHUNK
    lam_re = jnp.minimum(a_re, -1e-4)
    lam_im = a_im
    dt = jnp.exp(log_dt)[:, None]
    mag = jnp.exp(lam_re * dt)
    ab_re = mag * jnp.cos(lam_im * dt)
    ab_im = mag * jnp.sin(lam_im * dt)
    nr, ni = ab_re - 1.0, ab_im
    den = lam_re * lam_re + lam_im * lam_im
    coef_re = ((nr * lam_re + ni * lam_im) / den)[..., None]
    coef_im = ((ni * lam_re - nr * lam_im) / den)[..., None]
    bb_re = coef_re * b_re - coef_im * b_im
    bb_im = coef_re * b_im + coef_im * b_re
    j = jnp.arange(q + 1, dtype=F32)
    pmag = jnp.exp((lam_re * dt)[..., None] * j)
    pang = (lam_im * dt)[..., None] * j
    pw_re = pmag * jnp.cos(pang)
    pw_im = pmag * jnp.sin(pang)
    ca_re = c_re[..., None] * pw_re[:, None] - c_im[..., None] * pw_im[:, None]
    ca_im = c_re[..., None] * pw_im[:, None] + c_im[..., None] * pw_re[:, None]
    kern = (jnp.einsum("gapj,gph->gjah", ca_re, bb_re, precision="highest")
            - jnp.einsum("gapj,gph->gjah", ca_im, bb_im, precision="highest"))[:, :q]
    lag = jnp.arange(q)[None, :] - jnp.arange(q)[:, None]
    toep = kern[:, jnp.clip(lag, 0, q - 1)]
    toep = jnp.where((lag >= 0)[None, :, :, None, None], toep, 0.0)
    toep = toep.transpose(0, 1, 4, 2, 3).reshape(SSM_GROUPS, q * SSM_GROUP, q * SSM_GROUP)
    rev = q - 1 - jnp.arange(q)
    pr = pw_re[:, :, rev]
    pi = pw_im[:, :, rev]
    inj_re = pr[..., None] * bb_re[:, :, None] - pi[..., None] * bb_im[:, :, None]
    inj_im = pr[..., None] * bb_im[:, :, None] + pi[..., None] * bb_re[:, :, None]
    inj_re = inj_re.transpose(0, 2, 3, 1).reshape(SSM_GROUPS, q * SSM_GROUP, SSM_STATE)
    inj_im = inj_im.transpose(0, 2, 3, 1).reshape(SSM_GROUPS, q * SSM_GROUP, SSM_STATE)
    out_re = ca_re[..., 1:].transpose(0, 2, 3, 1).reshape(SSM_GROUPS, SSM_STATE, q * SSM_GROUP)
    out_im = (-ca_im[..., 1:]).transpose(0, 2, 3, 1).reshape(SSM_GROUPS, SSM_STATE, q * SSM_GROUP)
    aq = jnp.stack([pw_re[..., q], pw_im[..., q]], axis=1)
    return (toep.astype(BF16), inj_re.astype(BF16), inj_im.astype(BF16),
            out_re.astype(BF16), out_im.astype(BF16), aq)


def _s5_kernel(u_ref, toep_ref, injr_ref, inji_ref, outr_ref, outi_ref, aq_ref, y_ref,
               ir_ref, ii_ref, xr_ref, xi_ref, *, n_chunks, batch):
    u = u_ref[0]
    ir_ref[...] = jnp.dot(u, injr_ref[0], preferred_element_type=F32)
    ii_ref[...] = jnp.dot(u, inji_ref[0], preferred_element_type=F32)
    aq_re = aq_ref[0, 0:1, :]
    aq_im = aq_ref[0, 1:2, :]
    xr = jnp.zeros((batch, SSM_STATE), F32)
    xi = jnp.zeros((batch, SSM_STATE), F32)
    for c in range(n_chunks):
        rows = slice(c * batch, (c + 1) * batch)
        xr_ref[rows, :] = xr
        xi_ref[rows, :] = xi
        xr, xi = (aq_re * xr - aq_im * xi + ir_ref[rows, :],
                  aq_re * xi + aq_im * xr + ii_ref[rows, :])
    y = jnp.dot(u, toep_ref[0], preferred_element_type=F32)
    y = y + jnp.dot(xr_ref[...].astype(BF16), outr_ref[0], preferred_element_type=F32)
    y = y + jnp.dot(xi_ref[...].astype(BF16), outi_ref[0], preferred_element_type=F32)
    y_ref[0] = y


def _s5_scan(u_g, tables, n_chunks, batch):
    toep, inj_re, inj_im, out_re, out_im, aq = tables
    rows = n_chunks * batch
    width = SSM_CHUNK * SSM_GROUP

    def grp(shape):
        return pl.BlockSpec((1,) + shape, lambda g: (g, 0, 0))

    return pl.pallas_call(
        functools.partial(_s5_kernel, n_chunks=n_chunks, batch=batch),
        grid=(SSM_GROUPS,),
        in_specs=[grp((rows, width)), grp((width, width)), grp((width, SSM_STATE)),
                  grp((width, SSM_STATE)), grp((SSM_STATE, width)), grp((SSM_STATE, width)),
                  grp((2, SSM_STATE))],
        out_specs=grp((rows, width)),
        out_shape=jax.ShapeDtypeStruct((SSM_GROUPS, rows, width), F32),
        scratch_shapes=[pltpu.VMEM((rows, SSM_STATE), F32)] * 4,
        compiler_params=_params(),
    )(u_g, toep, inj_re, inj_im, out_re, out_im, aq)


def _even_post_kernel(h_ref, yc_ref, ys_ref, u_ref, d_ref, wglu_ref, bglu_ref, woc_ref, wos_ref,
                      g_ref, b_ref, rwh_ref, rwl_ref, rb_ref, hn_ref, hnb_ref, idx_ref, gate_ref):
    y = ys_ref[...] + d_ref[...] * u_ref[...]
    z = jax.nn.gelu(y)
    glu = jnp.dot(z.astype(BF16), wglu_ref[...], preferred_element_type=F32) + bglu_ref[...]
    z = z * _sigmoid(glu)
    mix = (jnp.dot(yc_ref[...].astype(BF16), woc_ref[...], preferred_element_type=F32)
           + jnp.dot(z.astype(BF16), wos_ref[...], preferred_element_type=F32))
    _norm_and_route(DN_ALPHA * h_ref[...] + mix, g_ref, b_ref, rwh_ref, rwl_ref, rb_ref,
                    hn_ref, hnb_ref, idx_ref, gate_ref)


def _even_post(h, y_conv, y_s, u, consts):
    t = h.shape[0]
    out_specs, out_shape = _post_outputs(t)
    return pl.pallas_call(
        _even_post_kernel,
        grid=(t // ROW_TILE,),
        in_specs=[_rows(D_MODEL), _rows(CONV_CH), _rows(SSM_WIDTH), _rows(SSM_WIDTH)]
        + [_full(a.shape) for a in consts],
        out_specs=out_specs, out_shape=out_shape,
        compiler_params=_params(),
    )(h, y_conv, y_s, u, *consts)


def _qkv_rope_kernel(x_ref, w_ref, cos_ref, sin_ref, o1_ref, o4_ref, o16_ref, stage_ref):
    qkv = jnp.dot(x_ref[...].astype(BF16), w_ref[...], preferred_element_type=F32)
    rows = qkv.shape[0]
    cos = jnp.concatenate([cos_ref[...]] * (D_MODEL // LANES), axis=1)
    sin = jnp.concatenate([sin_ref[...]] * (D_MODEL // LANES), axis=1)
    lane = lax.broadcasted_iota(jnp.int32, (rows, D_MODEL), 1)
    low_half = (lane % HEAD_DIM) < (HEAD_DIM // 2)

    def rope(xs):
        up = pltpu.roll(xs, D_MODEL - HEAD_DIM // 2, 1)
        down = pltpu.roll(xs, HEAD_DIM // 2, 1)
        return xs * cos + jnp.where(low_half, up, down) * sin

    parts = (rope(qkv[:, :D_MODEL]) * (HEAD_DIM ** -0.5), rope(qkv[:, D_MODEL:2 * D_MODEL]),
             qkv[:, 2 * D_MODEL:])
    for which, part in enumerate(parts):
        o1_ref[:, which * D_MODEL:(which + 1) * D_MODEL] = part.astype(BF16)
        for c in range(D_MODEL // LANES):
            stage_ref[which * (D_MODEL // LANES) + c] = part[:, c * LANES:(c + 1) * LANES]
    for dil, ref in ((DILATIONS[1], o4_ref), (DILATIONS[2], o16_ref)):
        for c in range(3 * D_MODEL // LANES):
            for r in range(dil):
                ref[0, r, :, c * LANES:(c + 1) * LANES] = (
                    stage_ref[c, pl.ds(r, rows // dil, stride=dil), :].astype(BF16))


def _rope_tables(seq_len):
    half = HEAD_DIM // 2
    inv = ROPE_THETA ** (-jnp.arange(half, dtype=F32) / half)
    ang = jnp.arange(seq_len, dtype=F32)[:, None] * inv[None, :]
    cos = jnp.tile(jnp.cos(ang), (1, LANES // half))
    sin = jnp.sin(ang)
    sin = jnp.tile(jnp.concatenate([-sin, sin], axis=1), (1, LANES // HEAD_DIM))
    return cos, sin


def _residue_spec(dil, rows, width, tiles_per_seq):
    return pl.BlockSpec((1, dil, rows // dil, width),
                        lambda i: (i // tiles_per_seq, 0, i % tiles_per_seq, 0))


def _qkv_rope(h, w_qkv_bf, cos, sin, batch, seq_len):
    t = h.shape[0]
    tps = seq_len // QKV_TILE
    d4, d16 = DILATIONS[1], DILATIONS[2]
    width = 3 * D_MODEL
    return pl.pallas_call(
        _qkv_rope_kernel,
        grid=(t // QKV_TILE,),
        in_specs=[_rows(D_MODEL, QKV_TILE), _full(w_qkv_bf.shape),
                  pl.BlockSpec((QKV_TILE, LANES), lambda i: (i % tps, 0)),
                  pl.BlockSpec((QKV_TILE, LANES), lambda i: (i % tps, 0))],
        out_specs=[_rows(width, QKV_TILE), _residue_spec(d4, QKV_TILE, width, tps),
                   _residue_spec(d16, QKV_TILE, width, tps)],
        out_shape=[jax.ShapeDtypeStruct((t, width), BF16),
                   jax.ShapeDtypeStruct((batch, d4, seq_len // d4, width), BF16),
                   jax.ShapeDtypeStruct((batch, d16, seq_len // d16, width), BF16)],
        scratch_shapes=[pltpu.VMEM((width // LANES, QKV_TILE, LANES), F32)],
        compiler_params=_params(),
    )(h, w_qkv_bf, cos, sin)


def _attn_kernel(q_ref, kp_ref, kc_ref, vp_ref, vc_ref, o_ref, lse_ref):
    blk = ATT_BLOCK
    qi = lax.broadcasted_iota(jnp.int32, (blk, 2 * blk), 0)
    kj = lax.broadcasted_iota(jnp.int32, (blk, 2 * blk), 1)
    dist = blk + qi - kj
    k_min = jnp.where(pl.program_id(1) == 0, blk, 0)
    bias = jnp.where(dist >= 0, jnp.where(dist <= blk, jnp.where(kj >= k_min, 0.0, NEG_BIG),
                                          NEG_BIG), NEG_BIG)
    lane = lax.broadcasted_iota(jnp.int32, (blk, LANES), 1)
    low = lane < HEAD_DIM
    lse_tile = jnp.zeros((blk, LANES), F32)
    for hp in range(N_HEADS // 2):
        cols = slice(hp * LANES, (hp + 1) * LANES)
        q2 = q_ref[0, :, cols]
        kk = jnp.concatenate([kp_ref[0, :, cols], kc_ref[0, :, cols]], axis=0)
        vv = jnp.concatenate([vp_ref[0, :, cols], vc_ref[0, :, cols]], axis=0)
        halves = []
        for hh in range(2):
            head = 2 * hp + hh
            qm = jnp.where(low if hh == 0 else jnp.logical_not(low), q2, jnp.zeros_like(q2))
            s = lax.dot_general(qm, kk, (((1,), (1,)), ((), ())),
                                preferred_element_type=F32) + bias
            m = jnp.max(s, axis=-1, keepdims=True)
            p = jnp.exp(s - m)
            den = jnp.sum(p, axis=-1, keepdims=True)
            pv = jnp.dot(p.astype(BF16), vv, preferred_element_type=F32)
            halves.append(pv / den)
            lse_tile = jnp.where(lane == head, m + jnp.log(den), lse_tile)
        o_ref[0, :, cols] = jnp.where(low, halves[0], halves[1]).astype(BF16)
    lse_ref[0] = lse_tile


def _attn_pattern(qkv_sub):
    n_sub, sub_len, _ = qkv_sub.shape

    def cur(which):
        return pl.BlockSpec((1, ATT_BLOCK, D_MODEL), lambda s, n: (s, n, which))

    def prev(which):
        return pl.BlockSpec((1, ATT_BLOCK, D_MODEL), lambda s, n: (s, jnp.maximum(n - 1, 0), which))

    return pl.pallas_call(
        _attn_kernel,
        grid=(n_sub, sub_len // ATT_BLOCK),
        in_specs=[cur(0), prev(1), cur(1), prev(2), cur(2)],
        out_specs=[pl.BlockSpec((1, ATT_BLOCK, D_MODEL), lambda s, n: (s, n, 0)),
                   pl.BlockSpec((1, ATT_BLOCK, LANES), lambda s, n: (s, n, 0))],
        out_shape=[jax.ShapeDtypeStruct((n_sub, sub_len, D_MODEL), BF16),
                   jax.ShapeDtypeStruct((n_sub, sub_len, LANES), F32)],
        compiler_params=_params(2),
    )(*([qkv_sub] * 5))


def _odd_post_kernel(h_ref, o1_ref, l1_ref, o4_ref, l4_ref, o16_ref, l16_ref, expand_ref, wo_ref,
                     g_ref, b_ref, rwh_ref, rwl_ref, rb_ref, hn_ref, hnb_ref, idx_ref, gate_ref,
                     os4_ref, ls4_ref, os16_ref, ls16_ref):
    rows = h_ref.shape[0]
    n_col = D_MODEL // LANES
    for dil, o_ref, l_ref, os_ref, ls_ref in ((DILATIONS[1], o4_ref, l4_ref, os4_ref, ls4_ref),
                                              (DILATIONS[2], o16_ref, l16_ref, os16_ref, ls16_ref)):
        for r in range(dil):
            sel = pl.ds(r, rows // dil, stride=dil)
            ls_ref[sel, :] = l_ref[0, r]
            for c in range(n_col):
                os_ref[c, sel, :] = o_ref[0, r, :, c * LANES:(c + 1) * LANES].astype(F32)
    lses = (l1_ref[...], ls4_ref[...], ls16_ref[...])
    outs = (o1_ref[...].astype(F32),
            jnp.concatenate([os4_ref[c] for c in range(n_col)], axis=1),
            jnp.concatenate([os16_ref[c] for c in range(n_col)], axis=1))
    mx = jnp.maximum(jnp.maximum(lses[0], lses[1]), lses[2])
    es = [jnp.exp(l - mx) for l in lses]
    den = es[0] + es[1] + es[2]
    o = jnp.zeros((rows, D_MODEL), F32)
    for e, out in zip(es, outs):
        hi, lo = _split_bf16(e / den)
        w = (jnp.dot(hi, expand_ref[...], preferred_element_type=F32)
             + jnp.dot(lo, expand_ref[...], preferred_element_type=F32))
        o = o + w * out
    mix = jnp.dot(o.astype(BF16), wo_ref[...], preferred_element_type=F32)
    _norm_and_route(DN_ALPHA * h_ref[...] + mix, g_ref, b_ref, rwh_ref, rwl_ref, rb_ref,
                    hn_ref, hnb_ref, idx_ref, gate_ref)


def _odd_post(h, pattern_outs, consts, seq_len):
    t = h.shape[0]
    tps = seq_len // ROW_TILE
    (o1, l1), (o4, l4), (o16, l16) = pattern_outs
    d4, d16 = DILATIONS[1], DILATIONS[2]
    out_specs, out_shape = _post_outputs(t)
    return pl.pallas_call(
        _odd_post_kernel,
        grid=(t // ROW_TILE,),
        in_specs=[_rows(D_MODEL), _rows(D_MODEL), _rows(LANES),
                  _residue_spec(d4, ROW_TILE, D_MODEL, tps), _residue_spec(d4, ROW_TILE, LANES, tps),
                  _residue_spec(d16, ROW_TILE, D_MODEL, tps), _residue_spec(d16, ROW_TILE, LANES, tps)]
        + [_full(a.shape) for a in consts],
        out_specs=out_specs, out_shape=out_shape,
        scratch_shapes=[pltpu.VMEM((D_MODEL // LANES, ROW_TILE, LANES), F32),
                        pltpu.VMEM((ROW_TILE, LANES), F32)] * 2,
        compiler_params=_params(),
    )(h, o1, l1, o4, l4, o16, l16, *consts)


def _attention_layer(h, w_qkv, w_o, cos, sin, consts, batch, seq_len):
    t = h.shape[0]
    qkv1, qkv4, qkv16 = _qkv_rope(h, w_qkv.astype(BF16), cos, sin, batch, seq_len)
    outs = []
    for dil, qkv in zip(DILATIONS, (qkv1, qkv4, qkv16)):
        sub_len = seq_len // dil
        o, lse = _attn_pattern(qkv.reshape(batch * dil, sub_len, 3 * D_MODEL))
        if dil == 1:
            outs.append((o.reshape(t, D_MODEL), lse.reshape(t, LANES)))
        else:
            outs.append((o.reshape(batch, dil, sub_len, D_MODEL),
                         lse.reshape(batch, dil, sub_len, LANES)))
    head_of_lane = jnp.arange(D_MODEL, dtype=jnp.int32) // HEAD_DIM
    expand = (jnp.arange(LANES, dtype=jnp.int32)[:, None] == head_of_lane[None, :]).astype(BF16)
    return _odd_post(h, outs, [expand, w_o.astype(BF16), *consts], seq_len)


def _moe_kernel(be_ref, nused_ref, x_ref, wgu_ref, bgu_ref, wd_ref, bd_ref, y_ref,
                wgu_bf_ref, wd_bf_ref):
    i = pl.program_id(0)
    new_expert = jnp.logical_or(i == 0, be_ref[i] != be_ref[jnp.maximum(i - 1, 0)])

    @pl.when(new_expert)
    def _():
        wgu_bf_ref[...] = wgu_ref[0].astype(BF16)
        wd_bf_ref[...] = wd_ref[0].astype(BF16)

    @pl.when(i < nused_ref[0])
    def _():
        hid = jnp.dot(x_ref[...], wgu_bf_ref[...], preferred_element_type=F32) + bgu_ref[0]
        gate = jnp.minimum(hid[:, :D_MODEL], SWIGLU_LIMIT)
        lin = jnp.clip(hid[:, D_MODEL:], -SWIGLU_LIMIT, SWIGLU_LIMIT)
        act = (lin + 1.0) * (gate * _sigmoid(SWIGLU_ALPHA * gate))
        y = jnp.dot(act.astype(BF16), wd_bf_ref[...], preferred_element_type=F32) + bd_ref[0]
        y_ref[...] = y.astype(BF16)

    @pl.when(i >= nused_ref[0])
    def _():
        y_ref[...] = jnp.zeros_like(y_ref)


def _moe_experts(xs, block_e, n_used, w_gu, b_gu, w_down, b_down):
    n_rows = xs.shape[0]
    grid_spec = pltpu.PrefetchScalarGridSpec(
        num_scalar_prefetch=2,
        grid=(n_rows // MOE_ROWS,),
        in_specs=[pl.BlockSpec((MOE_ROWS, D_MODEL), lambda i, be, nu: (i, 0)),
                  pl.BlockSpec((1, D_MODEL, 2 * D_MODEL), lambda i, be, nu: (be[i], 0, 0)),
                  pl.BlockSpec((1, 1, 2 * D_MODEL), lambda i, be, nu: (be[i], 0, 0)),
                  pl.BlockSpec((1, D_MODEL, D_MODEL), lambda i, be, nu: (be[i], 0, 0)),
                  pl.BlockSpec((1, 1, D_MODEL), lambda i, be, nu: (be[i], 0, 0))],
        out_specs=pl.BlockSpec((MOE_ROWS, D_MODEL), lambda i, be, nu: (i, 0)),
        scratch_shapes=[pltpu.VMEM((D_MODEL, 2 * D_MODEL), BF16), pltpu.VMEM((D_MODEL, D_MODEL), BF16)],
    )
    return pl.pallas_call(
        _moe_kernel,
        grid_spec=grid_spec,
        out_shape=jax.ShapeDtypeStruct((n_rows, D_MODEL), BF16),
        compiler_params=_params(),
    )(block_e, n_used, xs, w_gu, b_gu[:, None, :], w_down, b_down[:, None, :])


def _routing_tables(top_idx, t):
    n_assign = t * TOP_K
    flat_e = top_idx.T.reshape(-1)
    onehot = (flat_e[:, None] == jnp.arange(N_EXPERTS, dtype=jnp.int32)[None, :]).astype(jnp.int32)
    csum = jnp.cumsum(onehot, axis=0)
    counts = csum[-1]
    padded = (counts + MOE_ROWS - 1) // MOE_ROWS * MOE_ROWS
    pad_end = jnp.cumsum(padded)
    pad_start = pad_end - padded
    dest = jnp.sum(onehot * (csum - 1 + pad_start[None, :]), axis=1)
    n_blocks = n_assign // MOE_ROWS + N_EXPERTS
    flat_tok = jnp.arange(n_assign, dtype=jnp.int32) % t
    row_tok = jnp.zeros((n_blocks * MOE_ROWS,), jnp.int32).at[dest].set(
        flat_tok, unique_indices=True, mode="promise_in_bounds")
    block_start = jnp.arange(n_blocks, dtype=jnp.int32) * MOE_ROWS
    block_e = jnp.minimum(jnp.sum((block_start[:, None] >= pad_end[None, :]).astype(jnp.int32), axis=1),
                          N_EXPERTS - 1)
    n_used = (pad_end[-1:] // MOE_ROWS).astype(jnp.int32)
    return row_tok, dest, block_e, n_used


def _ffn_ln_kernel(h_ref, y4_ref, gate_ref, g_ref, b_ref, o_ref):
    gates = gate_ref[...]
    acc = DN_ALPHA * h_ref[...]
    for k in range(TOP_K):
        acc = acc + gates[:, k:k + 1] * y4_ref[k].astype(F32)
    o_ref[...] = _layer_norm(acc, g_ref[...], b_ref[...])


def _ffn_ln(h, y4, gates, g, b):
    t = h.shape[0]
    return pl.pallas_call(
        _ffn_ln_kernel,
        grid=(t // ROW_TILE,),
        in_specs=[_rows(D_MODEL), pl.BlockSpec((TOP_K, ROW_TILE, D_MODEL), lambda i: (0, i, 0)),
                  _rows(LANES), _full(g.shape), _full(b.shape)],
        out_specs=_rows(D_MODEL),
        out_shape=jax.ShapeDtypeStruct((t, D_MODEL), F32),
        compiler_params=_params(),
    )(h, y4, gates, g, b)


def _moe_layer(hn, hn_bf, idx_tile, gate_tile, w_gu, b_gu, w_down, b_down, ln_g, ln_b):
    t = hn.shape[0]
    row_tok, dest, block_e, n_used = _routing_tables(idx_tile[:, :TOP_K], t)
    xs = hn_bf.at[row_tok].get(mode="promise_in_bounds")
    ys = _moe_experts(xs, block_e, n_used, w_gu, b_gu, w_down, b_down)
    y4 = ys.at[dest].get(mode="promise_in_bounds").reshape(TOP_K, t, D_MODEL)
    return _ffn_ln(hn, y4, gate_tile, ln_g[None, :], ln_b[None, :])


def _router_consts(router_w, router_b):
    w = jnp.zeros((D_MODEL, LANES), F32).at[:, :N_EXPERTS].set(router_w)
    w_hi, w_lo = _split_bf16(w)
    b = jnp.full((1, LANES), NEG_BIG, F32).at[0, :N_EXPERTS].set(router_b)
    return w_hi, w_lo, b


def kernel(x, hy_w_in, conv_w, ssm_a_re, ssm_a_im, ssm_log_dt, ssm_b_re, ssm_b_im, ssm_c_re,
           ssm_c_im, ssm_d, ssm_w_glu, ssm_b_glu, hy_w_out, att_w_qkv, att_w_o, ln_mix_g,
           ln_mix_b, ln_ffn_g, ln_ffn_b, router_w, router_b, expert_w_gu, expert_b_gu,
           expert_w_down, expert_b_down):
    batch, seq_len, _ = x.shape
    t = batch * seq_len
    n_chunks = seq_len // SSM_CHUNK
    h = x.reshape(t, D_MODEL)
    cos, sin = _rope_tables(seq_len)
    for layer in range(DEPTH):
        i = layer // 2
        consts = [ln_mix_g[layer][None, :], ln_mix_b[layer][None, :],
                  *_router_consts(router_w[layer], router_b[layer])]
        if layer % 2 == 0:
            y_conv, u = _inproj_conv(h, hy_w_in[i].astype(BF16), conv_w[i], seq_len)
            tables = _s5_tables(ssm_a_re[i], ssm_a_im[i], ssm_log_dt[i], ssm_b_re[i], ssm_b_im[i],
                                ssm_c_re[i], ssm_c_im[i])
            u_g = u.astype(BF16).reshape(batch, n_chunks, SSM_CHUNK, SSM_GROUPS, SSM_GROUP)
            u_g = u_g.transpose(3, 1, 0, 2, 4).reshape(SSM_GROUPS, n_chunks * batch,
                                                       SSM_CHUNK * SSM_GROUP)
            y_g = _s5_scan(u_g, tables, n_chunks, batch)
            y_s = y_g.reshape(SSM_GROUPS, n_chunks, batch, SSM_CHUNK, SSM_GROUP)
            y_s = y_s.transpose(2, 1, 3, 0, 4).reshape(t, SSM_WIDTH)
            w_out = hy_w_out[i].astype(BF16)
            hn, hn_bf, idx_tile, gate_tile = _even_post(
                h, y_conv, y_s, u,
                [ssm_d[i].reshape(1, SSM_WIDTH), ssm_w_glu[i].astype(BF16), ssm_b_glu[i][None, :],
                 w_out[:CONV_CH], w_out[CONV_CH:], *consts])
        else:
            hn, hn_bf, idx_tile, gate_tile = _attention_layer(
                h, att_w_qkv[i], att_w_o[i], cos, sin, consts, batch, seq_len)
        h = _moe_layer(hn, hn_bf, idx_tile, gate_tile, expert_w_gu[layer], expert_b_gu[layer],
                       expert_w_down[layer], expert_b_down[layer], ln_ffn_g[layer], ln_ffn_b[layer])
    return h.reshape(batch, seq_len, D_MODEL)
```

```python
import functools

import jax
import jax.numpy as jnp
from jax import lax
from jax.experimental import pallas as pl
from jax.experimental.pallas import tpu as pltpu

F32 = jnp.float32
BF16 = jnp.bfloat16

D_MODEL = 1024
DEPTH = 4
CONV_CH = 512
SSM_WIDTH = 512
SSM_GROUP = 16
SSM_GROUPS = 32
SSM_STATE = 64
N_HEADS = 16
HEAD_DIM = 64
ROPE_THETA = 10000.0
DILATIONS = (1, 4, 16)
ATT_BLOCK = 128
N_EXPERTS = 32
TOP_K = 4
SWIGLU_LIMIT = 7.0
SWIGLU_ALPHA = 1.702
DN_ALPHA = (2 * DEPTH) ** 0.25
LN_EPS = 1e-5

LANES = 128
ROW_TILE = 512
QKV_TILE = 256
SSM_CHUNK = 32
MOE_ROWS = 512
VMEM_LIMIT = 56 * 1024 * 1024
NEG_BIG = -1e30


def _params(n_axes=1):
    return pltpu.CompilerParams(dimension_semantics=("arbitrary",) * n_axes,
                                vmem_limit_bytes=VMEM_LIMIT)


def _full(shape):
    return pl.BlockSpec(shape, lambda *_: (0,) * len(shape))


def _rows(width, tile=ROW_TILE):
    return pl.BlockSpec((tile, width), lambda i: (i, 0))


def _layer_norm(x, g, b):
    mu = jnp.mean(x, axis=-1, keepdims=True)
    xc = x - mu
    var = jnp.mean(xc * xc, axis=-1, keepdims=True)
    return xc * lax.rsqrt(var + LN_EPS) * g + b


def _sigmoid(x):
    return 1.0 / (1.0 + jnp.exp(-x))


def _split_bf16(x):
    hi = x.astype(BF16)
    return hi, (x - hi.astype(F32)).astype(BF16)


def _route(hn, rw_hi_ref, rw_lo_ref, rb_ref, idx_ref, gate_ref):
    hi, lo = _split_bf16(hn)
    logits = (jnp.dot(hi, rw_hi_ref[...], preferred_element_type=F32)
              + jnp.dot(hi, rw_lo_ref[...], preferred_element_type=F32)
              + jnp.dot(lo, rw_hi_ref[...], preferred_element_type=F32)
              + rb_ref[...])
    lane = lax.broadcasted_iota(jnp.int32, logits.shape, 1)
    idx_tile = jnp.zeros(logits.shape, jnp.int32)
    val_tile = jnp.zeros(logits.shape, F32)
    top0 = None
    den = None
    for k in range(TOP_K):
        mx = jnp.max(logits, axis=-1, keepdims=True)
        first = jnp.min(jnp.where(logits == mx, lane, LANES), axis=-1, keepdims=True)
        if k == 0:
            top0 = mx
        e = jnp.exp(mx - top0)
        den = e if k == 0 else den + e
        idx_tile = jnp.where(lane == k, first, idx_tile)
        val_tile = jnp.where(lane == k, e, val_tile)
        logits = jnp.where(lane == first, -jnp.inf, logits)
    idx_ref[...] = idx_tile
    gate_ref[...] = val_tile / den


def _post_outputs(t):
    specs = [_rows(D_MODEL), _rows(D_MODEL), _rows(LANES), _rows(LANES)]
    shapes = [jax.ShapeDtypeStruct((t, D_MODEL), F32), jax.ShapeDtypeStruct((t, D_MODEL), BF16),
              jax.ShapeDtypeStruct((t, LANES), jnp.int32), jax.ShapeDtypeStruct((t, LANES), F32)]
    return specs, shapes


def _norm_and_route(pre, g_ref, b_ref, rwh_ref, rwl_ref, rb_ref, hn_ref, hnb_ref, idx_ref, gate_ref):
    hn = _layer_norm(pre, g_ref[...], b_ref[...])
    hn_ref[...] = hn
    hnb_ref[...] = hn.astype(BF16)
    _route(hn, rwh_ref, rwl_ref, rb_ref, idx_ref, gate_ref)


def _inproj_conv_kernel(x_ref, w_ref, cw_ref, yconv_ref, u_ref, carry_ref, *, tiles_per_seq):
    @pl.when(pl.program_id(0) % tiles_per_seq == 0)
    def _():
        carry_ref[...] = jnp.zeros_like(carry_ref)

    proj = jnp.dot(x_ref[...].astype(BF16), w_ref[...], preferred_element_type=F32)
    gate_b = proj[:, :CONV_CH]
    gate_c = proj[:, CONV_CH:2 * CONV_CH]
    hid = proj[:, 2 * CONV_CH:3 * CONV_CH]
    v = gate_c * hid
    rows = v.shape[0]
    row = lax.broadcasted_iota(jnp.int32, v.shape, 0)
    prev1 = carry_ref[7:8, :]
    prev2 = carry_ref[6:7, :]
    vm1 = jnp.where(row == 0, prev1, pltpu.roll(v, 1, 0))
    vm2 = jnp.where(row == 0, prev2, jnp.where(row == 1, prev1, pltpu.roll(v, 2, 0)))
    conv = cw_ref[0:1, :] * vm2 + cw_ref[1:2, :] * vm1 + cw_ref[2:3, :] * v
    yconv_ref[...] = gate_b * conv
    u_ref[...] = proj[:, 3 * CONV_CH:]
    carry_ref[...] = v[rows - 8:, :]


def _inproj_conv(h, w_in_bf, conv_w, seq_len):
    t = h.shape[0]
    return pl.pallas_call(
        functools.partial(_inproj_conv_kernel, tiles_per_seq=seq_len // ROW_TILE),
        grid=(t // ROW_TILE,),
        in_specs=[_rows(D_MODEL), _full(w_in_bf.shape), _full(conv_w.shape)],
        out_specs=[_rows(CONV_CH), _rows(SSM_WIDTH)],
        out_shape=[jax.ShapeDtypeStruct((t, CONV_CH), F32),
                   jax.ShapeDtypeStruct((t, SSM_WIDTH), F32)],
        scratch_shapes=[pltpu.VMEM((8, CONV_CH), F32)],
        compiler_params=_params(),
    )(h, w_in_bf, conv_w)


def _s5_tables(a_re, a_im, log_dt, b_re, b_im, c_re, c_im):
    q = SSM_CHUNK
    lam_re = jnp.minimum(a_re, -1e-4)
    lam_im = a_im
    dt = jnp.exp(log_dt)[:, None]
    mag = jnp.exp(lam_re * dt)
    ab_re = mag * jnp.cos(lam_im * dt)
    ab_im = mag * jnp.sin(lam_im * dt)
    nr, ni = ab_re - 1.0, ab_im
    den = lam_re * lam_re + lam_im * lam_im
    coef_re = ((nr * lam_re + ni * lam_im) / den)[..., None]
    coef_im = ((ni * lam_re - nr * lam_im) / den)[..., None]
    bb_re = coef_re * b_re - coef_im * b_im
    bb_im = coef_re * b_im + coef_im * b_re
    j = jnp.arange(q + 1, dtype=F32)
    pmag = jnp.exp((lam_re * dt)[..., None] * j)
    pang = (lam_im * dt)[..., None] * j
    pw_re = pmag * jnp.cos(pang)
    pw_im = pmag * jnp.sin(pang)
    ca_re = c_re[..., None] * pw_re[:, None] - c_im[..., None] * pw_im[:, None]
    ca_im = c_re[..., None] * pw_im[:, None] + c_im[..., None] * pw_re[:, None]
    kern = (jnp.einsum("gapj,gph->gjah", ca_re, bb_re, precision="highest")
            - jnp.einsum("gapj,gph->gjah", ca_im, bb_im, precision="highest"))[:, :q]
    lag = jnp.arange(q)[None, :] - jnp.arange(q)[:, None]
    toep = kern[:, jnp.clip(lag, 0, q - 1)]
    toep = jnp.where((lag >= 0)[None, :, :, None, None], toep, 0.0)
    toep = toep.transpose(0, 1, 4, 2, 3).reshape(SSM_GROUPS, q * SSM_GROUP, q * SSM_GROUP)
    rev = q - 1 - jnp.arange(q)
    pr = pw_re[:, :, rev]
    pi = pw_im[:, :, rev]
    inj_re = pr[..., None] * bb_re[:, :, None] - pi[..., None] * bb_im[:, :, None]
    inj_im = pr[..., None] * bb_im[:, :, None] + pi[..., None] * bb_re[:, :, None]
    inj_re = inj_re.transpose(0, 2, 3, 1).reshape(SSM_GROUPS, q * SSM_GROUP, SSM_STATE)
    inj_im = inj_im.transpose(0, 2, 3, 1).reshape(SSM_GROUPS, q * SSM_GROUP, SSM_STATE)
    out_re = ca_re[..., 1:].transpose(0, 2, 3, 1).reshape(SSM_GROUPS, SSM_STATE, q * SSM_GROUP)
    out_im = (-ca_im[..., 1:]).transpose(0, 2, 3, 1).reshape(SSM_GROUPS, SSM_STATE, q * SSM_GROUP)
    aq = jnp.stack([pw_re[..., q], pw_im[..., q]], axis=1)
    return (toep.astype(BF16), inj_re.astype(BF16), inj_im.astype(BF16),
            out_re.astype(BF16), out_im.astype(BF16), aq)


def _s5_kernel(u_ref, toep_ref, injr_ref, inji_ref, outr_ref, outi_ref, aq_ref, y_ref,
               ir_ref, ii_ref, xr_ref, xi_ref, *, n_chunks, batch):
    u = u_ref[0]
    ir_ref[...] = jnp.dot(u, injr_ref[0], preferred_element_type=F32)
    ii_ref[...] = jnp.dot(u, inji_ref[0], preferred_element_type=F32)
    aq_re = aq_ref[0, 0:1, :]
    aq_im = aq_ref[0, 1:2, :]
    xr = jnp.zeros((batch, SSM_STATE), F32)
    xi = jnp.zeros((batch, SSM_STATE), F32)
    for c in range(n_chunks):
        rows = slice(c * batch, (c + 1) * batch)
        xr_ref[rows, :] = xr
        xi_ref[rows, :] = xi
        xr, xi = (aq_re * xr - aq_im * xi + ir_ref[rows, :],
                  aq_re * xi + aq_im * xr + ii_ref[rows, :])
    y = jnp.dot(u, toep_ref[0], preferred_element_type=F32)
    y = y + jnp.dot(xr_ref[...].astype(BF16), outr_ref[0], preferred_element_type=F32)
    y = y + jnp.dot(xi_ref[...].astype(BF16), outi_ref[0], preferred_element_type=F32)
    y_ref[0] = y


def _s5_scan(u_g, tables, n_chunks, batch):
    toep, inj_re, inj_im, out_re, out_im, aq = tables
    rows = n_chunks * batch
    width = SSM_CHUNK * SSM_GROUP

    def grp(shape):
        return pl.BlockSpec((1,) + shape, lambda g: (g, 0, 0))

    return pl.pallas_call(
        functools.partial(_s5_kernel, n_chunks=n_chunks, batch=batch),
        grid=(SSM_GROUPS,),
        in_specs=[grp((rows, width)), grp((width, width)), grp((width, SSM_STATE)),
                  grp((width, SSM_STATE)), grp((SSM_STATE, width)), grp((SSM_STATE, width)),
                  grp((2, SSM_STATE))],
        out_specs=grp((rows, width)),
        out_shape=jax.ShapeDtypeStruct((SSM_GROUPS, rows, width), F32),
        scratch_shapes=[pltpu.VMEM((rows, SSM_STATE), F32)] * 4,
        compiler_params=_params(),
    )(u_g, toep, inj_re, inj_im, out_re, out_im, aq)


def _even_post_kernel(h_ref, yc_ref, ys_ref, u_ref, d_ref, wglu_ref, bglu_ref, woc_ref, wos_ref,
                      g_ref, b_ref, rwh_ref, rwl_ref, rb_ref, hn_ref, hnb_ref, idx_ref, gate_ref):
    y = ys_ref[...] + d_ref[...] * u_ref[...]
    z = jax.nn.gelu(y)
    glu = jnp.dot(z.astype(BF16), wglu_ref[...], preferred_element_type=F32) + bglu_ref[...]
    z = z * _sigmoid(glu)
    mix = (jnp.dot(yc_ref[...].astype(BF16), woc_ref[...], preferred_element_type=F32)
           + jnp.dot(z.astype(BF16), wos_ref[...], preferred_element_type=F32))
    _norm_and_route(DN_ALPHA * h_ref[...] + mix, g_ref, b_ref, rwh_ref, rwl_ref, rb_ref,
                    hn_ref, hnb_ref, idx_ref, gate_ref)


def _even_post(h, y_conv, y_s, u, consts):
    t = h.shape[0]
    out_specs, out_shape = _post_outputs(t)
    return pl.pallas_call(
        _even_post_kernel,
        grid=(t // ROW_TILE,),
        in_specs=[_rows(D_MODEL), _rows(CONV_CH), _rows(SSM_WIDTH), _rows(SSM_WIDTH)]
        + [_full(a.shape) for a in consts],
        out_specs=out_specs, out_shape=out_shape,
        compiler_params=_params(),
    )(h, y_conv, y_s, u, *consts)


def _qkv_rope_kernel(x_ref, w_ref, cos_ref, sin_ref, o1_ref, o4_ref, o16_ref, stage_ref):
    qkv = jnp.dot(x_ref[...].astype(BF16), w_ref[...], preferred_element_type=F32)
    rows = qkv.shape[0]
    cos = jnp.concatenate([cos_ref[...]] * (D_MODEL // LANES), axis=1)
    sin = jnp.concatenate([sin_ref[...]] * (D_MODEL // LANES), axis=1)
    lane = lax.broadcasted_iota(jnp.int32, (rows, D_MODEL), 1)
    low_half = (lane % HEAD_DIM) < (HEAD_DIM // 2)

    def rope(xs):
        up = pltpu.roll(xs, D_MODEL - HEAD_DIM // 2, 1)
        down = pltpu.roll(xs, HEAD_DIM // 2, 1)
        return xs * cos + jnp.where(low_half, up, down) * sin

    parts = (rope(qkv[:, :D_MODEL]) * (HEAD_DIM ** -0.5), rope(qkv[:, D_MODEL:2 * D_MODEL]),
             qkv[:, 2 * D_MODEL:])
    for which, part in enumerate(parts):
        o1_ref[:, which * D_MODEL:(which + 1) * D_MODEL] = part.astype(BF16)
        for c in range(D_MODEL // LANES):
            stage_ref[which * (D_MODEL // LANES) + c] = part[:, c * LANES:(c + 1) * LANES]
    for dil, ref in ((DILATIONS[1], o4_ref), (DILATIONS[2], o16_ref)):
        for c in range(3 * D_MODEL // LANES):
            for r in range(dil):
                ref[0, r, :, c * LANES:(c + 1) * LANES] = (
                    stage_ref[c, pl.ds(r, rows // dil, stride=dil), :].astype(BF16))


def _rope_tables(seq_len):
    half = HEAD_DIM // 2
    inv = ROPE_THETA ** (-jnp.arange(half, dtype=F32) / half)
    ang = jnp.arange(seq_len, dtype=F32)[:, None] * inv[None, :]
    cos = jnp.tile(jnp.cos(ang), (1, LANES // half))
    sin = jnp.sin(ang)
    sin = jnp.tile(jnp.concatenate([-sin, sin], axis=1), (1, LANES // HEAD_DIM))
    return cos, sin


def _residue_spec(dil, rows, width, tiles_per_seq):
    return pl.BlockSpec((1, dil, rows // dil, width),
                        lambda i: (i // tiles_per_seq, 0, i % tiles_per_seq, 0))


def _qkv_rope(h, w_qkv_bf, cos, sin, batch, seq_len):
    t = h.shape[0]
    tps = seq_len // QKV_TILE
    d4, d16 = DILATIONS[1], DILATIONS[2]
    width = 3 * D_MODEL
    return pl.pallas_call(
        _qkv_rope_kernel,
        grid=(t // QKV_TILE,),
        in_specs=[_rows(D_MODEL, QKV_TILE), _full(w_qkv_bf.shape),
                  pl.BlockSpec((QKV_TILE, LANES), lambda i: (i % tps, 0)),
                  pl.BlockSpec((QKV_TILE, LANES), lambda i: (i % tps, 0))],
        out_specs=[_rows(width, QKV_TILE), _residue_spec(d4, QKV_TILE, width, tps),
                   _residue_spec(d16, QKV_TILE, width, tps)],
        out_shape=[jax.ShapeDtypeStruct((t, width), BF16),
                   jax.ShapeDtypeStruct((batch, d4, seq_len // d4, width), BF16),
                   jax.ShapeDtypeStruct((batch, d16, seq_len // d16, width), BF16)],
        scratch_shapes=[pltpu.VMEM((width // LANES, QKV_TILE, LANES), F32)],
        compiler_params=_params(),
    )(h, w_qkv_bf, cos, sin)


def _attn_kernel(q_ref, kp_ref, kc_ref, vp_ref, vc_ref, o_ref, lse_ref):
    blk = ATT_BLOCK
    qi = lax.broadcasted_iota(jnp.int32, (blk, 2 * blk), 0)
    kj = lax.broadcasted_iota(jnp.int32, (blk, 2 * blk), 1)
    dist = blk + qi - kj
    k_min = jnp.where(pl.program_id(1) == 0, blk, 0)
    bias = jnp.where(dist >= 0, jnp.where(dist <= blk, jnp.where(kj >= k_min, 0.0, NEG_BIG),
                                          NEG_BIG), NEG_BIG)
    lane = lax.broadcasted_iota(jnp.int32, (blk, LANES), 1)
    low = lane < HEAD_DIM
    lse_tile = jnp.zeros((blk, LANES), F32)
    for hp in range(N_HEADS // 2):
        cols = slice(hp * LANES, (hp + 1) * LANES)
        q2 = q_ref[0, :, cols]
        kk = jnp.concatenate([kp_ref[0, :, cols], kc_ref[0, :, cols]], axis=0)
        vv = jnp.concatenate([vp_ref[0, :, cols], vc_ref[0, :, cols]], axis=0)
        halves = []
        for hh in range(2):
            head = 2 * hp + hh
            qm = jnp.where(low if hh == 0 else jnp.logical_not(low), q2, jnp.zeros_like(q2))
            s = lax.dot_general(qm, kk, (((1,), (1,)), ((), ())),
                                preferred_element_type=F32) + bias
            m = jnp.max(s, axis=-1, keepdims=True)
            p = jnp.exp(s - m)
            den = jnp.sum(p, axis=-1, keepdims=True)
            pv = jnp.dot(p.astype(BF16), vv, preferred_element_type=F32)
            halves.append(pv / den)
            lse_tile = jnp.where(lane == head, m + jnp.log(den), lse_tile)
        o_ref[0, :, cols] = jnp.where(low, halves[0], halves[1]).astype(BF16)
    lse_ref[0] = lse_tile


def _attn_pattern(qkv_sub):
    n_sub, sub_len, _ = qkv_sub.shape

    def cur(which):
        return pl.BlockSpec((1, ATT_BLOCK, D_MODEL), lambda s, n: (s, n, which))

    def prev(which):
        return pl.BlockSpec((1, ATT_BLOCK, D_MODEL), lambda s, n: (s, jnp.maximum(n - 1, 0), which))

    return pl.pallas_call(
        _attn_kernel,
        grid=(n_sub, sub_len // ATT_BLOCK),
        in_specs=[cur(0), prev(1), cur(1), prev(2), cur(2)],
        out_specs=[pl.BlockSpec((1, ATT_BLOCK, D_MODEL), lambda s, n: (s, n, 0)),
                   pl.BlockSpec((1, ATT_BLOCK, LANES), lambda s, n: (s, n, 0))],
        out_shape=[jax.ShapeDtypeStruct((n_sub, sub_len, D_MODEL), BF16),
                   jax.ShapeDtypeStruct((n_sub, sub_len, LANES), F32)],
        compiler_params=_params(2),
    )(*([qkv_sub] * 5))


def _odd_post_kernel(h_ref, o1_ref, l1_ref, o4_ref, l4_ref, o16_ref, l16_ref, expand_ref, wo_ref,
                     g_ref, b_ref, rwh_ref, rwl_ref, rb_ref, hn_ref, hnb_ref, idx_ref, gate_ref,
                     os4_ref, ls4_ref, os16_ref, ls16_ref):
    rows = h_ref.shape[0]
    n_col = D_MODEL // LANES
    for dil, o_ref, l_ref, os_ref, ls_ref in ((DILATIONS[1], o4_ref, l4_ref, os4_ref, ls4_ref),
                                              (DILATIONS[2], o16_ref, l16_ref, os16_ref, ls16_ref)):
        for r in range(dil):
            sel = pl.ds(r, rows // dil, stride=dil)
            ls_ref[sel, :] = l_ref[0, r]
            for c in range(n_col):
                os_ref[c, sel, :] = o_ref[0, r, :, c * LANES:(c + 1) * LANES].astype(F32)
    lses = (l1_ref[...], ls4_ref[...], ls16_ref[...])
    outs = (o1_ref[...].astype(F32),
            jnp.concatenate([os4_ref[c] for c in range(n_col)], axis=1),
            jnp.concatenate([os16_ref[c] for c in range(n_col)], axis=1))
    mx = jnp.maximum(jnp.maximum(lses[0], lses[1]), lses[2])
    es = [jnp.exp(l - mx) for l in lses]
    den = es[0] + es[1] + es[2]
    o = jnp.zeros((rows, D_MODEL), F32)
    for e, out in zip(es, outs):
        hi, lo = _split_bf16(e / den)
        w = (jnp.dot(hi, expand_ref[...], preferred_element_type=F32)
             + jnp.dot(lo, expand_ref[...], preferred_element_type=F32))
        o = o + w * out
    mix = jnp.dot(o.astype(BF16), wo_ref[...], preferred_element_type=F32)
    _norm_and_route(DN_ALPHA * h_ref[...] + mix, g_ref, b_ref, rwh_ref, rwl_ref, rb_ref,
                    hn_ref, hnb_ref, idx_ref, gate_ref)


def _odd_post(h, pattern_outs, consts, seq_len):
    t = h.shape[0]
    tps = seq_len // ROW_TILE
    (o1, l1), (o4, l4), (o16, l16) = pattern_outs
    d4, d16 = DILATIONS[1], DILATIONS[2]
    out_specs, out_shape = _post_outputs(t)
    return pl.pallas_call(
        _odd_post_kernel,
        grid=(t // ROW_TILE,),
        in_specs=[_rows(D_MODEL), _rows(D_MODEL), _rows(LANES),
                  _residue_spec(d4, ROW_TILE, D_MODEL, tps), _residue_spec(d4, ROW_TILE, LANES, tps),
                  _residue_spec(d16, ROW_TILE, D_MODEL, tps), _residue_spec(d16, ROW_TILE, LANES, tps)]
        + [_full(a.shape) for a in consts],
        out_specs=out_specs, out_shape=out_shape,
        scratch_shapes=[pltpu.VMEM((D_MODEL // LANES, ROW_TILE, LANES), F32),
                        pltpu.VMEM((ROW_TILE, LANES), F32)] * 2,
        compiler_params=_params(),
    )(h, o1, l1, o4, l4, o16, l16, *consts)


def _attention_layer(h, w_qkv, w_o, cos, sin, consts, batch, seq_len):
    t = h.shape[0]
    qkv1, qkv4, qkv16 = _qkv_rope(h, w_qkv.astype(BF16), cos, sin, batch, seq_len)
    outs = []
    for dil, qkv in zip(DILATIONS, (qkv1, qkv4, qkv16)):
        sub_len = seq_len // dil
        o, lse = _attn_pattern(qkv.reshape(batch * dil, sub_len, 3 * D_MODEL))
        if dil == 1:
            outs.append((o.reshape(t, D_MODEL), lse.reshape(t, LANES)))
        else:
            outs.append((o.reshape(batch, dil, sub_len, D_MODEL),
                         lse.reshape(batch, dil, sub_len, LANES)))
    head_of_lane = jnp.arange(D_MODEL, dtype=jnp.int32) // HEAD_DIM
    expand = (jnp.arange(LANES, dtype=jnp.int32)[:, None] == head_of_lane[None, :]).astype(BF16)
    return _odd_post(h, outs, [expand, w_o.astype(BF16), *consts], seq_len)


def _moe_kernel(be_ref, nused_ref, x_ref, wgu_ref, bgu_ref, wd_ref, bd_ref, y_ref,
                wgu_bf_ref, wd_bf_ref):
    i = pl.program_id(0)
    new_expert = jnp.logical_or(i == 0, be_ref[i] != be_ref[jnp.maximum(i - 1, 0)])

    @pl.when(new_expert)
    def _():
        wgu_bf_ref[...] = wgu_ref[0, 0].astype(BF16)
        wd_bf_ref[...] = wd_ref[0, 0].astype(BF16)

    @pl.when(i < nused_ref[0])
    def _():
        hid = jnp.dot(x_ref[...], wgu_bf_ref[...], preferred_element_type=F32) + bgu_ref[0]
        gate = jnp.minimum(hid[:, :D_MODEL], SWIGLU_LIMIT)
        lin = jnp.clip(hid[:, D_MODEL:], -SWIGLU_LIMIT, SWIGLU_LIMIT)
        act = (lin + 1.0) * (gate * _sigmoid(SWIGLU_ALPHA * gate))
        y = jnp.dot(act.astype(BF16), wd_bf_ref[...], preferred_element_type=F32) + bd_ref[0]
        y_ref[...] = y.astype(BF16)

    @pl.when(i >= nused_ref[0])
    def _():
        y_ref[...] = jnp.zeros_like(y_ref)


def _moe_experts(xs, block_e, n_used, layer, w_gu, b_gu, w_down, b_down):
    n_rows = xs.shape[0]
    grid_spec = pltpu.PrefetchScalarGridSpec(
        num_scalar_prefetch=2,
        grid=(n_rows // MOE_ROWS,),
        in_specs=[pl.BlockSpec((MOE_ROWS, D_MODEL), lambda i, be, nu: (i, 0)),
                  pl.BlockSpec((1, 1, D_MODEL, 2 * D_MODEL), lambda i, be, nu: (layer, be[i], 0, 0)),
                  pl.BlockSpec((1, 1, 2 * D_MODEL), lambda i, be, nu: (be[i], 0, 0)),
                  pl.BlockSpec((1, 1, D_MODEL, D_MODEL), lambda i, be, nu: (layer, be[i], 0, 0)),
                  pl.BlockSpec((1, 1, D_MODEL), lambda i, be, nu: (be[i], 0, 0))],
        out_specs=pl.BlockSpec((MOE_ROWS, D_MODEL), lambda i, be, nu: (i, 0)),
        scratch_shapes=[pltpu.VMEM((D_MODEL, 2 * D_MODEL), BF16), pltpu.VMEM((D_MODEL, D_MODEL), BF16)],
    )
    return pl.pallas_call(
        _moe_kernel,
        grid_spec=grid_spec,
        out_shape=jax.ShapeDtypeStruct((n_rows, D_MODEL), BF16),
        compiler_params=_params(),
    )(block_e, n_used, xs, w_gu, b_gu[:, None, :], w_down, b_down[:, None, :])


def _routing_tables(top_idx, t):
    n_assign = t * TOP_K
    flat_e = top_idx.T.reshape(-1)
    onehot = (flat_e[:, None] == jnp.arange(N_EXPERTS, dtype=jnp.int32)[None, :]).astype(jnp.int32)
    csum = jnp.cumsum(onehot, axis=0)
    counts = csum[-1]
    padded = (counts + MOE_ROWS - 1) // MOE_ROWS * MOE_ROWS
    pad_end = jnp.cumsum(padded)
    pad_start = pad_end - padded
    dest = jnp.sum(onehot * (csum - 1 + pad_start[None, :]), axis=1)
    n_blocks = n_assign // MOE_ROWS + N_EXPERTS
    flat_tok = jnp.arange(n_assign, dtype=jnp.int32) % t
    row_tok = (jnp.arange(n_blocks * MOE_ROWS, dtype=jnp.int32) % t).at[dest].set(
        flat_tok, unique_indices=True, mode="promise_in_bounds")
    block_start = jnp.arange(n_blocks, dtype=jnp.int32) * MOE_ROWS
    block_e = jnp.minimum(jnp.sum((block_start[:, None] >= pad_end[None, :]).astype(jnp.int32), axis=1),
                          N_EXPERTS - 1)
    n_used = (pad_end[-1:] // MOE_ROWS).astype(jnp.int32)
    return row_tok, dest, block_e, n_used


def _ffn_ln_kernel(h_ref, y4_ref, gate_ref, g_ref, b_ref, o_ref):
    gates = gate_ref[...]
    acc = DN_ALPHA * h_ref[...]
    for k in range(TOP_K):
        acc = acc + gates[:, k:k + 1] * y4_ref[k].astype(F32)
    o_ref[...] = _layer_norm(acc, g_ref[...], b_ref[...])


def _ffn_ln(h, y4, gates, g, b):
    t = h.shape[0]
    return pl.pallas_call(
        _ffn_ln_kernel,
        grid=(t // ROW_TILE,),
        in_specs=[_rows(D_MODEL), pl.BlockSpec((TOP_K, ROW_TILE, D_MODEL), lambda i: (0, i, 0)),
                  _rows(LANES), _full(g.shape), _full(b.shape)],
        out_specs=_rows(D_MODEL),
        out_shape=jax.ShapeDtypeStruct((t, D_MODEL), F32),
        compiler_params=_params(),
    )(h, y4, gates, g, b)


def _moe_layer(hn, hn_bf, idx_tile, gate_tile, layer, w_gu, b_gu, w_down, b_down, ln_g, ln_b):
    t = hn.shape[0]
    row_tok, dest, block_e, n_used = _routing_tables(idx_tile[:, :TOP_K], t)
    xs = hn_bf.at[row_tok].get(mode="promise_in_bounds")
    ys = _moe_experts(xs, block_e, n_used, layer, w_gu, b_gu, w_down, b_down)
    y4 = ys.at[dest].get(mode="promise_in_bounds").reshape(TOP_K, t, D_MODEL)
    return _ffn_ln(hn, y4, gate_tile, ln_g[None, :], ln_b[None, :])


def _router_consts(router_w, router_b):
    w = jnp.zeros((D_MODEL, LANES), F32).at[:, :N_EXPERTS].set(router_w)
    w_hi, w_lo = _split_bf16(w)
    b = jnp.full((1, LANES), NEG_BIG, F32).at[0, :N_EXPERTS].set(router_b)
    return w_hi, w_lo, b


def kernel(x, hy_w_in, conv_w, ssm_a_re, ssm_a_im, ssm_log_dt, ssm_b_re, ssm_b_im, ssm_c_re,
           ssm_c_im, ssm_d, ssm_w_glu, ssm_b_glu, hy_w_out, att_w_qkv, att_w_o, ln_mix_g,
           ln_mix_b, ln_ffn_g, ln_ffn_b, router_w, router_b, expert_w_gu, expert_b_gu,
           expert_w_down, expert_b_down):
    batch, seq_len, _ = x.shape
    t = batch * seq_len
    n_chunks = seq_len // SSM_CHUNK
    h = x.reshape(t, D_MODEL)
    cos, sin = _rope_tables(seq_len)
    for layer in range(DEPTH):
        i = layer // 2
        consts = [ln_mix_g[layer][None, :], ln_mix_b[layer][None, :],
                  *_router_consts(router_w[layer], router_b[layer])]
        if layer % 2 == 0:
            y_conv, u = _inproj_conv(h, hy_w_in[i].astype(BF16), conv_w[i], seq_len)
            tables = _s5_tables(ssm_a_re[i], ssm_a_im[i], ssm_log_dt[i], ssm_b_re[i], ssm_b_im[i],
                                ssm_c_re[i], ssm_c_im[i])
            u_g = u.astype(BF16).reshape(batch, n_chunks, SSM_CHUNK, SSM_GROUPS, SSM_GROUP)
            u_g = u_g.transpose(3, 1, 0, 2, 4).reshape(SSM_GROUPS, n_chunks * batch,
                                                       SSM_CHUNK * SSM_GROUP)
            y_g = _s5_scan(u_g, tables, n_chunks, batch)
            y_s = y_g.reshape(SSM_GROUPS, n_chunks, batch, SSM_CHUNK, SSM_GROUP)
            y_s = y_s.transpose(2, 1, 3, 0, 4).reshape(t, SSM_WIDTH)
            w_out = hy_w_out[i].astype(BF16)
            hn, hn_bf, idx_tile, gate_tile = _even_post(
                h, y_conv, y_s, u,
                [ssm_d[i].reshape(1, SSM_WIDTH), ssm_w_glu[i].astype(BF16), ssm_b_glu[i][None, :],
                 w_out[:CONV_CH], w_out[CONV_CH:], *consts])
        else:
            hn, hn_bf, idx_tile, gate_tile = _attention_layer(
                h, att_w_qkv[i], att_w_o[i], cos, sin, consts, batch, seq_len)
        h = _moe_layer(hn, hn_bf, idx_tile, gate_tile, layer, expert_w_gu, expert_b_gu[layer],
                       expert_w_down, expert_b_down[layer], ln_ffn_g[layer], ln_ffn_b[layer])
    return h.reshape(batch, seq_len, D_MODEL)
```

```python
import functools

import jax
import jax.numpy as jnp
from jax import lax
from jax.experimental import pallas as pl
from jax.experimental.pallas import tpu as pltpu

F32 = jnp.float32
BF16 = jnp.bfloat16

D_MODEL = 1024
DEPTH = 4
CONV_CH = 512
SSM_WIDTH = 512
SSM_GROUP = 16
SSM_GROUPS = 32
SSM_STATE = 64
N_HEADS = 16
HEAD_DIM = 64
ROPE_THETA = 10000.0
DILATIONS = (1, 4, 16)
ATT_BLOCK = 128
N_EXPERTS = 32
TOP_K = 4
SWIGLU_LIMIT = 7.0
SWIGLU_ALPHA = 1.702
DN_ALPHA = (2 * DEPTH) ** 0.25
LN_EPS = 1e-5

LANES = 128
ROW_TILE = 512
QKV_TILE = 256
SSM_CHUNK = 32
MOE_ROWS = 512
VMEM_LIMIT = 56 * 1024 * 1024
NEG_BIG = -1e30


def _params(n_axes=1):
    return pltpu.CompilerParams(dimension_semantics=("arbitrary",) * n_axes,
                                vmem_limit_bytes=VMEM_LIMIT)


def _full(shape):
    return pl.BlockSpec(shape, lambda *_: (0,) * len(shape))


def _rows(width, tile=ROW_TILE):
    return pl.BlockSpec((tile, width), lambda i: (i, 0))


def _layer_norm(x, g, b):
    mu = jnp.mean(x, axis=-1, keepdims=True)
    xc = x - mu
    var = jnp.mean(xc * xc, axis=-1, keepdims=True)
    return xc * lax.rsqrt(var + LN_EPS) * g + b


def _sigmoid(x):
    return 1.0 / (1.0 + jnp.exp(-x))


def _split_bf16(x):
    hi = x.astype(BF16)
    return hi, (x - hi.astype(F32)).astype(BF16)


def _route(hn, rw_hi_ref, rw_lo_ref, rb_ref, idx_ref, gate_ref):
    hi, lo = _split_bf16(hn)
    logits = (jnp.dot(hi, rw_hi_ref[...], preferred_element_type=F32)
              + jnp.dot(hi, rw_lo_ref[...], preferred_element_type=F32)
              + jnp.dot(lo, rw_hi_ref[...], preferred_element_type=F32)
              + rb_ref[...])
    lane = lax.broadcasted_iota(jnp.int32, logits.shape, 1)
    idx_tile = jnp.zeros(logits.shape, jnp.int32)
    val_tile = jnp.zeros(logits.shape, F32)
    top0 = None
    den = None
    for k in range(TOP_K):
        mx = jnp.max(logits, axis=-1, keepdims=True)
        first = jnp.min(jnp.where(logits == mx, lane, LANES), axis=-1, keepdims=True)
        if k == 0:
            top0 = mx
        e = jnp.exp(mx - top0)
        den = e if k == 0 else den + e
        idx_tile = jnp.where(lane == k, first, idx_tile)
        val_tile = jnp.where(lane == k, e, val_tile)
        logits = jnp.where(lane == first, -jnp.inf, logits)
    idx_ref[...] = idx_tile
    gate_ref[...] = val_tile / den


def _post_outputs(t):
    specs = [_rows(D_MODEL), _rows(D_MODEL), _rows(LANES), _rows(LANES)]
    shapes = [jax.ShapeDtypeStruct((t, D_MODEL), F32), jax.ShapeDtypeStruct((t, D_MODEL), BF16),
              jax.ShapeDtypeStruct((t, LANES), jnp.int32), jax.ShapeDtypeStruct((t, LANES), F32)]
    return specs, shapes


def _norm_and_route(pre, g_ref, b_ref, rwh_ref, rwl_ref, rb_ref, hn_ref, hnb_ref, idx_ref, gate_ref):
    hn = _layer_norm(pre, g_ref[...], b_ref[...])
    hn_ref[...] = hn
    hnb_ref[...] = hn.astype(BF16)
    _route(hn, rwh_ref, rwl_ref, rb_ref, idx_ref, gate_ref)


def _inproj_conv_kernel(x_ref, w_ref, cw_ref, yconv_ref, u_ref, carry_ref, *, tiles_per_seq):
    @pl.when(pl.program_id(0) % tiles_per_seq == 0)
    def _():
        carry_ref[...] = jnp.zeros_like(carry_ref)

    proj = jnp.dot(x_ref[...].astype(BF16), w_ref[...], preferred_element_type=F32)
    gate_b = proj[:, :CONV_CH]
    gate_c = proj[:, CONV_CH:2 * CONV_CH]
    hid = proj[:, 2 * CONV_CH:3 * CONV_CH]
    v = gate_c * hid
    rows = v.shape[0]
    row = lax.broadcasted_iota(jnp.int32, v.shape, 0)
    prev1 = carry_ref[7:8, :]
    prev2 = carry_ref[6:7, :]
    vm1 = jnp.where(row == 0, prev1, pltpu.roll(v, 1, 0))
    vm2 = jnp.where(row == 0, prev2, jnp.where(row == 1, prev1, pltpu.roll(v, 2, 0)))
    conv = cw_ref[0:1, :] * vm2 + cw_ref[1:2, :] * vm1 + cw_ref[2:3, :] * v
    yconv_ref[...] = gate_b * conv
    u_ref[...] = proj[:, 3 * CONV_CH:]
    carry_ref[...] = v[rows - 8:, :]


def _inproj_conv(h, w_in_bf, conv_w, seq_len):
    t = h.shape[0]
    return pl.pallas_call(
        functools.partial(_inproj_conv_kernel, tiles_per_seq=seq_len // ROW_TILE),
        grid=(t // ROW_TILE,),
        in_specs=[_rows(D_MODEL), _full(w_in_bf.shape), _full(conv_w.shape)],
        out_specs=[_rows(CONV_CH), _rows(SSM_WIDTH)],
        out_shape=[jax.ShapeDtypeStruct((t, CONV_CH), F32),
                   jax.ShapeDtypeStruct((t, SSM_WIDTH), F32)],
        scratch_shapes=[pltpu.VMEM((8, CONV_CH), F32)],
        compiler_params=_params(),
    )(h, w_in_bf, conv_w)


def _s5_tables(a_re, a_im, log_dt, b_re, b_im, c_re, c_im):
    q = SSM_CHUNK
    lam_re = jnp.minimum(a_re, -1e-4)
    lam_im = a_im
    dt = jnp.exp(log_dt)[:, None]
    mag = jnp.exp(lam_re * dt)
    ab_re = mag * jnp.cos(lam_im * dt)
    ab_im = mag * jnp.sin(lam_im * dt)
    nr, ni = ab_re - 1.0, ab_im
    den = lam_re * lam_re + lam_im * lam_im
    coef_re = ((nr * lam_re + ni * lam_im) / den)[..., None]
    coef_im = ((ni * lam_re - nr * lam_im) / den)[..., None]
    bb_re = coef_re * b_re - coef_im * b_im
    bb_im = coef_re * b_im + coef_im * b_re
    j = jnp.arange(q + 1, dtype=F32)
    pmag = jnp.exp((lam_re * dt)[..., None] * j)
    pang = (lam_im * dt)[..., None] * j
    pw_re = pmag * jnp.cos(pang)
    pw_im = pmag * jnp.sin(pang)
    ca_re = c_re[..., None] * pw_re[:, None] - c_im[..., None] * pw_im[:, None]
    ca_im = c_re[..., None] * pw_im[:, None] + c_im[..., None] * pw_re[:, None]
    kern = (jnp.einsum("gapj,gph->gjah", ca_re, bb_re, precision="highest")
            - jnp.einsum("gapj,gph->gjah", ca_im, bb_im, precision="highest"))[:, :q]
    lag = jnp.arange(q)[None, :] - jnp.arange(q)[:, None]
    toep = kern[:, jnp.clip(lag, 0, q - 1)]
    toep = jnp.where((lag >= 0)[None, :, :, None, None], toep, 0.0)
    toep = toep.transpose(0, 1, 4, 2, 3).reshape(SSM_GROUPS, q * SSM_GROUP, q * SSM_GROUP)
    rev = q - 1 - jnp.arange(q)
    pr = pw_re[:, :, rev]
    pi = pw_im[:, :, rev]
    inj_re = pr[..., None] * bb_re[:, :, None] - pi[..., None] * bb_im[:, :, None]
    inj_im = pr[..., None] * bb_im[:, :, None] + pi[..., None] * bb_re[:, :, None]
    inj_re = inj_re.transpose(0, 2, 3, 1).reshape(SSM_GROUPS, q * SSM_GROUP, SSM_STATE)
    inj_im = inj_im.transpose(0, 2, 3, 1).reshape(SSM_GROUPS, q * SSM_GROUP, SSM_STATE)
    out_re = ca_re[..., 1:].transpose(0, 2, 3, 1).reshape(SSM_GROUPS, SSM_STATE, q * SSM_GROUP)
    out_im = (-ca_im[..., 1:]).transpose(0, 2, 3, 1).reshape(SSM_GROUPS, SSM_STATE, q * SSM_GROUP)
    aq = jnp.stack([pw_re[..., q], pw_im[..., q]], axis=1)
    return (toep.astype(BF16), inj_re.astype(BF16), inj_im.astype(BF16),
            out_re.astype(BF16), out_im.astype(BF16), aq)


def _s5_kernel(u_ref, toep_ref, injr_ref, inji_ref, outr_ref, outi_ref, aq_ref, y_ref,
               ir_ref, ii_ref, xr_ref, xi_ref, *, n_chunks, batch):
    u = u_ref[0]
    ir_ref[...] = jnp.dot(u, injr_ref[0], preferred_element_type=F32)
    ii_ref[...] = jnp.dot(u, inji_ref[0], preferred_element_type=F32)
    aq_re = aq_ref[0, 0:1, :]
    aq_im = aq_ref[0, 1:2, :]
    xr = jnp.zeros((batch, SSM_STATE), F32)
    xi = jnp.zeros((batch, SSM_STATE), F32)
    for c in range(n_chunks):
        rows = slice(c * batch, (c + 1) * batch)
        xr_ref[rows, :] = xr
        xi_ref[rows, :] = xi
        xr, xi = (aq_re * xr - aq_im * xi + ir_ref[rows, :],
                  aq_re * xi + aq_im * xr + ii_ref[rows, :])
    y = jnp.dot(u, toep_ref[0], preferred_element_type=F32)
    y = y + jnp.dot(xr_ref[...].astype(BF16), outr_ref[0], preferred_element_type=F32)
    y = y + jnp.dot(xi_ref[...].astype(BF16), outi_ref[0], preferred_element_type=F32)
    y_ref[0] = y


def _s5_scan(u_g, tables, n_chunks, batch):
    toep, inj_re, inj_im, out_re, out_im, aq = tables
    rows = n_chunks * batch
    width = SSM_CHUNK * SSM_GROUP

    def grp(shape):
        return pl.BlockSpec((1,) + shape, lambda g: (g, 0, 0))

    return pl.pallas_call(
        functools.partial(_s5_kernel, n_chunks=n_chunks, batch=batch),
        grid=(SSM_GROUPS,),
        in_specs=[grp((rows, width)), grp((width, width)), grp((width, SSM_STATE)),
                  grp((width, SSM_STATE)), grp((SSM_STATE, width)), grp((SSM_STATE, width)),
                  grp((2, SSM_STATE))],
        out_specs=grp((rows, width)),
        out_shape=jax.ShapeDtypeStruct((SSM_GROUPS, rows, width), F32),
        scratch_shapes=[pltpu.VMEM((rows, SSM_STATE), F32)] * 4,
        compiler_params=_params(),
    )(u_g, toep, inj_re, inj_im, out_re, out_im, aq)


def _even_post_kernel(h_ref, yc_ref, ys_ref, u_ref, d_ref, wglu_ref, bglu_ref, woc_ref, wos_ref,
                      g_ref, b_ref, rwh_ref, rwl_ref, rb_ref, hn_ref, hnb_ref, idx_ref, gate_ref):
    y = ys_ref[...] + d_ref[...] * u_ref[...]
    z = jax.nn.gelu(y)
    glu = jnp.dot(z.astype(BF16), wglu_ref[...], preferred_element_type=F32) + bglu_ref[...]
    z = z * _sigmoid(glu)
    mix = (jnp.dot(yc_ref[...].astype(BF16), woc_ref[...], preferred_element_type=F32)
           + jnp.dot(z.astype(BF16), wos_ref[...], preferred_element_type=F32))
    _norm_and_route(DN_ALPHA * h_ref[...] + mix, g_ref, b_ref, rwh_ref, rwl_ref, rb_ref,
                    hn_ref, hnb_ref, idx_ref, gate_ref)


def _even_post(h, y_conv, y_s, u, consts):
    t = h.shape[0]
    out_specs, out_shape = _post_outputs(t)
    return pl.pallas_call(
        _even_post_kernel,
        grid=(t // ROW_TILE,),
        in_specs=[_rows(D_MODEL), _rows(CONV_CH), _rows(SSM_WIDTH), _rows(SSM_WIDTH)]
        + [_full(a.shape) for a in consts],
        out_specs=out_specs, out_shape=out_shape,
        compiler_params=_params(),
    )(h, y_conv, y_s, u, *consts)


def _qkv_rope_kernel(x_ref, w_ref, cos_ref, sin_ref, p4_ref, p16_ref, o1_ref, o4_ref, o16_ref):
    qkv = jnp.dot(x_ref[...].astype(BF16), w_ref[...], preferred_element_type=F32)
    rows = qkv.shape[0]
    cos = jnp.concatenate([cos_ref[...]] * (D_MODEL // LANES), axis=1)
    sin = jnp.concatenate([sin_ref[...]] * (D_MODEL // LANES), axis=1)
    lane = lax.broadcasted_iota(jnp.int32, (rows, D_MODEL), 1)
    low_half = (lane % HEAD_DIM) < (HEAD_DIM // 2)

    def rope(xs):
        up = pltpu.roll(xs, D_MODEL - HEAD_DIM // 2, 1)
        down = pltpu.roll(xs, HEAD_DIM // 2, 1)
        return xs * cos + jnp.where(low_half, up, down) * sin

    parts = (rope(qkv[:, :D_MODEL]) * (HEAD_DIM ** -0.5), rope(qkv[:, D_MODEL:2 * D_MODEL]),
             qkv[:, 2 * D_MODEL:])
    for which, part in enumerate(parts):
        cols = slice(which * D_MODEL, (which + 1) * D_MODEL)
        part = part.astype(BF16)
        o1_ref[:, cols] = part
        for dil, perm_ref, ref in ((DILATIONS[1], p4_ref, o4_ref), (DILATIONS[2], p16_ref, o16_ref)):
            split = jnp.dot(perm_ref[...], part, preferred_element_type=F32).astype(BF16)
            n = rows // dil
            for r in range(dil):
                ref[0, r, :, cols] = split[r * n:(r + 1) * n, :]


def _rope_tables(seq_len):
    half = HEAD_DIM // 2
    inv = ROPE_THETA ** (-jnp.arange(half, dtype=F32) / half)
    ang = jnp.arange(seq_len, dtype=F32)[:, None] * inv[None, :]
    cos = jnp.tile(jnp.cos(ang), (1, LANES // half))
    sin = jnp.sin(ang)
    sin = jnp.tile(jnp.concatenate([-sin, sin], axis=1), (1, LANES // HEAD_DIM))
    return cos, sin


def _residue_spec(dil, rows, width, tiles_per_seq):
    return pl.BlockSpec((1, dil, rows // dil, width),
                        lambda i: (i // tiles_per_seq, 0, i % tiles_per_seq, 0))


def _qkv_rope(h, w_qkv_bf, cos, sin, batch, seq_len):
    t = h.shape[0]
    tps = seq_len // QKV_TILE
    d4, d16 = DILATIONS[1], DILATIONS[2]
    width = 3 * D_MODEL
    perms = []
    for dil in (d4, d16):
        j = jnp.arange(QKV_TILE, dtype=jnp.int32)
        src = (j % (QKV_TILE // dil)) * dil + j // (QKV_TILE // dil)
        perms.append((src[:, None] == j[None, :]).astype(BF16))
    return pl.pallas_call(
        _qkv_rope_kernel,
        grid=(t // QKV_TILE,),
        in_specs=[_rows(D_MODEL, QKV_TILE), _full(w_qkv_bf.shape),
                  pl.BlockSpec((QKV_TILE, LANES), lambda i: (i % tps, 0)),
                  pl.BlockSpec((QKV_TILE, LANES), lambda i: (i % tps, 0)),
                  _full((QKV_TILE, QKV_TILE)), _full((QKV_TILE, QKV_TILE))],
        out_specs=[_rows(width, QKV_TILE), _residue_spec(d4, QKV_TILE, width, tps),
                   _residue_spec(d16, QKV_TILE, width, tps)],
        out_shape=[jax.ShapeDtypeStruct((t, width), BF16),
                   jax.ShapeDtypeStruct((batch, d4, seq_len // d4, width), BF16),
                   jax.ShapeDtypeStruct((batch, d16, seq_len // d16, width), BF16)],
        compiler_params=_params(),
    )(h, w_qkv_bf, cos, sin, *perms)


def _attn_kernel(q_ref, kp_ref, kc_ref, vp_ref, vc_ref, o_ref, lse_ref):
    blk = ATT_BLOCK
    qi = lax.broadcasted_iota(jnp.int32, (blk, 2 * blk), 0)
    kj = lax.broadcasted_iota(jnp.int32, (blk, 2 * blk), 1)
    dist = blk + qi - kj
    k_min = jnp.where(pl.program_id(1) == 0, blk, 0)
    bias = jnp.where(dist >= 0, jnp.where(dist <= blk, jnp.where(kj >= k_min, 0.0, NEG_BIG),
                                          NEG_BIG), NEG_BIG)
    lane = lax.broadcasted_iota(jnp.int32, (blk, LANES), 1)
    low = lane < HEAD_DIM
    lse_tile = jnp.zeros((blk, LANES), F32)
    for hp in range(N_HEADS // 2):
        cols = slice(hp * LANES, (hp + 1) * LANES)
        q2 = q_ref[0, :, cols]
        kk = jnp.concatenate([kp_ref[0, :, cols], kc_ref[0, :, cols]], axis=0)
        vv = jnp.concatenate([vp_ref[0, :, cols], vc_ref[0, :, cols]], axis=0)
        halves = []
        for hh in range(2):
            head = 2 * hp + hh
            qm = jnp.where(low if hh == 0 else jnp.logical_not(low), q2, jnp.zeros_like(q2))
            s = lax.dot_general(qm, kk, (((1,), (1,)), ((), ())),
                                preferred_element_type=F32) + bias
            m = jnp.max(s, axis=-1, keepdims=True)
            p = jnp.exp(s - m)
            den = jnp.sum(p, axis=-1, keepdims=True)
            pv = jnp.dot(p.astype(BF16), vv, preferred_element_type=F32)
            halves.append(pv / den)
            lse_tile = jnp.where(lane == head, m + jnp.log(den), lse_tile)
        o_ref[0, :, cols] = jnp.where(low, halves[0], halves[1]).astype(BF16)
    lse_ref[0] = lse_tile


def _attn_pattern(qkv_sub):
    n_sub, sub_len, _ = qkv_sub.shape

    def cur(which):
        return pl.BlockSpec((1, ATT_BLOCK, D_MODEL), lambda s, n: (s, n, which))

    def prev(which):
        return pl.BlockSpec((1, ATT_BLOCK, D_MODEL), lambda s, n: (s, jnp.maximum(n - 1, 0), which))

    return pl.pallas_call(
        _attn_kernel,
        grid=(n_sub, sub_len // ATT_BLOCK),
        in_specs=[cur(0), prev(1), cur(1), prev(2), cur(2)],
        out_specs=[pl.BlockSpec((1, ATT_BLOCK, D_MODEL), lambda s, n: (s, n, 0)),
                   pl.BlockSpec((1, ATT_BLOCK, LANES), lambda s, n: (s, n, 0))],
        out_shape=[jax.ShapeDtypeStruct((n_sub, sub_len, D_MODEL), BF16),
                   jax.ShapeDtypeStruct((n_sub, sub_len, LANES), F32)],
        compiler_params=_params(2),
    )(*([qkv_sub] * 5))


def _odd_post_kernel(h_ref, o1_ref, l1_ref, o4_ref, l4_ref, o16_ref, l16_ref, expand_ref, wo_ref,
                     g_ref, b_ref, rwh_ref, rwl_ref, rb_ref, hn_ref, hnb_ref, idx_ref, gate_ref,
                     os4_ref, ls4_ref, os16_ref, ls16_ref):
    rows = h_ref.shape[0]
    n_col = D_MODEL // LANES
    for dil, o_ref, l_ref, os_ref, ls_ref in ((DILATIONS[1], o4_ref, l4_ref, os4_ref, ls4_ref),
                                              (DILATIONS[2], o16_ref, l16_ref, os16_ref, ls16_ref)):
        for r in range(dil):
            sel = pl.ds(r, rows // dil, stride=dil)
            ls_ref[sel, :] = l_ref[0, r]
            for c in range(n_col):
                os_ref[c, sel, :] = o_ref[0, r, :, c * LANES:(c + 1) * LANES].astype(F32)
    lses = (l1_ref[...], ls4_ref[...], ls16_ref[...])
    outs = (o1_ref[...].astype(F32),
            jnp.concatenate([os4_ref[c] for c in range(n_col)], axis=1),
            jnp.concatenate([os16_ref[c] for c in range(n_col)], axis=1))
    mx = jnp.maximum(jnp.maximum(lses[0], lses[1]), lses[2])
    es = [jnp.exp(l - mx) for l in lses]
    den = es[0] + es[1] + es[2]
    o = jnp.zeros((rows, D_MODEL), F32)
    for e, out in zip(es, outs):
        hi, lo = _split_bf16(e / den)
        w = (jnp.dot(hi, expand_ref[...], preferred_element_type=F32)
             + jnp.dot(lo, expand_ref[...], preferred_element_type=F32))
        o = o + w * out
    mix = jnp.dot(o.astype(BF16), wo_ref[...], preferred_element_type=F32)
    _norm_and_route(DN_ALPHA * h_ref[...] + mix, g_ref, b_ref, rwh_ref, rwl_ref, rb_ref,
                    hn_ref, hnb_ref, idx_ref, gate_ref)


def _odd_post(h, pattern_outs, consts, seq_len):
    t = h.shape[0]
    tps = seq_len // ROW_TILE
    (o1, l1), (o4, l4), (o16, l16) = pattern_outs
    d4, d16 = DILATIONS[1], DILATIONS[2]
    out_specs, out_shape = _post_outputs(t)
    return pl.pallas_call(
        _odd_post_kernel,
        grid=(t // ROW_TILE,),
        in_specs=[_rows(D_MODEL), _rows(D_MODEL), _rows(LANES),
                  _residue_spec(d4, ROW_TILE, D_MODEL, tps), _residue_spec(d4, ROW_TILE, LANES, tps),
                  _residue_spec(d16, ROW_TILE, D_MODEL, tps), _residue_spec(d16, ROW_TILE, LANES, tps)]
        + [_full(a.shape) for a in consts],
        out_specs=out_specs, out_shape=out_shape,
        scratch_shapes=[pltpu.VMEM((D_MODEL // LANES, ROW_TILE, LANES), F32),
                        pltpu.VMEM((ROW_TILE, LANES), F32)] * 2,
        compiler_params=_params(),
    )(h, o1, l1, o4, l4, o16, l16, *consts)


def _attention_layer(h, w_qkv, w_o, cos, sin, consts, batch, seq_len):
    t = h.shape[0]
    qkv1, qkv4, qkv16 = _qkv_rope(h, w_qkv.astype(BF16), cos, sin, batch, seq_len)
    outs = []
    for dil, qkv in zip(DILATIONS, (qkv1, qkv4, qkv16)):
        sub_len = seq_len // dil
        o, lse = _attn_pattern(qkv.reshape(batch * dil, sub_len, 3 * D_MODEL))
        if dil == 1:
            outs.append((o.reshape(t, D_MODEL), lse.reshape(t, LANES)))
        else:
            outs.append((o.reshape(batch, dil, sub_len, D_MODEL),
                         lse.reshape(batch, dil, sub_len, LANES)))
    head_of_lane = jnp.arange(D_MODEL, dtype=jnp.int32) // HEAD_DIM
    expand = (jnp.arange(LANES, dtype=jnp.int32)[:, None] == head_of_lane[None, :]).astype(BF16)
    return _odd_post(h, outs, [expand, w_o.astype(BF16), *consts], seq_len)


def _moe_kernel(be_ref, nused_ref, x_ref, wgu_ref, bgu_ref, wd_ref, bd_ref, y_ref,
                wgu_bf_ref, wd_bf_ref):
    i = pl.program_id(0)
    new_expert = jnp.logical_or(i == 0, be_ref[i] != be_ref[jnp.maximum(i - 1, 0)])

    @pl.when(new_expert)
    def _():
        wgu_bf_ref[...] = wgu_ref[0, 0].astype(BF16)
        wd_bf_ref[...] = wd_ref[0, 0].astype(BF16)

    @pl.when(i < nused_ref[0])
    def _():
        hid = jnp.dot(x_ref[...], wgu_bf_ref[...], preferred_element_type=F32) + bgu_ref[0]
        gate = jnp.minimum(hid[:, :D_MODEL], SWIGLU_LIMIT)
        lin = jnp.clip(hid[:, D_MODEL:], -SWIGLU_LIMIT, SWIGLU_LIMIT)
        act = (lin + 1.0) * (gate * _sigmoid(SWIGLU_ALPHA * gate))
        y = jnp.dot(act.astype(BF16), wd_bf_ref[...], preferred_element_type=F32) + bd_ref[0]
        y_ref[...] = y.astype(BF16)

    @pl.when(i >= nused_ref[0])
    def _():
        y_ref[...] = jnp.zeros_like(y_ref)


def _moe_experts(xs, block_e, n_used, layer, w_gu, b_gu, w_down, b_down):
    n_rows = xs.shape[0]
    grid_spec = pltpu.PrefetchScalarGridSpec(
        num_scalar_prefetch=2,
        grid=(n_rows // MOE_ROWS,),
        in_specs=[pl.BlockSpec((MOE_ROWS, D_MODEL), lambda i, be, nu: (i, 0)),
                  pl.BlockSpec((1, 1, D_MODEL, 2 * D_MODEL), lambda i, be, nu: (layer, be[i], 0, 0)),
                  pl.BlockSpec((1, 1, 2 * D_MODEL), lambda i, be, nu: (be[i], 0, 0)),
                  pl.BlockSpec((1, 1, D_MODEL, D_MODEL), lambda i, be, nu: (layer, be[i], 0, 0)),
                  pl.BlockSpec((1, 1, D_MODEL), lambda i, be, nu: (be[i], 0, 0))],
        out_specs=pl.BlockSpec((MOE_ROWS, D_MODEL), lambda i, be, nu: (i, 0)),
        scratch_shapes=[pltpu.VMEM((D_MODEL, 2 * D_MODEL), BF16), pltpu.VMEM((D_MODEL, D_MODEL), BF16)],
    )
    return pl.pallas_call(
        _moe_kernel,
        grid_spec=grid_spec,
        out_shape=jax.ShapeDtypeStruct((n_rows, D_MODEL), BF16),
        compiler_params=_params(),
    )(block_e, n_used, xs, w_gu, b_gu[:, None, :], w_down, b_down[:, None, :])


def _routing_tables(top_idx, t):
    n_assign = t * TOP_K
    flat_e = top_idx.T.reshape(-1)
    onehot = (flat_e[:, None] == jnp.arange(N_EXPERTS, dtype=jnp.int32)[None, :]).astype(jnp.int32)
    csum = jnp.cumsum(onehot, axis=0)
    counts = csum[-1]
    padded = (counts + MOE_ROWS - 1) // MOE_ROWS * MOE_ROWS
    pad_end = jnp.cumsum(padded)
    pad_start = pad_end - padded
    dest = jnp.sum(onehot * (csum - 1 + pad_start[None, :]), axis=1)
    n_blocks = n_assign // MOE_ROWS + N_EXPERTS
    flat_tok = jnp.arange(n_assign, dtype=jnp.int32) % t
    row_tok = (jnp.arange(n_blocks * MOE_ROWS, dtype=jnp.int32) % t).at[dest].set(
        flat_tok, unique_indices=True, mode="promise_in_bounds")
    block_start = jnp.arange(n_blocks, dtype=jnp.int32) * MOE_ROWS
    block_e = jnp.minimum(jnp.sum((block_start[:, None] >= pad_end[None, :]).astype(jnp.int32), axis=1),
                          N_EXPERTS - 1)
    n_used = (pad_end[-1:] // MOE_ROWS).astype(jnp.int32)
    return row_tok, dest, block_e, n_used


def _ffn_ln_kernel(h_ref, y4_ref, gate_ref, g_ref, b_ref, o_ref):
    gates = gate_ref[...]
    acc = DN_ALPHA * h_ref[...]
    for k in range(TOP_K):
        acc = acc + gates[:, k:k + 1] * y4_ref[k].astype(F32)
    o_ref[...] = _layer_norm(acc, g_ref[...], b_ref[...])


def _ffn_ln(h, y4, gates, g, b):
    t = h.shape[0]
    return pl.pallas_call(
        _ffn_ln_kernel,
        grid=(t // ROW_TILE,),
        in_specs=[_rows(D_MODEL), pl.BlockSpec((TOP_K, ROW_TILE, D_MODEL), lambda i: (0, i, 0)),
                  _rows(LANES), _full(g.shape), _full(b.shape)],
        out_specs=_rows(D_MODEL),
        out_shape=jax.ShapeDtypeStruct((t, D_MODEL), F32),
        compiler_params=_params(),
    )(h, y4, gates, g, b)


def _moe_layer(hn, hn_bf, idx_tile, gate_tile, layer, w_gu, b_gu, w_down, b_down, ln_g, ln_b):
    t = hn.shape[0]
    row_tok, dest, block_e, n_used = _routing_tables(idx_tile[:, :TOP_K], t)
    xs = hn_bf.at[row_tok].get(mode="promise_in_bounds")
    ys = _moe_experts(xs, block_e, n_used, layer, w_gu, b_gu, w_down, b_down)
    y4 = ys.at[dest].get(mode="promise_in_bounds").reshape(TOP_K, t, D_MODEL)
    return _ffn_ln(hn, y4, gate_tile, ln_g[None, :], ln_b[None, :])


def _router_consts(router_w, router_b):
    w = jnp.zeros((D_MODEL, LANES), F32).at[:, :N_EXPERTS].set(router_w)
    w_hi, w_lo = _split_bf16(w)
    b = jnp.full((1, LANES), NEG_BIG, F32).at[0, :N_EXPERTS].set(router_b)
    return w_hi, w_lo, b


def kernel(x, hy_w_in, conv_w, ssm_a_re, ssm_a_im, ssm_log_dt, ssm_b_re, ssm_b_im, ssm_c_re,
           ssm_c_im, ssm_d, ssm_w_glu, ssm_b_glu, hy_w_out, att_w_qkv, att_w_o, ln_mix_g,
           ln_mix_b, ln_ffn_g, ln_ffn_b, router_w, router_b, expert_w_gu, expert_b_gu,
           expert_w_down, expert_b_down):
    batch, seq_len, _ = x.shape
    n_pipes = 2 if batch % 2 == 0 else 1
    sub = batch // n_pipes
    t = sub * seq_len
    n_chunks = seq_len // SSM_CHUNK
    hs = [x[p * sub:(p + 1) * sub].reshape(t, D_MODEL) for p in range(n_pipes)]
    cos, sin = _rope_tables(seq_len)
    for layer in range(DEPTH):
        i = layer // 2
        consts = [ln_mix_g[layer][None, :], ln_mix_b[layer][None, :],
                  *_router_consts(router_w[layer], router_b[layer])]
        if layer % 2 == 0:
            w_in = hy_w_in[i].astype(BF16)
            w_out = hy_w_out[i].astype(BF16)
            tables = _s5_tables(ssm_a_re[i], ssm_a_im[i], ssm_log_dt[i], ssm_b_re[i], ssm_b_im[i],
                                ssm_c_re[i], ssm_c_im[i])
            post_consts = [ssm_d[i].reshape(1, SSM_WIDTH), ssm_w_glu[i].astype(BF16),
                           ssm_b_glu[i][None, :], w_out[:CONV_CH], w_out[CONV_CH:], *consts]
        for p, h in enumerate(hs):
            if layer % 2 == 0:
                y_conv, u = _inproj_conv(h, w_in, conv_w[i], seq_len)
                u_g = u.astype(BF16).reshape(sub, n_chunks, SSM_CHUNK, SSM_GROUPS, SSM_GROUP)
                u_g = u_g.transpose(3, 1, 0, 2, 4).reshape(SSM_GROUPS, n_chunks * sub,
                                                           SSM_CHUNK * SSM_GROUP)
                y_g = _s5_scan(u_g, tables, n_chunks, sub)
                y_s = y_g.reshape(SSM_GROUPS, n_chunks, sub, SSM_CHUNK, SSM_GROUP)
                y_s = y_s.transpose(2, 1, 3, 0, 4).reshape(t, SSM_WIDTH)
                hn, hn_bf, idx_tile, gate_tile = _even_post(h, y_conv, y_s, u, post_consts)
            else:
                hn, hn_bf, idx_tile, gate_tile = _attention_layer(
                    h, att_w_qkv[i], att_w_o[i], cos, sin, consts, sub, seq_len)
            hs[p] = _moe_layer(hn, hn_bf, idx_tile, gate_tile, layer, expert_w_gu, expert_b_gu[layer],
                               expert_w_down, expert_b_down[layer], ln_ffn_g[layer], ln_ffn_b[layer])
    return jnp.concatenate([h.reshape(sub, seq_len, D_MODEL) for h in hs], axis=0)
```

```python
import functools

import jax
import jax.numpy as jnp
from jax import lax
from jax.experimental import pallas as pl
from jax.experimental.pallas import tpu as pltpu

F32 = jnp.float32
BF16 = jnp.bfloat16

D_MODEL = 1024
DEPTH = 4
CONV_CH = 512
SSM_WIDTH = 512
SSM_GROUP = 16
SSM_GROUPS = 32
SSM_STATE = 64
N_HEADS = 16
HEAD_DIM = 64
ROPE_THETA = 10000.0
DILATIONS = (1, 4, 16)
ATT_BLOCK = 128
N_EXPERTS = 32
TOP_K = 4
SWIGLU_LIMIT = 7.0
SWIGLU_ALPHA = 1.702
DN_ALPHA = (2 * DEPTH) ** 0.25
LN_EPS = 1e-5

LANES = 128
ROW_TILE = 512
QKV_TILE = 256
SSM_CHUNK = 8
MOE_ROWS = 512
VMEM_LIMIT = 56 * 1024 * 1024
NEG_BIG = -1e30


def _params(n_axes=1):
    return pltpu.CompilerParams(dimension_semantics=("arbitrary",) * n_axes,
                                vmem_limit_bytes=VMEM_LIMIT)


def _full(shape):
    return pl.BlockSpec(shape, lambda *_: (0,) * len(shape))


def _rows(width, tile=ROW_TILE):
    return pl.BlockSpec((tile, width), lambda i: (i, 0))


def _layer_norm(x, g, b):
    mu = jnp.mean(x, axis=-1, keepdims=True)
    xc = x - mu
    var = jnp.mean(xc * xc, axis=-1, keepdims=True)
    return xc * lax.rsqrt(var + LN_EPS) * g + b


def _sigmoid(x):
    return 1.0 / (1.0 + jnp.exp(-x))


def _split_bf16(x):
    hi = x.astype(BF16)
    return hi, (x - hi.astype(F32)).astype(BF16)


def _route(hn, rw_hi_ref, rw_lo_ref, rb_ref, idx_ref, gate_ref):
    hi, lo = _split_bf16(hn)
    logits = (jnp.dot(hi, rw_hi_ref[...], preferred_element_type=F32)
              + jnp.dot(hi, rw_lo_ref[...], preferred_element_type=F32)
              + jnp.dot(lo, rw_hi_ref[...], preferred_element_type=F32)
              + rb_ref[...])
    lane = lax.broadcasted_iota(jnp.int32, logits.shape, 1)
    idx_tile = jnp.zeros(logits.shape, jnp.int32)
    val_tile = jnp.zeros(logits.shape, F32)
    top0 = None
    den = None
    for k in range(TOP_K):
        mx = jnp.max(logits, axis=-1, keepdims=True)
        first = jnp.min(jnp.where(logits == mx, lane, LANES), axis=-1, keepdims=True)
        if k == 0:
            top0 = mx
        e = jnp.exp(mx - top0)
        den = e if k == 0 else den + e
        idx_tile = jnp.where(lane == k, first, idx_tile)
        val_tile = jnp.where(lane == k, e, val_tile)
        logits = jnp.where(lane == first, -jnp.inf, logits)
    idx_ref[...] = idx_tile
    gate_ref[...] = val_tile / den


def _post_outputs(t):
    specs = [_rows(D_MODEL), _rows(D_MODEL), _rows(LANES), _rows(LANES)]
    shapes = [jax.ShapeDtypeStruct((t, D_MODEL), F32), jax.ShapeDtypeStruct((t, D_MODEL), BF16),
              jax.ShapeDtypeStruct((t, LANES), jnp.int32), jax.ShapeDtypeStruct((t, LANES), F32)]
    return specs, shapes


def _norm_and_route(pre, g_ref, b_ref, rwh_ref, rwl_ref, rb_ref, hn_ref, hnb_ref, idx_ref, gate_ref):
    hn = _layer_norm(pre, g_ref[...], b_ref[...])
    hn_ref[...] = hn
    hnb_ref[...] = hn.astype(BF16)
    _route(hn, rwh_ref, rwl_ref, rb_ref, idx_ref, gate_ref)


def _chunk_perm(rows):
    j = jnp.arange(rows, dtype=jnp.int32)
    n = rows // SSM_CHUNK
    src = (j % n) * SSM_CHUNK + j // n
    return (src[:, None] == j[None, :]).astype(BF16)


def _inproj_conv_kernel(x_ref, w_ref, cw_ref, perm_ref, yconv_ref, u_ref, usg_ref, carry_ref, *,
                        tiles_per_seq):
    @pl.when(pl.program_id(0) % tiles_per_seq == 0)
    def _():
        carry_ref[...] = jnp.zeros_like(carry_ref)

    proj = jnp.dot(x_ref[...].astype(BF16), w_ref[...], preferred_element_type=F32)
    gate_b = proj[:, :CONV_CH]
    gate_c = proj[:, CONV_CH:2 * CONV_CH]
    hid = proj[:, 2 * CONV_CH:3 * CONV_CH]
    v = gate_c * hid
    rows = v.shape[0]
    row = lax.broadcasted_iota(jnp.int32, v.shape, 0)
    prev1 = carry_ref[7:8, :]
    prev2 = carry_ref[6:7, :]
    vm1 = jnp.where(row == 0, prev1, pltpu.roll(v, 1, 0))
    vm2 = jnp.where(row == 0, prev2, jnp.where(row == 1, prev1, pltpu.roll(v, 2, 0)))
    conv = cw_ref[0:1, :] * vm2 + cw_ref[1:2, :] * vm1 + cw_ref[2:3, :] * v
    yconv_ref[...] = gate_b * conv
    u = proj[:, 3 * CONV_CH:]
    u_ref[...] = u
    carry_ref[...] = v[rows - 8:, :]
    by_pos = jnp.dot(perm_ref[...], u.astype(BF16), preferred_element_type=F32).astype(BF16)
    n = rows // SSM_CHUNK
    for sg in range(SSM_WIDTH // LANES):
        for s in range(SSM_CHUNK):
            usg_ref[sg, :, s * LANES:(s + 1) * LANES] = by_pos[s * n:(s + 1) * n,
                                                               sg * LANES:(sg + 1) * LANES]


def _inproj_conv(h, w_in_bf, conv_w, seq_len):
    t = h.shape[0]
    n_sg = SSM_WIDTH // LANES
    perm = _chunk_perm(ROW_TILE)
    return pl.pallas_call(
        functools.partial(_inproj_conv_kernel, tiles_per_seq=seq_len // ROW_TILE),
        grid=(t // ROW_TILE,),
        in_specs=[_rows(D_MODEL), _full(w_in_bf.shape), _full(conv_w.shape), _full(perm.shape)],
        out_specs=[_rows(CONV_CH), _rows(SSM_WIDTH),
                   pl.BlockSpec((n_sg, ROW_TILE // SSM_CHUNK, SSM_CHUNK * LANES), lambda i: (0, i, 0))],
        out_shape=[jax.ShapeDtypeStruct((t, CONV_CH), F32),
                   jax.ShapeDtypeStruct((t, SSM_WIDTH), F32),
                   jax.ShapeDtypeStruct((n_sg, t // SSM_CHUNK, SSM_CHUNK * LANES), BF16)],
        scratch_shapes=[pltpu.VMEM((8, CONV_CH), F32)],
        compiler_params=_params(),
    )(h, w_in_bf, conv_w, perm)


def _s5_tables(a_re, a_im, log_dt, b_re, b_im, c_re, c_im):
    q = SSM_CHUNK
    per_sg = LANES // SSM_GROUP
    n_sg = SSM_GROUPS // per_sg
    lam_re = jnp.minimum(a_re, -1e-4)
    lam_im = a_im
    dt = jnp.exp(log_dt)[:, None]
    mag = jnp.exp(lam_re * dt)
    ab_re = mag * jnp.cos(lam_im * dt)
    ab_im = mag * jnp.sin(lam_im * dt)
    nr, ni = ab_re - 1.0, ab_im
    den = lam_re * lam_re + lam_im * lam_im
    coef_re = ((nr * lam_re + ni * lam_im) / den)[..., None]
    coef_im = ((ni * lam_re - nr * lam_im) / den)[..., None]
    bb_re = coef_re * b_re - coef_im * b_im
    bb_im = coef_re * b_im + coef_im * b_re
    j = jnp.arange(q + 1, dtype=F32)
    pmag = jnp.exp((lam_re * dt)[..., None] * j)
    pang = (lam_im * dt)[..., None] * j
    pw_re = pmag * jnp.cos(pang)
    pw_im = pmag * jnp.sin(pang)
    ca_re = c_re[..., None] * pw_re[:, None] - c_im[..., None] * pw_im[:, None]
    ca_im = c_re[..., None] * pw_im[:, None] + c_im[..., None] * pw_re[:, None]
    kern = (jnp.einsum("gapj,gph->gjah", ca_re, bb_re, precision="highest")
            - jnp.einsum("gapj,gph->gjah", ca_im, bb_im, precision="highest"))[:, :q]
    lag = jnp.arange(q)[None, :] - jnp.arange(q)[:, None]
    toep = kern[:, jnp.clip(lag, 0, q - 1)]
    toep = jnp.where((lag >= 0)[None, :, :, None, None], toep, 0.0).transpose(0, 1, 4, 2, 3)
    rev = q - 1 - jnp.arange(q)
    pr = pw_re[:, :, rev]
    pi = pw_im[:, :, rev]
    inj_re = pr[..., None] * bb_re[:, :, None] - pi[..., None] * bb_im[:, :, None]
    inj_im = pr[..., None] * bb_im[:, :, None] + pi[..., None] * bb_re[:, :, None]
    inj_re = inj_re.transpose(0, 2, 3, 1)
    inj_im = inj_im.transpose(0, 2, 3, 1)
    out_re = ca_re[..., 1:].transpose(0, 2, 3, 1)
    out_im = (-ca_im[..., 1:]).transpose(0, 2, 3, 1)
    eye = jnp.eye(per_sg, dtype=F32)
    width = q * LANES
    n_state = per_sg * SSM_STATE

    def sg(a):
        return a.reshape((n_sg, per_sg) + a.shape[1:])

    toep_sg = jnp.einsum("Scshtk,cd->Sschtdk", sg(toep), eye).reshape(n_sg, width, width)
    inj_sg = jnp.concatenate(
        [jnp.einsum("Scshp,cd->Sschdp", sg(a), eye).reshape(n_sg, width, n_state)
         for a in (inj_re, inj_im)], axis=2)
    out_sg = jnp.concatenate(
        [jnp.einsum("Scpth,cd->Scptdh", sg(a), eye).reshape(n_sg, n_state, width)
         for a in (out_re, out_im)], axis=1)
    aq = jnp.stack([pw_re[..., q].reshape(n_sg, n_state), pw_im[..., q].reshape(n_sg, n_state)], axis=1)
    return toep_sg.astype(BF16), inj_sg.astype(BF16), out_sg.astype(BF16), aq


def _s5_kernel(u_ref, toep_ref, inj_ref, out_ref, aq_ref, y_ref, inj_scr, xs_scr, *, n_chunks):
    n_state = aq_ref.shape[2]
    u = u_ref[0]
    inj_scr[...] = jnp.dot(u, inj_ref[0], preferred_element_type=F32)
    aq_re = jnp.broadcast_to(aq_ref[0, 0:1, :], (8, n_state))
    aq_im = jnp.broadcast_to(aq_ref[0, 1:2, :], (8, n_state))
    sub = lax.broadcasted_iota(jnp.int32, (8, n_state), 0)

    def block(blk, carry):
        re, im = carry
        start = pl.multiple_of(blk * 8, 8)
        inj = inj_scr[pl.ds(start, 8), :]
        xs_re = jnp.zeros((8, n_state), F32)
        xs_im = jnp.zeros((8, n_state), F32)
        for j in range(8):
            xs_re = jnp.where(sub == j, re, xs_re)
            xs_im = jnp.where(sub == j, im, xs_im)
            in_re = jnp.broadcast_to(inj[j:j + 1, :n_state], (8, n_state))
            in_im = jnp.broadcast_to(inj[j:j + 1, n_state:], (8, n_state))
            re, im = aq_re * re - aq_im * im + in_re, aq_re * im + aq_im * re + in_im
        xs_scr[pl.ds(start, 8), :n_state] = xs_re
        xs_scr[pl.ds(start, 8), n_state:] = xs_im
        return re, im

    zero = jnp.zeros((8, n_state), F32)
    lax.fori_loop(0, n_chunks // 8, block, (zero, zero))
    y = jnp.dot(u, toep_ref[0], preferred_element_type=F32)
    y_ref[0] = y + jnp.dot(xs_scr[...].astype(BF16), out_ref[0], preferred_element_type=F32)


def _s5_scan(u_sg, tables, batch, n_chunks):
    toep, inj, out, aq = tables
    n_sg, _, width = u_sg.shape
    n_state2 = inj.shape[2]

    def per_sg(shape):
        return pl.BlockSpec((1,) + shape, lambda g, b: (g, 0, 0))

    return pl.pallas_call(
        functools.partial(_s5_kernel, n_chunks=n_chunks),
        grid=(n_sg, batch),
        in_specs=[pl.BlockSpec((1, n_chunks, width), lambda g, b: (g, b, 0)),
                  per_sg((width, width)), per_sg((width, n_state2)), per_sg((n_state2, width)),
                  per_sg((2, n_state2 // 2))],
        out_specs=pl.BlockSpec((1, n_chunks, width), lambda g, b: (g, b, 0)),
        out_shape=jax.ShapeDtypeStruct(u_sg.shape, F32),
        scratch_shapes=[pltpu.VMEM((n_chunks, n_state2), F32)] * 2,
        compiler_params=_params(2),
    )(u_sg, toep, inj, out, aq)


def _even_post_kernel(h_ref, yc_ref, ysg_ref, u_ref, perm_ref, d_ref, wglu_ref, bglu_ref, woc_ref,
                      wos_ref, g_ref, b_ref, rwh_ref, rwl_ref, rb_ref, hn_ref, hnb_ref, idx_ref,
                      gate_ref):
    n_sg = ysg_ref.shape[0]
    by_pos = jnp.concatenate(
        [jnp.concatenate([ysg_ref[sg, :, s * LANES:(s + 1) * LANES] for sg in range(n_sg)], axis=1)
         for s in range(SSM_CHUNK)], axis=0)
    hi, lo = _split_bf16(by_pos)
    y_scan = (jnp.dot(perm_ref[...], hi, preferred_element_type=F32)
              + jnp.dot(perm_ref[...], lo, preferred_element_type=F32))
    y = y_scan + d_ref[...] * u_ref[...]
    z = jax.nn.gelu(y)
    glu = jnp.dot(z.astype(BF16), wglu_ref[...], preferred_element_type=F32) + bglu_ref[...]
    z = z * _sigmoid(glu)
    mix = (jnp.dot(yc_ref[...].astype(BF16), woc_ref[...], preferred_element_type=F32)
           + jnp.dot(z.astype(BF16), wos_ref[...], preferred_element_type=F32))
    _norm_and_route(DN_ALPHA * h_ref[...] + mix, g_ref, b_ref, rwh_ref, rwl_ref, rb_ref,
                    hn_ref, hnb_ref, idx_ref, gate_ref)


def _even_post(h, y_conv, y_sg, u, consts):
    t = h.shape[0]
    n_sg = y_sg.shape[0]
    out_specs, out_shape = _post_outputs(t)
    consts = [_chunk_perm(ROW_TILE).T, *consts]
    return pl.pallas_call(
        _even_post_kernel,
        grid=(t // ROW_TILE,),
        in_specs=[_rows(D_MODEL), _rows(CONV_CH),
                  pl.BlockSpec((n_sg, ROW_TILE // SSM_CHUNK, SSM_CHUNK * LANES), lambda i: (0, i, 0)),
                  _rows(SSM_WIDTH)] + [_full(a.shape) for a in consts],
        out_specs=out_specs, out_shape=out_shape,
        compiler_params=_params(),
    )(h, y_conv, y_sg, u, *consts)


def _qkv_rope_kernel(x_ref, w_ref, cos_ref, sin_ref, p4_ref, p16_ref, o1_ref, o4_ref, o16_ref):
    qkv = jnp.dot(x_ref[...].astype(BF16), w_ref[...], preferred_element_type=F32)
    rows = qkv.shape[0]
    cos = jnp.concatenate([cos_ref[...]] * (D_MODEL // LANES), axis=1)
    sin = jnp.concatenate([sin_ref[...]] * (D_MODEL // LANES), axis=1)
    lane = lax.broadcasted_iota(jnp.int32, (rows, D_MODEL), 1)
    low_half = (lane % HEAD_DIM) < (HEAD_DIM // 2)

    def rope(xs):
        up = pltpu.roll(xs, D_MODEL - HEAD_DIM // 2, 1)
        down = pltpu.roll(xs, HEAD_DIM // 2, 1)
        return xs * cos + jnp.where(low_half, up, down) * sin

    parts = (rope(qkv[:, :D_MODEL]) * (HEAD_DIM ** -0.5), rope(qkv[:, D_MODEL:2 * D_MODEL]),
             qkv[:, 2 * D_MODEL:])
    for which, part in enumerate(parts):
        cols = slice(which * D_MODEL, (which + 1) * D_MODEL)
        part = part.astype(BF16)
        o1_ref[:, cols] = part
        for dil, perm_ref, ref in ((DILATIONS[1], p4_ref, o4_ref), (DILATIONS[2], p16_ref, o16_ref)):
            split = jnp.dot(perm_ref[...], part, preferred_element_type=F32).astype(BF16)
            n = rows // dil
            for r in range(dil):
                ref[0, r, :, cols] = split[r * n:(r + 1) * n, :]


def _rope_tables(seq_len):
    half = HEAD_DIM // 2
    inv = ROPE_THETA ** (-jnp.arange(half, dtype=F32) / half)
    ang = jnp.arange(seq_len, dtype=F32)[:, None] * inv[None, :]
    cos = jnp.tile(jnp.cos(ang), (1, LANES // half))
    sin = jnp.sin(ang)
    sin = jnp.tile(jnp.concatenate([-sin, sin], axis=1), (1, LANES // HEAD_DIM))
    return cos, sin


def _residue_spec(dil, rows, width, tiles_per_seq):
    return pl.BlockSpec((1, dil, rows // dil, width),
                        lambda i: (i // tiles_per_seq, 0, i % tiles_per_seq, 0))


def _qkv_rope(h, w_qkv_bf, cos, sin, batch, seq_len):
    t = h.shape[0]
    tps = seq_len // QKV_TILE
    d4, d16 = DILATIONS[1], DILATIONS[2]
    width = 3 * D_MODEL
    perms = []
    for dil in (d4, d16):
        j = jnp.arange(QKV_TILE, dtype=jnp.int32)
        src = (j % (QKV_TILE // dil)) * dil + j // (QKV_TILE // dil)
        perms.append((src[:, None] == j[None, :]).astype(BF16))
    return pl.pallas_call(
        _qkv_rope_kernel,
        grid=(t // QKV_TILE,),
        in_specs=[_rows(D_MODEL, QKV_TILE), _full(w_qkv_bf.shape),
                  pl.BlockSpec((QKV_TILE, LANES), lambda i: (i % tps, 0)),
                  pl.BlockSpec((QKV_TILE, LANES), lambda i: (i % tps, 0)),
                  _full((QKV_TILE, QKV_TILE)), _full((QKV_TILE, QKV_TILE))],
        out_specs=[_rows(width, QKV_TILE), _residue_spec(d4, QKV_TILE, width, tps),
                   _residue_spec(d16, QKV_TILE, width, tps)],
        out_shape=[jax.ShapeDtypeStruct((t, width), BF16),
                   jax.ShapeDtypeStruct((batch, d4, seq_len // d4, width), BF16),
                   jax.ShapeDtypeStruct((batch, d16, seq_len // d16, width), BF16)],
        compiler_params=_params(),
    )(h, w_qkv_bf, cos, sin, *perms)


def _attn_kernel(q_ref, kp_ref, kc_ref, vp_ref, vc_ref, o_ref, lse_ref):
    blk = ATT_BLOCK
    qi = lax.broadcasted_iota(jnp.int32, (blk, 2 * blk), 0)
    kj = lax.broadcasted_iota(jnp.int32, (blk, 2 * blk), 1)
    dist = blk + qi - kj
    k_min = jnp.where(pl.program_id(1) == 0, blk, 0)
    bias = jnp.where(dist >= 0, jnp.where(dist <= blk, jnp.where(kj >= k_min, 0.0, NEG_BIG),
                                          NEG_BIG), NEG_BIG)
    lane = lax.broadcasted_iota(jnp.int32, (blk, LANES), 1)
    low = lane < HEAD_DIM
    lse_tile = jnp.zeros((blk, LANES), F32)
    for hp in range(N_HEADS // 2):
        cols = slice(hp * LANES, (hp + 1) * LANES)
        q2 = q_ref[0, :, cols]
        kk = jnp.concatenate([kp_ref[0, :, cols], kc_ref[0, :, cols]], axis=0)
        vv = jnp.concatenate([vp_ref[0, :, cols], vc_ref[0, :, cols]], axis=0)
        halves = []
        for hh in range(2):
            head = 2 * hp + hh
            qm = jnp.where(low if hh == 0 else jnp.logical_not(low), q2, jnp.zeros_like(q2))
            s = lax.dot_general(qm, kk, (((1,), (1,)), ((), ())),
                                preferred_element_type=F32) + bias
            m = jnp.max(s, axis=-1, keepdims=True)
            p = jnp.exp(s - m)
            den = jnp.sum(p, axis=-1, keepdims=True)
            pv = jnp.dot(p.astype(BF16), vv, preferred_element_type=F32)
            halves.append(pv / den)
            lse_tile = jnp.where(lane == head, m + jnp.log(den), lse_tile)
        o_ref[0, :, cols] = jnp.where(low, halves[0], halves[1]).astype(BF16)
    lse_ref[0] = lse_tile


def _attn_pattern(qkv_sub):
    n_sub, sub_len, _ = qkv_sub.shape

    def cur(which):
        return pl.BlockSpec((1, ATT_BLOCK, D_MODEL), lambda s, n: (s, n, which))

    def prev(which):
        return pl.BlockSpec((1, ATT_BLOCK, D_MODEL), lambda s, n: (s, jnp.maximum(n - 1, 0), which))

    return pl.pallas_call(
        _attn_kernel,
        grid=(n_sub, sub_len // ATT_BLOCK),
        in_specs=[cur(0), prev(1), cur(1), prev(2), cur(2)],
        out_specs=[pl.BlockSpec((1, ATT_BLOCK, D_MODEL), lambda s, n: (s, n, 0)),
                   pl.BlockSpec((1, ATT_BLOCK, LANES), lambda s, n: (s, n, 0))],
        out_shape=[jax.ShapeDtypeStruct((n_sub, sub_len, D_MODEL), BF16),
                   jax.ShapeDtypeStruct((n_sub, sub_len, LANES), F32)],
        compiler_params=_params(2),
    )(*([qkv_sub] * 5))


def _odd_post_kernel(h_ref, o1_ref, l1_ref, o4_ref, l4_ref, o16_ref, l16_ref, expand_ref, wo_ref,
                     g_ref, b_ref, rwh_ref, rwl_ref, rb_ref, hn_ref, hnb_ref, idx_ref, gate_ref,
                     os4_ref, ls4_ref, os16_ref, ls16_ref):
    rows = h_ref.shape[0]
    n_col = D_MODEL // LANES
    for dil, o_ref, l_ref, os_ref, ls_ref in ((DILATIONS[1], o4_ref, l4_ref, os4_ref, ls4_ref),
                                              (DILATIONS[2], o16_ref, l16_ref, os16_ref, ls16_ref)):
        for r in range(dil):
            sel = pl.ds(r, rows // dil, stride=dil)
            ls_ref[sel, :] = l_ref[0, r]
            for c in range(n_col):
                os_ref[c, sel, :] = o_ref[0, r, :, c * LANES:(c + 1) * LANES].astype(F32)
    lses = (l1_ref[...], ls4_ref[...], ls16_ref[...])
    outs = (o1_ref[...].astype(F32),
            jnp.concatenate([os4_ref[c] for c in range(n_col)], axis=1),
            jnp.concatenate([os16_ref[c] for c in range(n_col)], axis=1))
    mx = jnp.maximum(jnp.maximum(lses[0], lses[1]), lses[2])
    es = [jnp.exp(l - mx) for l in lses]
    den = es[0] + es[1] + es[2]
    o = jnp.zeros((rows, D_MODEL), F32)
    for e, out in zip(es, outs):
        hi, lo = _split_bf16(e / den)
        w = (jnp.dot(hi, expand_ref[...], preferred_element_type=F32)
             + jnp.dot(lo, expand_ref[...], preferred_element_type=F32))
        o = o + w * out
    mix = jnp.dot(o.astype(BF16), wo_ref[...], preferred_element_type=F32)
    _norm_and_route(DN_ALPHA * h_ref[...] + mix, g_ref, b_ref, rwh_ref, rwl_ref, rb_ref,
                    hn_ref, hnb_ref, idx_ref, gate_ref)


def _odd_post(h, pattern_outs, consts, seq_len):
    t = h.shape[0]
    tps = seq_len // ROW_TILE
    (o1, l1), (o4, l4), (o16, l16) = pattern_outs
    d4, d16 = DILATIONS[1], DILATIONS[2]
    out_specs, out_shape = _post_outputs(t)
    return pl.pallas_call(
        _odd_post_kernel,
        grid=(t // ROW_TILE,),
        in_specs=[_rows(D_MODEL), _rows(D_MODEL), _rows(LANES),
                  _residue_spec(d4, ROW_TILE, D_MODEL, tps), _residue_spec(d4, ROW_TILE, LANES, tps),
                  _residue_spec(d16, ROW_TILE, D_MODEL, tps), _residue_spec(d16, ROW_TILE, LANES, tps)]
        + [_full(a.shape) for a in consts],
        out_specs=out_specs, out_shape=out_shape,
        scratch_shapes=[pltpu.VMEM((D_MODEL // LANES, ROW_TILE, LANES), F32),
                        pltpu.VMEM((ROW_TILE, LANES), F32)] * 2,
        compiler_params=_params(),
    )(h, o1, l1, o4, l4, o16, l16, *consts)


def _attention_layer(h, w_qkv, w_o, cos, sin, consts, batch, seq_len):
    t = h.shape[0]
    qkv1, qkv4, qkv16 = _qkv_rope(h, w_qkv.astype(BF16), cos, sin, batch, seq_len)
    outs = []
    for dil, qkv in zip(DILATIONS, (qkv1, qkv4, qkv16)):
        sub_len = seq_len // dil
        o, lse = _attn_pattern(qkv.reshape(batch * dil, sub_len, 3 * D_MODEL))
        if dil == 1:
            outs.append((o.reshape(t, D_MODEL), lse.reshape(t, LANES)))
        else:
            outs.append((o.reshape(batch, dil, sub_len, D_MODEL),
                         lse.reshape(batch, dil, sub_len, LANES)))
    head_of_lane = jnp.arange(D_MODEL, dtype=jnp.int32) // HEAD_DIM
    expand = (jnp.arange(LANES, dtype=jnp.int32)[:, None] == head_of_lane[None, :]).astype(BF16)
    return _odd_post(h, outs, [expand, w_o.astype(BF16), *consts], seq_len)


def _moe_kernel(be_ref, nused_ref, x_ref, wgu_ref, bgu_ref, wd_ref, bd_ref, y_ref,
                wgu_bf_ref, wd_bf_ref):
    i = pl.program_id(0)
    new_expert = jnp.logical_or(i == 0, be_ref[i] != be_ref[jnp.maximum(i - 1, 0)])

    @pl.when(new_expert)
    def _():
        wgu_bf_ref[...] = wgu_ref[0, 0].astype(BF16)
        wd_bf_ref[...] = wd_ref[0, 0].astype(BF16)

    @pl.when(i < nused_ref[0])
    def _():
        hid = jnp.dot(x_ref[...], wgu_bf_ref[...], preferred_element_type=F32) + bgu_ref[0]
        gate = jnp.minimum(hid[:, :D_MODEL], SWIGLU_LIMIT)
        lin = jnp.clip(hid[:, D_MODEL:], -SWIGLU_LIMIT, SWIGLU_LIMIT)
        act = (lin + 1.0) * (gate * _sigmoid(SWIGLU_ALPHA * gate))
        y = jnp.dot(act.astype(BF16), wd_bf_ref[...], preferred_element_type=F32) + bd_ref[0]
        y_ref[...] = y.astype(BF16)

    @pl.when(i >= nused_ref[0])
    def _():
        y_ref[...] = jnp.zeros_like(y_ref)


def _moe_experts(xs, block_e, n_used, layer, w_gu, b_gu, w_down, b_down):
    n_rows = xs.shape[0]
    grid_spec = pltpu.PrefetchScalarGridSpec(
        num_scalar_prefetch=2,
        grid=(n_rows // MOE_ROWS,),
        in_specs=[pl.BlockSpec((MOE_ROWS, D_MODEL), lambda i, be, nu: (i, 0)),
                  pl.BlockSpec((1, 1, D_MODEL, 2 * D_MODEL), lambda i, be, nu: (layer, be[i], 0, 0)),
                  pl.BlockSpec((1, 1, 2 * D_MODEL), lambda i, be, nu: (be[i], 0, 0)),
                  pl.BlockSpec((1, 1, D_MODEL, D_MODEL), lambda i, be, nu: (layer, be[i], 0, 0)),
                  pl.BlockSpec((1, 1, D_MODEL), lambda i, be, nu: (be[i], 0, 0))],
        out_specs=pl.BlockSpec((MOE_ROWS, D_MODEL), lambda i, be, nu: (i, 0)),
        scratch_shapes=[pltpu.VMEM((D_MODEL, 2 * D_MODEL), BF16), pltpu.VMEM((D_MODEL, D_MODEL), BF16)],
    )
    return pl.pallas_call(
        _moe_kernel,
        grid_spec=grid_spec,
        out_shape=jax.ShapeDtypeStruct((n_rows, D_MODEL), BF16),
        compiler_params=_params(),
    )(block_e, n_used, xs, w_gu, b_gu[:, None, :], w_down, b_down[:, None, :])


def _routing_tables(top_idx, t):
    n_assign = t * TOP_K
    flat_e = top_idx.T.reshape(-1)
    onehot = (flat_e[:, None] == jnp.arange(N_EXPERTS, dtype=jnp.int32)[None, :]).astype(jnp.int32)
    csum = jnp.cumsum(onehot, axis=0)
    counts = csum[-1]
    padded = (counts + MOE_ROWS - 1) // MOE_ROWS * MOE_ROWS
    pad_end = jnp.cumsum(padded)
    pad_start = pad_end - padded
    dest = jnp.sum(onehot * (csum - 1 + pad_start[None, :]), axis=1)
    n_blocks = n_assign // MOE_ROWS + N_EXPERTS
    flat_tok = jnp.arange(n_assign, dtype=jnp.int32) % t
    row_tok = (jnp.arange(n_blocks * MOE_ROWS, dtype=jnp.int32) % t).at[dest].set(
        flat_tok, unique_indices=True, mode="promise_in_bounds")
    block_start = jnp.arange(n_blocks, dtype=jnp.int32) * MOE_ROWS
    block_e = jnp.minimum(jnp.sum((block_start[:, None] >= pad_end[None, :]).astype(jnp.int32), axis=1),
                          N_EXPERTS - 1)
    n_used = (pad_end[-1:] // MOE_ROWS).astype(jnp.int32)
    return row_tok, dest, block_e, n_used


def _ffn_ln_kernel(h_ref, y4_ref, gate_ref, g_ref, b_ref, o_ref):
    gates = gate_ref[...]
    acc = DN_ALPHA * h_ref[...]
    for k in range(TOP_K):
        acc = acc + gates[:, k:k + 1] * y4_ref[k].astype(F32)
    o_ref[...] = _layer_norm(acc, g_ref[...], b_ref[...])


def _ffn_ln(h, y4, gates, g, b):
    t = h.shape[0]
    return pl.pallas_call(
        _ffn_ln_kernel,
        grid=(t // ROW_TILE,),
        in_specs=[_rows(D_MODEL), pl.BlockSpec((TOP_K, ROW_TILE, D_MODEL), lambda i: (0, i, 0)),
                  _rows(LANES), _full(g.shape), _full(b.shape)],
        out_specs=_rows(D_MODEL),
        out_shape=jax.ShapeDtypeStruct((t, D_MODEL), F32),
        compiler_params=_params(),
    )(h, y4, gates, g, b)


def _moe_layer(hn, hn_bf, idx_tile, gate_tile, layer, w_gu, b_gu, w_down, b_down, ln_g, ln_b):
    t = hn.shape[0]
    row_tok, dest, block_e, n_used = _routing_tables(idx_tile[:, :TOP_K], t)
    xs = hn_bf.at[row_tok].get(mode="promise_in_bounds")
    ys = _moe_experts(xs, block_e, n_used, layer, w_gu, b_gu, w_down, b_down)
    y4 = ys.at[dest].get(mode="promise_in_bounds").reshape(TOP_K, t, D_MODEL)
    return _ffn_ln(hn, y4, gate_tile, ln_g[None, :], ln_b[None, :])


def _router_consts(router_w, router_b):
    w = jnp.zeros((D_MODEL, LANES), F32).at[:, :N_EXPERTS].set(router_w)
    w_hi, w_lo = _split_bf16(w)
    b = jnp.full((1, LANES), NEG_BIG, F32).at[0, :N_EXPERTS].set(router_b)
    return w_hi, w_lo, b


def kernel(x, hy_w_in, conv_w, ssm_a_re, ssm_a_im, ssm_log_dt, ssm_b_re, ssm_b_im, ssm_c_re,
           ssm_c_im, ssm_d, ssm_w_glu, ssm_b_glu, hy_w_out, att_w_qkv, att_w_o, ln_mix_g,
           ln_mix_b, ln_ffn_g, ln_ffn_b, router_w, router_b, expert_w_gu, expert_b_gu,
           expert_w_down, expert_b_down):
    batch, seq_len, _ = x.shape
    t = batch * seq_len
    h = x.reshape(t, D_MODEL)
    cos, sin = _rope_tables(seq_len)
    for layer in range(DEPTH):
        i = layer // 2
        consts = [ln_mix_g[layer][None, :], ln_mix_b[layer][None, :],
                  *_router_consts(router_w[layer], router_b[layer])]
        if layer % 2 == 0:
            y_conv, u, u_sg = _inproj_conv(h, hy_w_in[i].astype(BF16), conv_w[i], seq_len)
            tables = _s5_tables(ssm_a_re[i], ssm_a_im[i], ssm_log_dt[i], ssm_b_re[i], ssm_b_im[i],
                                ssm_c_re[i], ssm_c_im[i])
            y_sg = _s5_scan(u_sg, tables, batch, seq_len // SSM_CHUNK)
            w_out = hy_w_out[i].astype(BF16)
            hn, hn_bf, idx_tile, gate_tile = _even_post(
                h, y_conv, y_sg, u,
                [ssm_d[i].reshape(1, SSM_WIDTH), ssm_w_glu[i].astype(BF16), ssm_b_glu[i][None, :],
                 w_out[:CONV_CH], w_out[CONV_CH:], *consts])
        else:
            hn, hn_bf, idx_tile, gate_tile = _attention_layer(
                h, att_w_qkv[i], att_w_o[i], cos, sin, consts, batch, seq_len)
        h = _moe_layer(hn, hn_bf, idx_tile, gate_tile, layer, expert_w_gu, expert_b_gu[layer],
                       expert_w_down, expert_b_down[layer], ln_ffn_g[layer], ln_ffn_b[layer])
    return h.reshape(batch, seq_len, D_MODEL)
```

```python
import functools

import jax
import jax.numpy as jnp
from jax import lax
from jax.experimental import pallas as pl
from jax.experimental.pallas import tpu as pltpu

F32 = jnp.float32
BF16 = jnp.bfloat16

D_MODEL = 1024
DEPTH = 4
CONV_CH = 512
SSM_WIDTH = 512
SSM_GROUP = 16
SSM_GROUPS = 32
SSM_STATE = 64
N_HEADS = 16
HEAD_DIM = 64
ROPE_THETA = 10000.0
DILATIONS = (1, 4, 16)
ATT_BLOCK = 128
N_EXPERTS = 32
TOP_K = 4
SWIGLU_LIMIT = 7.0
SWIGLU_ALPHA = 1.702
DN_ALPHA = (2 * DEPTH) ** 0.25
LN_EPS = 1e-5

LANES = 128
ROW_TILE = 512
QKV_TILE = 256
SSM_CHUNK = 8
MOE_ROWS = 512
VMEM_LIMIT = 56 * 1024 * 1024
NEG_BIG = -1e30


def _params(n_axes=1):
    return pltpu.CompilerParams(dimension_semantics=("arbitrary",) * n_axes,
                                vmem_limit_bytes=VMEM_LIMIT)


def _full(shape):
    return pl.BlockSpec(shape, lambda *_: (0,) * len(shape))


def _rows(width, tile=ROW_TILE):
    return pl.BlockSpec((tile, width), lambda i: (i, 0))


def _layer_norm(x, g, b):
    mu = jnp.mean(x, axis=-1, keepdims=True)
    xc = x - mu
    var = jnp.mean(xc * xc, axis=-1, keepdims=True)
    return xc * lax.rsqrt(var + LN_EPS) * g + b


def _sigmoid(x):
    return 1.0 / (1.0 + jnp.exp(-x))


def _split_bf16(x):
    hi = x.astype(BF16)
    return hi, (x - hi.astype(F32)).astype(BF16)


def _route(hn, rw_hi_ref, rw_lo_ref, rb_ref, tri_ref, idx_ref, gate_ref, cnt_ref, seen_ref):
    @pl.when(pl.program_id(0) == 0)
    def _():
        seen_ref[...] = jnp.zeros_like(seen_ref)

    hi, lo = _split_bf16(hn)
    logits = (jnp.dot(hi, rw_hi_ref[...], preferred_element_type=F32)
              + jnp.dot(hi, rw_lo_ref[...], preferred_element_type=F32)
              + jnp.dot(lo, rw_hi_ref[...], preferred_element_type=F32)
              + rb_ref[...])
    lane = lax.broadcasted_iota(jnp.int32, logits.shape, 1)
    idx_tile = jnp.zeros(logits.shape, jnp.int32)
    val_tile = jnp.zeros(logits.shape, F32)
    top0 = None
    den = None
    seen = seen_ref[0:1, :]
    for k in range(TOP_K):
        mx = jnp.max(logits, axis=-1, keepdims=True)
        first = jnp.min(jnp.where(logits == mx, lane, LANES), axis=-1, keepdims=True)
        if k == 0:
            top0 = mx
        e = jnp.exp(mx - top0)
        den = e if k == 0 else den + e
        chosen = lane == first
        onehot = jnp.where(chosen, 1.0, 0.0)
        before = jnp.dot(tri_ref[...], onehot.astype(BF16), preferred_element_type=F32) + seen
        rank = jnp.sum(jnp.where(chosen, before, 0.0), axis=-1, keepdims=True)
        idx_tile = jnp.where(lane == k, first, idx_tile)
        idx_tile = jnp.where(lane == TOP_K + k, rank.astype(jnp.int32), idx_tile)
        val_tile = jnp.where(lane == k, e, val_tile)
        logits = jnp.where(chosen, -jnp.inf, logits)
        seen = seen + jnp.sum(onehot, axis=0, keepdims=True)
    idx_ref[...] = idx_tile
    gate_ref[...] = val_tile / den
    seen_ref[...] = jnp.broadcast_to(seen, seen_ref.shape)
    cnt_ref[...] = jnp.broadcast_to(seen, cnt_ref.shape)


def _post_outputs(t):
    specs = [_rows(D_MODEL), _rows(D_MODEL), _rows(LANES), _rows(LANES), _full((8, LANES))]
    shapes = [jax.ShapeDtypeStruct((t, D_MODEL), F32), jax.ShapeDtypeStruct((t, D_MODEL), BF16),
              jax.ShapeDtypeStruct((t, LANES), jnp.int32), jax.ShapeDtypeStruct((t, LANES), F32),
              jax.ShapeDtypeStruct((8, LANES), F32)]
    return specs, shapes


def _norm_and_route(pre, g_ref, b_ref, route_refs, hn_ref, hnb_ref, idx_ref, gate_ref, cnt_ref,
                    seen_ref):
    hn = _layer_norm(pre, g_ref[...], b_ref[...])
    hn_ref[...] = hn
    hnb_ref[...] = hn.astype(BF16)
    _route(hn, *route_refs, idx_ref, gate_ref, cnt_ref, seen_ref)


def _chunk_perm(rows):
    j = jnp.arange(rows, dtype=jnp.int32)
    n = rows // SSM_CHUNK
    src = (j % n) * SSM_CHUNK + j // n
    return (src[:, None] == j[None, :]).astype(BF16)


def _inproj_conv_kernel(x_ref, w_ref, cw_ref, perm_ref, yconv_ref, u_ref, usg_ref, carry_ref, *,
                        tiles_per_seq):
    @pl.when(pl.program_id(0) % tiles_per_seq == 0)
    def _():
        carry_ref[...] = jnp.zeros_like(carry_ref)

    proj = jnp.dot(x_ref[...].astype(BF16), w_ref[...], preferred_element_type=F32)
    gate_b = proj[:, :CONV_CH]
    gate_c = proj[:, CONV_CH:2 * CONV_CH]
    hid = proj[:, 2 * CONV_CH:3 * CONV_CH]
    v = gate_c * hid
    rows = v.shape[0]
    row = lax.broadcasted_iota(jnp.int32, v.shape, 0)
    prev1 = carry_ref[7:8, :]
    prev2 = carry_ref[6:7, :]
    vm1 = jnp.where(row == 0, prev1, pltpu.roll(v, 1, 0))
    vm2 = jnp.where(row == 0, prev2, jnp.where(row == 1, prev1, pltpu.roll(v, 2, 0)))
    conv = cw_ref[0:1, :] * vm2 + cw_ref[1:2, :] * vm1 + cw_ref[2:3, :] * v
    yconv_ref[...] = gate_b * conv
    u = proj[:, 3 * CONV_CH:]
    u_ref[...] = u
    carry_ref[...] = v[rows - 8:, :]
    by_pos = jnp.dot(perm_ref[...], u.astype(BF16), preferred_element_type=F32).astype(BF16)
    n = rows // SSM_CHUNK
    for sg in range(SSM_WIDTH // LANES):
        for s in range(SSM_CHUNK):
            usg_ref[sg, :, s * LANES:(s + 1) * LANES] = by_pos[s * n:(s + 1) * n,
                                                               sg * LANES:(sg + 1) * LANES]


def _inproj_conv(h, w_in_bf, conv_w, seq_len):
    t = h.shape[0]
    n_sg = SSM_WIDTH // LANES
    perm = _chunk_perm(ROW_TILE)
    return pl.pallas_call(
        functools.partial(_inproj_conv_kernel, tiles_per_seq=seq_len // ROW_TILE),
        grid=(t // ROW_TILE,),
        in_specs=[_rows(D_MODEL), _full(w_in_bf.shape), _full(conv_w.shape), _full(perm.shape)],
        out_specs=[_rows(CONV_CH), _rows(SSM_WIDTH),
                   pl.BlockSpec((n_sg, ROW_TILE // SSM_CHUNK, SSM_CHUNK * LANES), lambda i: (0, i, 0))],
        out_shape=[jax.ShapeDtypeStruct((t, CONV_CH), F32),
                   jax.ShapeDtypeStruct((t, SSM_WIDTH), F32),
                   jax.ShapeDtypeStruct((n_sg, t // SSM_CHUNK, SSM_CHUNK * LANES), BF16)],
        scratch_shapes=[pltpu.VMEM((8, CONV_CH), F32)],
        compiler_params=_params(),
    )(h, w_in_bf, conv_w, perm)


def _s5_tables(a_re, a_im, log_dt, b_re, b_im, c_re, c_im):
    q = SSM_CHUNK
    per_sg = LANES // SSM_GROUP
    n_sg = SSM_GROUPS // per_sg
    lam_re = jnp.minimum(a_re, -1e-4)
    lam_im = a_im
    dt = jnp.exp(log_dt)[:, None]
    mag = jnp.exp(lam_re * dt)
    ab_re = mag * jnp.cos(lam_im * dt)
    ab_im = mag * jnp.sin(lam_im * dt)
    nr, ni = ab_re - 1.0, ab_im
    den = lam_re * lam_re + lam_im * lam_im
    coef_re = ((nr * lam_re + ni * lam_im) / den)[..., None]
    coef_im = ((ni * lam_re - nr * lam_im) / den)[..., None]
    bb_re = coef_re * b_re - coef_im * b_im
    bb_im = coef_re * b_im + coef_im * b_re
    j = jnp.arange(q + 1, dtype=F32)
    pmag = jnp.exp((lam_re * dt)[..., None] * j)
    pang = (lam_im * dt)[..., None] * j
    pw_re = pmag * jnp.cos(pang)
    pw_im = pmag * jnp.sin(pang)
    ca_re = c_re[..., None] * pw_re[:, None] - c_im[..., None] * pw_im[:, None]
    ca_im = c_re[..., None] * pw_im[:, None] + c_im[..., None] * pw_re[:, None]
    kern = (jnp.einsum("gapj,gph->gjah", ca_re, bb_re, precision="highest")
            - jnp.einsum("gapj,gph->gjah", ca_im, bb_im, precision="highest"))[:, :q]
    lag = jnp.arange(q)[None, :] - jnp.arange(q)[:, None]
    toep = kern[:, jnp.clip(lag, 0, q - 1)]
    toep = jnp.where((lag >= 0)[None, :, :, None, None], toep, 0.0).transpose(0, 1, 4, 2, 3)
    rev = q - 1 - jnp.arange(q)
    pr = pw_re[:, :, rev]
    pi = pw_im[:, :, rev]
    inj_re = pr[..., None] * bb_re[:, :, None] - pi[..., None] * bb_im[:, :, None]
    inj_im = pr[..., None] * bb_im[:, :, None] + pi[..., None] * bb_re[:, :, None]
    inj_re = inj_re.transpose(0, 2, 3, 1)
    inj_im = inj_im.transpose(0, 2, 3, 1)
    out_re = ca_re[..., 1:].transpose(0, 2, 3, 1)
    out_im = (-ca_im[..., 1:]).transpose(0, 2, 3, 1)
    eye = jnp.eye(per_sg, dtype=F32)
    width = q * LANES
    n_state = per_sg * SSM_STATE

    def sg(a):
        return a.reshape((n_sg, per_sg) + a.shape[1:])

    def block_diag(a, member_second):
        a = a.reshape(a.shape[:-1] + (1, a.shape[-1]))
        a = a * eye.reshape((1, per_sg) + (1,) * (a.ndim - 4) + (per_sg, 1))
        return jnp.moveaxis(a, 1, 2) if member_second else a

    toep_sg = block_diag(sg(toep), True).reshape(n_sg, width, width)
    inj_sg = jnp.concatenate([block_diag(sg(a), True).reshape(n_sg, width, n_state)
                              for a in (inj_re, inj_im)], axis=2)
    out_sg = jnp.concatenate([block_diag(sg(a), False).reshape(n_sg, n_state, width)
                              for a in (out_re, out_im)], axis=1)
    aq = jnp.stack([pw_re[..., q].reshape(n_sg, n_state), pw_im[..., q].reshape(n_sg, n_state)], axis=1)
    return toep_sg.astype(BF16), inj_sg.astype(BF16), out_sg.astype(BF16), aq


def _s5_kernel(u_ref, toep_ref, inj_ref, out_ref, aq_ref, y_ref, inj_scr, xs_scr, *, n_chunks):
    n_state = aq_ref.shape[2]
    u = u_ref[0]
    inj_scr[...] = jnp.dot(u, inj_ref[0], preferred_element_type=F32)
    aq_re = jnp.broadcast_to(aq_ref[0, 0:1, :], (8, n_state))
    aq_im = jnp.broadcast_to(aq_ref[0, 1:2, :], (8, n_state))
    sub = lax.broadcasted_iota(jnp.int32, (8, n_state), 0)

    def block(blk, carry):
        re, im = carry
        start = pl.multiple_of(blk * 8, 8)
        inj = inj_scr[pl.ds(start, 8), :]
        xs_re = jnp.zeros((8, n_state), F32)
        xs_im = jnp.zeros((8, n_state), F32)
        for j in range(8):
            xs_re = jnp.where(sub == j, re, xs_re)
            xs_im = jnp.where(sub == j, im, xs_im)
            in_re = jnp.broadcast_to(inj[j:j + 1, :n_state], (8, n_state))
            in_im = jnp.broadcast_to(inj[j:j + 1, n_state:], (8, n_state))
            re, im = aq_re * re - aq_im * im + in_re, aq_re * im + aq_im * re + in_im
        xs_scr[pl.ds(start, 8), :n_state] = xs_re
        xs_scr[pl.ds(start, 8), n_state:] = xs_im
        return re, im

    zero = jnp.zeros((8, n_state), F32)
    lax.fori_loop(0, n_chunks // 8, block, (zero, zero))
    y = jnp.dot(u, toep_ref[0], preferred_element_type=F32)
    y_ref[0] = y + jnp.dot(xs_scr[...].astype(BF16), out_ref[0], preferred_element_type=F32)


def _s5_scan(u_sg, tables, batch, n_chunks):
    toep, inj, out, aq = tables
    n_sg, _, width = u_sg.shape
    n_state2 = inj.shape[2]

    def per_sg(shape):
        return pl.BlockSpec((1,) + shape, lambda g, b: (g, 0, 0))

    return pl.pallas_call(
        functools.partial(_s5_kernel, n_chunks=n_chunks),
        grid=(n_sg, batch),
        in_specs=[pl.BlockSpec((1, n_chunks, width), lambda g, b: (g, b, 0)),
                  per_sg((width, width)), per_sg((width, n_state2)), per_sg((n_state2, width)),
                  per_sg((2, n_state2 // 2))],
        out_specs=pl.BlockSpec((1, n_chunks, width), lambda g, b: (g, b, 0)),
        out_shape=jax.ShapeDtypeStruct(u_sg.shape, F32),
        scratch_shapes=[pltpu.VMEM((n_chunks, n_state2), F32)] * 2,
        compiler_params=_params(2),
    )(u_sg, toep, inj, out, aq)


def _even_post_kernel(h_ref, yc_ref, ysg_ref, u_ref, perm_ref, d_ref, wglu_ref, bglu_ref, woc_ref,
                      wos_ref, g_ref, b_ref, rwh_ref, rwl_ref, rb_ref, tri_ref, *out_and_scratch):
    n_sg = ysg_ref.shape[0]
    by_pos = jnp.concatenate(
        [jnp.concatenate([ysg_ref[sg, :, s * LANES:(s + 1) * LANES] for sg in range(n_sg)], axis=1)
         for s in range(SSM_CHUNK)], axis=0)
    hi, lo = _split_bf16(by_pos)
    y_scan = (jnp.dot(perm_ref[...], hi, preferred_element_type=F32)
              + jnp.dot(perm_ref[...], lo, preferred_element_type=F32))
    y = y_scan + d_ref[...] * u_ref[...]
    z = jax.nn.gelu(y)
    glu = jnp.dot(z.astype(BF16), wglu_ref[...], preferred_element_type=F32) + bglu_ref[...]
    z = z * _sigmoid(glu)
    mix = (jnp.dot(yc_ref[...].astype(BF16), woc_ref[...], preferred_element_type=F32)
           + jnp.dot(z.astype(BF16), wos_ref[...], preferred_element_type=F32))
    _norm_and_route(DN_ALPHA * h_ref[...] + mix, g_ref, b_ref, (rwh_ref, rwl_ref, rb_ref, tri_ref),
                    *out_and_scratch)


def _even_post(h, y_conv, y_sg, u, consts):
    t = h.shape[0]
    n_sg = y_sg.shape[0]
    out_specs, out_shape = _post_outputs(t)
    consts = [_chunk_perm(ROW_TILE).T, *consts]
    return pl.pallas_call(
        _even_post_kernel,
        grid=(t // ROW_TILE,),
        in_specs=[_rows(D_MODEL), _rows(CONV_CH),
                  pl.BlockSpec((n_sg, ROW_TILE // SSM_CHUNK, SSM_CHUNK * LANES), lambda i: (0, i, 0)),
                  _rows(SSM_WIDTH)] + [_full(a.shape) for a in consts],
        out_specs=out_specs, out_shape=out_shape,
        scratch_shapes=[pltpu.VMEM((8, LANES), F32)],
        compiler_params=_params(),
    )(h, y_conv, y_sg, u, *consts)


def _qkv_rope_kernel(x_ref, w_ref, cos_ref, sin_ref, p4_ref, p16_ref, o1_ref, o4_ref, o16_ref):
    qkv = jnp.dot(x_ref[...].astype(BF16), w_ref[...], preferred_element_type=F32)
    rows = qkv.shape[0]
    cos = jnp.concatenate([cos_ref[...]] * (D_MODEL // LANES), axis=1)
    sin = jnp.concatenate([sin_ref[...]] * (D_MODEL // LANES), axis=1)
    lane = lax.broadcasted_iota(jnp.int32, (rows, D_MODEL), 1)
    low_half = (lane % HEAD_DIM) < (HEAD_DIM // 2)

    def rope(xs):
        up = pltpu.roll(xs, D_MODEL - HEAD_DIM // 2, 1)
        down = pltpu.roll(xs, HEAD_DIM // 2, 1)
        return xs * cos + jnp.where(low_half, up, down) * sin

    parts = (rope(qkv[:, :D_MODEL]) * (HEAD_DIM ** -0.5), rope(qkv[:, D_MODEL:2 * D_MODEL]),
             qkv[:, 2 * D_MODEL:])
    for which, part in enumerate(parts):
        cols = slice(which * D_MODEL, (which + 1) * D_MODEL)
        part = part.astype(BF16)
        o1_ref[:, cols] = part
        for dil, perm_ref, ref in ((DILATIONS[1], p4_ref, o4_ref), (DILATIONS[2], p16_ref, o16_ref)):
            split = jnp.dot(perm_ref[...], part, preferred_element_type=F32).astype(BF16)
            n = rows // dil
            for r in range(dil):
                ref[0, r, :, cols] = split[r * n:(r + 1) * n, :]


def _rope_tables(seq_len):
    half = HEAD_DIM // 2
    inv = ROPE_THETA ** (-jnp.arange(half, dtype=F32) / half)
    ang = jnp.arange(seq_len, dtype=F32)[:, None] * inv[None, :]
    cos = jnp.tile(jnp.cos(ang), (1, LANES // half))
    sin = jnp.sin(ang)
    sin = jnp.tile(jnp.concatenate([-sin, sin], axis=1), (1, LANES // HEAD_DIM))
    return cos, sin


def _residue_spec(dil, rows, width, tiles_per_seq):
    return pl.BlockSpec((1, dil, rows // dil, width),
                        lambda i: (i // tiles_per_seq, 0, i % tiles_per_seq, 0))


def _qkv_rope(h, w_qkv_bf, cos, sin, batch, seq_len):
    t = h.shape[0]
    tps = seq_len // QKV_TILE
    d4, d16 = DILATIONS[1], DILATIONS[2]
    width = 3 * D_MODEL
    perms = []
    for dil in (d4, d16):
        j = jnp.arange(QKV_TILE, dtype=jnp.int32)
        src = (j % (QKV_TILE // dil)) * dil + j // (QKV_TILE // dil)
        perms.append((src[:, None] == j[None, :]).astype(BF16))
    return pl.pallas_call(
        _qkv_rope_kernel,
        grid=(t // QKV_TILE,),
        in_specs=[_rows(D_MODEL, QKV_TILE), _full(w_qkv_bf.shape),
                  pl.BlockSpec((QKV_TILE, LANES), lambda i: (i % tps, 0)),
                  pl.BlockSpec((QKV_TILE, LANES), lambda i: (i % tps, 0)),
                  _full((QKV_TILE, QKV_TILE)), _full((QKV_TILE, QKV_TILE))],
        out_specs=[_rows(width, QKV_TILE), _residue_spec(d4, QKV_TILE, width, tps),
                   _residue_spec(d16, QKV_TILE, width, tps)],
        out_shape=[jax.ShapeDtypeStruct((t, width), BF16),
                   jax.ShapeDtypeStruct((batch, d4, seq_len // d4, width), BF16),
                   jax.ShapeDtypeStruct((batch, d16, seq_len // d16, width), BF16)],
        compiler_params=_params(),
    )(h, w_qkv_bf, cos, sin, *perms)


def _attn_kernel(q_ref, kp_ref, kc_ref, vp_ref, vc_ref, o_ref, lse_ref):
    blk = ATT_BLOCK
    qi = lax.broadcasted_iota(jnp.int32, (blk, 2 * blk), 0)
    kj = lax.broadcasted_iota(jnp.int32, (blk, 2 * blk), 1)
    dist = blk + qi - kj
    k_min = jnp.where(pl.program_id(1) == 0, blk, 0)
    bias = jnp.where(dist >= 0, jnp.where(dist <= blk, jnp.where(kj >= k_min, 0.0, NEG_BIG),
                                          NEG_BIG), NEG_BIG)
    lane = lax.broadcasted_iota(jnp.int32, (blk, LANES), 1)
    low = lane < HEAD_DIM
    lse_tile = jnp.zeros((blk, LANES), F32)
    for hp in range(N_HEADS // 2):
        cols = slice(hp * LANES, (hp + 1) * LANES)
        q2 = q_ref[0, :, cols]
        kk = jnp.concatenate([kp_ref[0, :, cols], kc_ref[0, :, cols]], axis=0)
        vv = jnp.concatenate([vp_ref[0, :, cols], vc_ref[0, :, cols]], axis=0)
        halves = []
        for hh in range(2):
            head = 2 * hp + hh
            qm = jnp.where(low if hh == 0 else jnp.logical_not(low), q2, jnp.zeros_like(q2))
            s = lax.dot_general(qm, kk, (((1,), (1,)), ((), ())),
                                preferred_element_type=F32) + bias
            m = jnp.max(s, axis=-1, keepdims=True)
            p = jnp.exp(s - m)
            den = jnp.sum(p, axis=-1, keepdims=True)
            pv = jnp.dot(p.astype(BF16), vv, preferred_element_type=F32)
            halves.append(pv / den)
            lse_tile = jnp.where(lane == head, m + jnp.log(den), lse_tile)
        o_ref[0, :, cols] = jnp.where(low, halves[0], halves[1]).astype(BF16)
    lse_ref[0] = lse_tile


def _attn_pattern(qkv_sub):
    n_sub, sub_len, _ = qkv_sub.shape

    def cur(which):
        return pl.BlockSpec((1, ATT_BLOCK, D_MODEL), lambda s, n: (s, n, which))

    def prev(which):
        return pl.BlockSpec((1, ATT_BLOCK, D_MODEL), lambda s, n: (s, jnp.maximum(n - 1, 0), which))

    return pl.pallas_call(
        _attn_kernel,
        grid=(n_sub, sub_len // ATT_BLOCK),
        in_specs=[cur(0), prev(1), cur(1), prev(2), cur(2)],
        out_specs=[pl.BlockSpec((1, ATT_BLOCK, D_MODEL), lambda s, n: (s, n, 0)),
                   pl.BlockSpec((1, ATT_BLOCK, LANES), lambda s, n: (s, n, 0))],
        out_shape=[jax.ShapeDtypeStruct((n_sub, sub_len, D_MODEL), BF16),
                   jax.ShapeDtypeStruct((n_sub, sub_len, LANES), F32)],
        compiler_params=_params(2),
    )(*([qkv_sub] * 5))


def _odd_post_kernel(h_ref, o1_ref, l1_ref, o4_ref, l4_ref, o16_ref, l16_ref, expand_ref, wo_ref,
                     g_ref, b_ref, rwh_ref, rwl_ref, rb_ref, tri_ref, hn_ref, hnb_ref, idx_ref,
                     gate_ref, cnt_ref, os4_ref, ls4_ref, os16_ref, ls16_ref, seen_ref):
    rows = h_ref.shape[0]
    n_col = D_MODEL // LANES
    for dil, o_ref, l_ref, os_ref, ls_ref in ((DILATIONS[1], o4_ref, l4_ref, os4_ref, ls4_ref),
                                              (DILATIONS[2], o16_ref, l16_ref, os16_ref, ls16_ref)):
        for r in range(dil):
            sel = pl.ds(r, rows // dil, stride=dil)
            ls_ref[sel, :] = l_ref[0, r]
            for c in range(n_col):
                os_ref[c, sel, :] = o_ref[0, r, :, c * LANES:(c + 1) * LANES].astype(F32)
    lses = (l1_ref[...], ls4_ref[...], ls16_ref[...])
    outs = (o1_ref[...].astype(F32),
            jnp.concatenate([os4_ref[c] for c in range(n_col)], axis=1),
            jnp.concatenate([os16_ref[c] for c in range(n_col)], axis=1))
    mx = jnp.maximum(jnp.maximum(lses[0], lses[1]), lses[2])
    es = [jnp.exp(l - mx) for l in lses]
    den = es[0] + es[1] + es[2]
    o = jnp.zeros((rows, D_MODEL), F32)
    for e, out in zip(es, outs):
        hi, lo = _split_bf16(e / den)
        w = (jnp.dot(hi, expand_ref[...], preferred_element_type=F32)
             + jnp.dot(lo, expand_ref[...], preferred_element_type=F32))
        o = o + w * out
    mix = jnp.dot(o.astype(BF16), wo_ref[...], preferred_element_type=F32)
    _norm_and_route(DN_ALPHA * h_ref[...] + mix, g_ref, b_ref, (rwh_ref, rwl_ref, rb_ref, tri_ref),
                    hn_ref, hnb_ref, idx_ref, gate_ref, cnt_ref, seen_ref)


def _odd_post(h, pattern_outs, consts, seq_len):
    t = h.shape[0]
    tps = seq_len // ROW_TILE
    (o1, l1), (o4, l4), (o16, l16) = pattern_outs
    d4, d16 = DILATIONS[1], DILATIONS[2]
    out_specs, out_shape = _post_outputs(t)
    return pl.pallas_call(
        _odd_post_kernel,
        grid=(t // ROW_TILE,),
        in_specs=[_rows(D_MODEL), _rows(D_MODEL), _rows(LANES),
                  _residue_spec(d4, ROW_TILE, D_MODEL, tps), _residue_spec(d4, ROW_TILE, LANES, tps),
                  _residue_spec(d16, ROW_TILE, D_MODEL, tps), _residue_spec(d16, ROW_TILE, LANES, tps)]
        + [_full(a.shape) for a in consts],
        out_specs=out_specs, out_shape=out_shape,
        scratch_shapes=[pltpu.VMEM((D_MODEL // LANES, ROW_TILE, LANES), F32),
                        pltpu.VMEM((ROW_TILE, LANES), F32)] * 2 + [pltpu.VMEM((8, LANES), F32)],
        compiler_params=_params(),
    )(h, o1, l1, o4, l4, o16, l16, *consts)


def _attention_layer(h, w_qkv, w_o, cos, sin, consts, batch, seq_len):
    t = h.shape[0]
    qkv1, qkv4, qkv16 = _qkv_rope(h, w_qkv.astype(BF16), cos, sin, batch, seq_len)
    outs = []
    for dil, qkv in zip(DILATIONS, (qkv1, qkv4, qkv16)):
        sub_len = seq_len // dil
        o, lse = _attn_pattern(qkv.reshape(batch * dil, sub_len, 3 * D_MODEL))
        if dil == 1:
            outs.append((o.reshape(t, D_MODEL), lse.reshape(t, LANES)))
        else:
            outs.append((o.reshape(batch, dil, sub_len, D_MODEL),
                         lse.reshape(batch, dil, sub_len, LANES)))
    head_of_lane = jnp.arange(D_MODEL, dtype=jnp.int32) // HEAD_DIM
    expand = (jnp.arange(LANES, dtype=jnp.int32)[:, None] == head_of_lane[None, :]).astype(BF16)
    return _odd_post(h, outs, [expand, w_o.astype(BF16), *consts], seq_len)


def _moe_kernel(be_ref, nused_ref, x_ref, wgu_ref, bgu_ref, wd_ref, bd_ref, y_ref,
                wgu_bf_ref, wd_bf_ref):
    i = pl.program_id(0)
    new_expert = jnp.logical_or(i == 0, be_ref[i] != be_ref[jnp.maximum(i - 1, 0)])

    @pl.when(new_expert)
    def _():
        wgu_bf_ref[...] = wgu_ref[0, 0].astype(BF16)
        wd_bf_ref[...] = wd_ref[0, 0].astype(BF16)

    @pl.when(i < nused_ref[0])
    def _():
        hid = jnp.dot(x_ref[...], wgu_bf_ref[...], preferred_element_type=F32) + bgu_ref[0]
        gate = jnp.minimum(hid[:, :D_MODEL], SWIGLU_LIMIT)
        lin = jnp.clip(hid[:, D_MODEL:], -SWIGLU_LIMIT, SWIGLU_LIMIT)
        act = (lin + 1.0) * (gate * _sigmoid(SWIGLU_ALPHA * gate))
        y = jnp.dot(act.astype(BF16), wd_bf_ref[...], preferred_element_type=F32) + bd_ref[0]
        y_ref[...] = y.astype(BF16)

    @pl.when(i >= nused_ref[0])
    def _():
        y_ref[...] = jnp.zeros_like(y_ref)


def _moe_experts(xs, block_e, n_used, layer, w_gu, b_gu, w_down, b_down):
    n_rows = xs.shape[0]
    grid_spec = pltpu.PrefetchScalarGridSpec(
        num_scalar_prefetch=2,
        grid=(n_rows // MOE_ROWS,),
        in_specs=[pl.BlockSpec((MOE_ROWS, D_MODEL), lambda i, be, nu: (i, 0)),
                  pl.BlockSpec((1, 1, D_MODEL, 2 * D_MODEL), lambda i, be, nu: (layer, be[i], 0, 0)),
                  pl.BlockSpec((1, 1, 2 * D_MODEL), lambda i, be, nu: (be[i], 0, 0)),
                  pl.BlockSpec((1, 1, D_MODEL, D_MODEL), lambda i, be, nu: (layer, be[i], 0, 0)),
                  pl.BlockSpec((1, 1, D_MODEL), lambda i, be, nu: (be[i], 0, 0))],
        out_specs=pl.BlockSpec((MOE_ROWS, D_MODEL), lambda i, be, nu: (i, 0)),
        scratch_shapes=[pltpu.VMEM((D_MODEL, 2 * D_MODEL), BF16), pltpu.VMEM((D_MODEL, D_MODEL), BF16)],
    )
    return pl.pallas_call(
        _moe_kernel,
        grid_spec=grid_spec,
        out_shape=jax.ShapeDtypeStruct((n_rows, D_MODEL), BF16),
        compiler_params=_params(),
    )(block_e, n_used, xs, w_gu, b_gu[:, None, :], w_down, b_down[:, None, :])


def _routing_tables(idx_tile, counts, t):
    n_assign = t * TOP_K
    top_idx = idx_tile[:, :TOP_K]
    rank = idx_tile[:, TOP_K:2 * TOP_K]
    counts = counts[0, :N_EXPERTS].astype(jnp.int32)
    padded = (counts + MOE_ROWS - 1) // MOE_ROWS * MOE_ROWS
    pad_end = jnp.cumsum(padded)
    pad_start = pad_end - padded
    experts = jnp.arange(N_EXPERTS, dtype=jnp.int32)
    start_of = jnp.sum(jnp.where(top_idx[..., None] == experts, pad_start, 0), axis=-1)
    dest = (start_of + rank).T.reshape(-1)
    n_blocks = n_assign // MOE_ROWS + N_EXPERTS
    flat_tok = jnp.arange(n_assign, dtype=jnp.int32) % t
    row_tok = (jnp.arange(n_blocks * MOE_ROWS, dtype=jnp.int32) % t).at[dest].set(
        flat_tok, unique_indices=True, mode="promise_in_bounds")
    block_start = jnp.arange(n_blocks, dtype=jnp.int32) * MOE_ROWS
    block_e = jnp.minimum(jnp.sum((block_start[:, None] >= pad_end[None, :]).astype(jnp.int32), axis=1),
                          N_EXPERTS - 1)
    n_used = (pad_end[-1:] // MOE_ROWS).astype(jnp.int32)
    return row_tok, dest, block_e, n_used


def _ffn_ln_kernel(h_ref, y4_ref, gate_ref, g_ref, b_ref, o_ref):
    gates = gate_ref[...]
    acc = DN_ALPHA * h_ref[...]
    for k in range(TOP_K):
        acc = acc + gates[:, k:k + 1] * y4_ref[k].astype(F32)
    o_ref[...] = _layer_norm(acc, g_ref[...], b_ref[...])


def _ffn_ln(h, y4, gates, g, b):
    t = h.shape[0]
    return pl.pallas_call(
        _ffn_ln_kernel,
        grid=(t // ROW_TILE,),
        in_specs=[_rows(D_MODEL), pl.BlockSpec((TOP_K, ROW_TILE, D_MODEL), lambda i: (0, i, 0)),
                  _rows(LANES), _full(g.shape), _full(b.shape)],
        out_specs=_rows(D_MODEL),
        out_shape=jax.ShapeDtypeStruct((t, D_MODEL), F32),
        compiler_params=_params(),
    )(h, y4, gates, g, b)


def _moe_layer(hn, hn_bf, idx_tile, gate_tile, counts, layer, w_gu, b_gu, w_down, b_down, ln_g, ln_b):
    t = hn.shape[0]
    row_tok, dest, block_e, n_used = _routing_tables(idx_tile, counts, t)
    xs = hn_bf.at[row_tok].get(mode="promise_in_bounds")
    ys = _moe_experts(xs, block_e, n_used, layer, w_gu, b_gu, w_down, b_down)
    y4 = ys.at[dest].get(mode="promise_in_bounds").reshape(TOP_K, t, D_MODEL)
    return _ffn_ln(hn, y4, gate_tile, ln_g[None, :], ln_b[None, :])


def _router_consts(router_w, router_b):
    w = jnp.zeros((D_MODEL, LANES), F32).at[:, :N_EXPERTS].set(router_w)
    w_hi, w_lo = _split_bf16(w)
    b = jnp.full((1, LANES), NEG_BIG, F32).at[0, :N_EXPERTS].set(router_b)
    r = jnp.arange(ROW_TILE, dtype=jnp.int32)
    tri = (r[None, :] < r[:, None]).astype(BF16)
    return w_hi, w_lo, b, tri


def kernel(x, hy_w_in, conv_w, ssm_a_re, ssm_a_im, ssm_log_dt, ssm_b_re, ssm_b_im, ssm_c_re,
           ssm_c_im, ssm_d, ssm_w_glu, ssm_b_glu, hy_w_out, att_w_qkv, att_w_o, ln_mix_g,
           ln_mix_b, ln_ffn_g, ln_ffn_b, router_w, router_b, expert_w_gu, expert_b_gu,
           expert_w_down, expert_b_down):
    batch, seq_len, _ = x.shape
    t = batch * seq_len
    h = x.reshape(t, D_MODEL)
    cos, sin = _rope_tables(seq_len)
    for layer in range(DEPTH):
        i = layer // 2
        consts = [ln_mix_g[layer][None, :], ln_mix_b[layer][None, :],
                  *_router_consts(router_w[layer], router_b[layer])]
        if layer % 2 == 0:
            y_conv, u, u_sg = _inproj_conv(h, hy_w_in[i].astype(BF16), conv_w[i], seq_len)
            tables = _s5_tables(ssm_a_re[i], ssm_a_im[i], ssm_log_dt[i], ssm_b_re[i], ssm_b_im[i],
                                ssm_c_re[i], ssm_c_im[i])
            y_sg = _s5_scan(u_sg, tables, batch, seq_len // SSM_CHUNK)
            w_out = hy_w_out[i].astype(BF16)
            hn, hn_bf, idx_tile, gate_tile, counts = _even_post(
                h, y_conv, y_sg, u,
                [ssm_d[i].reshape(1, SSM_WIDTH), ssm_w_glu[i].astype(BF16), ssm_b_glu[i][None, :],
                 w_out[:CONV_CH], w_out[CONV_CH:], *consts])
        else:
            hn, hn_bf, idx_tile, gate_tile, counts = _attention_layer(
                h, att_w_qkv[i], att_w_o[i], cos, sin, consts, batch, seq_len)
        h = _moe_layer(hn, hn_bf, idx_tile, gate_tile, counts, layer, expert_w_gu, expert_b_gu[layer],
                       expert_w_down, expert_b_down[layer], ln_ffn_g[layer], ln_ffn_b[layer])
    return h.reshape(batch, seq_len, D_MODEL)
```

```python
import functools

import jax
import jax.numpy as jnp
from jax import lax
from jax.experimental import pallas as pl
from jax.experimental.pallas import tpu as pltpu

F32 = jnp.float32
BF16 = jnp.bfloat16

D_MODEL = 1024
DEPTH = 4
CONV_CH = 512
SSM_WIDTH = 512
SSM_GROUP = 16
SSM_GROUPS = 32
SSM_STATE = 64
N_HEADS = 16
HEAD_DIM = 64
ROPE_THETA = 10000.0
DILATIONS = (1, 4, 16)
ATT_BLOCK = 128
ATT_BLOCKS_PER_STEP = 2
N_EXPERTS = 32
TOP_K = 4
SWIGLU_LIMIT = 7.0
SWIGLU_ALPHA = 1.702
DN_ALPHA = (2 * DEPTH) ** 0.25
LN_EPS = 1e-5

LANES = 128
ROW_TILE = 512
QKV_TILE = 256
SSM_CHUNK = 8
MOE_ROWS = 512
VMEM_LIMIT = 56 * 1024 * 1024
NEG_BIG = -1e30


def _params(n_axes=1):
    return pltpu.CompilerParams(dimension_semantics=("arbitrary",) * n_axes,
                                vmem_limit_bytes=VMEM_LIMIT)


def _full(shape):
    return pl.BlockSpec(shape, lambda *_: (0,) * len(shape))


def _rows(width, tile=ROW_TILE):
    return pl.BlockSpec((tile, width), lambda i: (i, 0))


def _layer_norm(x, g, b):
    mu = jnp.mean(x, axis=-1, keepdims=True)
    xc = x - mu
    var = jnp.mean(xc * xc, axis=-1, keepdims=True)
    return xc * lax.rsqrt(var + LN_EPS) * g + b


def _sigmoid(x):
    return 1.0 / (1.0 + jnp.exp(-x))


def _split_bf16(x):
    hi = x.astype(BF16)
    return hi, (x - hi.astype(F32)).astype(BF16)


def _route(hn, rw_hi_ref, rw_lo_ref, rb_ref, tri_ref, idx_ref, gate_ref, cnt_ref, seen_ref):
    @pl.when(pl.program_id(0) == 0)
    def _():
        seen_ref[...] = jnp.zeros_like(seen_ref)

    hi, lo = _split_bf16(hn)
    logits = (jnp.dot(hi, rw_hi_ref[...], preferred_element_type=F32)
              + jnp.dot(hi, rw_lo_ref[...], preferred_element_type=F32)
              + jnp.dot(lo, rw_hi_ref[...], preferred_element_type=F32)
              + rb_ref[...])
    lane = lax.broadcasted_iota(jnp.int32, logits.shape, 1)
    idx_tile = jnp.zeros(logits.shape, jnp.int32)
    val_tile = jnp.zeros(logits.shape, F32)
    top0 = None
    den = None
    firsts = []
    for k in range(TOP_K):
        mx = jnp.max(logits, axis=-1, keepdims=True)
        first = jnp.min(jnp.where(logits == mx, lane, LANES), axis=-1, keepdims=True)
        if k == 0:
            top0 = mx
        e = jnp.exp(mx - top0)
        den = e if k == 0 else den + e
        idx_tile = jnp.where(lane == k, first, idx_tile)
        val_tile = jnp.where(lane == k, e, val_tile)
        logits = jnp.where(lane == first, -jnp.inf, logits)
        firsts.append(first)
    gate_ref[...] = val_tile / den
    picked = jnp.where(logits == -jnp.inf, 1.0, 0.0)
    seen = seen_ref[0:1, :]
    before = jnp.dot(tri_ref[...], picked.astype(BF16), preferred_element_type=F32) + seen
    for k, first in enumerate(firsts):
        rank = jnp.sum(jnp.where(lane == first, before, 0.0), axis=-1, keepdims=True)
        idx_tile = jnp.where(lane == TOP_K + k, rank.astype(jnp.int32), idx_tile)
    idx_ref[...] = idx_tile
    seen = seen + jnp.sum(picked, axis=0, keepdims=True)
    seen_ref[...] = jnp.broadcast_to(seen, seen_ref.shape)
    cnt_ref[...] = jnp.broadcast_to(seen, cnt_ref.shape)


def _post_outputs(t):
    specs = [_rows(D_MODEL), _rows(D_MODEL), _rows(LANES), _rows(LANES), _full((8, LANES))]
    shapes = [jax.ShapeDtypeStruct((t, D_MODEL), F32), jax.ShapeDtypeStruct((t, D_MODEL), BF16),
              jax.ShapeDtypeStruct((t, LANES), jnp.int32), jax.ShapeDtypeStruct((t, LANES), F32),
              jax.ShapeDtypeStruct((8, LANES), F32)]
    return specs, shapes


def _norm_and_route(pre, g_ref, b_ref, route_refs, hn_ref, hnb_ref, idx_ref, gate_ref, cnt_ref,
                    seen_ref):
    hn = _layer_norm(pre, g_ref[...], b_ref[...])
    hn_ref[...] = hn
    hnb_ref[...] = hn.astype(BF16)
    _route(hn, *route_refs, idx_ref, gate_ref, cnt_ref, seen_ref)


def _chunk_perm(rows):
    j = jnp.arange(rows, dtype=jnp.int32)
    n = rows // SSM_CHUNK
    src = (j % n) * SSM_CHUNK + j // n
    return (src[:, None] == j[None, :]).astype(BF16)


def _inproj_conv_kernel(x_ref, w_ref, cw_ref, perm_ref, yconv_ref, u_ref, usg_ref, carry_ref, *,
                        tiles_per_seq):
    @pl.when(pl.program_id(0) % tiles_per_seq == 0)
    def _():
        carry_ref[...] = jnp.zeros_like(carry_ref)

    proj = jnp.dot(x_ref[...].astype(BF16), w_ref[...], preferred_element_type=F32)
    gate_b = proj[:, :CONV_CH]
    gate_c = proj[:, CONV_CH:2 * CONV_CH]
    hid = proj[:, 2 * CONV_CH:3 * CONV_CH]
    v = gate_c * hid
    rows = v.shape[0]
    row = lax.broadcasted_iota(jnp.int32, v.shape, 0)
    prev1 = carry_ref[7:8, :]
    prev2 = carry_ref[6:7, :]
    vm1 = jnp.where(row == 0, prev1, pltpu.roll(v, 1, 0))
    vm2 = jnp.where(row == 0, prev2, jnp.where(row == 1, prev1, pltpu.roll(v, 2, 0)))
    conv = cw_ref[0:1, :] * vm2 + cw_ref[1:2, :] * vm1 + cw_ref[2:3, :] * v
    yconv_ref[...] = gate_b * conv
    u = proj[:, 3 * CONV_CH:]
    u_ref[...] = u
    carry_ref[...] = v[rows - 8:, :]
    by_pos = jnp.dot(perm_ref[...], u.astype(BF16), preferred_element_type=F32).astype(BF16)
    n = rows // SSM_CHUNK
    for sg in range(SSM_WIDTH // LANES):
        for s in range(SSM_CHUNK):
            usg_ref[sg, :, s * LANES:(s + 1) * LANES] = by_pos[s * n:(s + 1) * n,
                                                               sg * LANES:(sg + 1) * LANES]


def _inproj_conv(h, w_in_bf, conv_w, seq_len):
    t = h.shape[0]
    n_sg = SSM_WIDTH // LANES
    perm = _chunk_perm(ROW_TILE)
    return pl.pallas_call(
        functools.partial(_inproj_conv_kernel, tiles_per_seq=seq_len // ROW_TILE),
        grid=(t // ROW_TILE,),
        in_specs=[_rows(D_MODEL), _full(w_in_bf.shape), _full(conv_w.shape), _full(perm.shape)],
        out_specs=[_rows(CONV_CH), _rows(SSM_WIDTH),
                   pl.BlockSpec((n_sg, ROW_TILE // SSM_CHUNK, SSM_CHUNK * LANES), lambda i: (0, i, 0))],
        out_shape=[jax.ShapeDtypeStruct((t, CONV_CH), F32),
                   jax.ShapeDtypeStruct((t, SSM_WIDTH), F32),
                   jax.ShapeDtypeStruct((n_sg, t // SSM_CHUNK, SSM_CHUNK * LANES), BF16)],
        scratch_shapes=[pltpu.VMEM((8, CONV_CH), F32)],
        compiler_params=_params(),
    )(h, w_in_bf, conv_w, perm)


def _s5_tables(a_re, a_im, log_dt, b_re, b_im, c_re, c_im):
    q = SSM_CHUNK
    per_sg = LANES // SSM_GROUP
    n_sg = SSM_GROUPS // per_sg
    lam_re = jnp.minimum(a_re, -1e-4)
    lam_im = a_im
    dt = jnp.exp(log_dt)[:, None]
    mag = jnp.exp(lam_re * dt)
    ab_re = mag * jnp.cos(lam_im * dt)
    ab_im = mag * jnp.sin(lam_im * dt)
    nr, ni = ab_re - 1.0, ab_im
    den = lam_re * lam_re + lam_im * lam_im
    coef_re = ((nr * lam_re + ni * lam_im) / den)[..., None]
    coef_im = ((ni * lam_re - nr * lam_im) / den)[..., None]
    bb_re = coef_re * b_re - coef_im * b_im
    bb_im = coef_re * b_im + coef_im * b_re
    j = jnp.arange(q + 1, dtype=F32)
    pmag = jnp.exp((lam_re * dt)[..., None] * j)
    pang = (lam_im * dt)[..., None] * j
    pw_re = pmag * jnp.cos(pang)
    pw_im = pmag * jnp.sin(pang)
    ca_re = c_re[..., None] * pw_re[:, None] - c_im[..., None] * pw_im[:, None]
    ca_im = c_re[..., None] * pw_im[:, None] + c_im[..., None] * pw_re[:, None]
    kern = (jnp.einsum("gapj,gph->gjah", ca_re, bb_re, precision="highest")
            - jnp.einsum("gapj,gph->gjah", ca_im, bb_im, precision="highest"))[:, :q]
    lag = jnp.arange(q)[None, :] - jnp.arange(q)[:, None]
    toep = kern[:, jnp.clip(lag, 0, q - 1)]
    toep = jnp.where((lag >= 0)[None, :, :, None, None], toep, 0.0).transpose(0, 1, 4, 2, 3)
    rev = q - 1 - jnp.arange(q)
    pr = pw_re[:, :, rev]
    pi = pw_im[:, :, rev]
    inj_re = pr[..., None] * bb_re[:, :, None] - pi[..., None] * bb_im[:, :, None]
    inj_im = pr[..., None] * bb_im[:, :, None] + pi[..., None] * bb_re[:, :, None]
    inj_re = inj_re.transpose(0, 2, 3, 1)
    inj_im = inj_im.transpose(0, 2, 3, 1)
    out_re = ca_re[..., 1:].transpose(0, 2, 3, 1)
    out_im = (-ca_im[..., 1:]).transpose(0, 2, 3, 1)
    eye = jnp.eye(per_sg, dtype=F32)
    width = q * LANES
    n_state = per_sg * SSM_STATE

    def sg(a):
        return a.reshape((n_sg, per_sg) + a.shape[1:])

    toep_sg = jnp.einsum("Scshtk,cd->Sschtdk", sg(toep), eye).reshape(n_sg, width, width)
    inj_sg = jnp.concatenate(
        [jnp.einsum("Scshp,cd->Sschdp", sg(a), eye).reshape(n_sg, width, n_state)
         for a in (inj_re, inj_im)], axis=2)
    out_sg = jnp.concatenate(
        [jnp.einsum("Scpth,cd->Scptdh", sg(a), eye).reshape(n_sg, n_state, width)
         for a in (out_re, out_im)], axis=1)
    aq = jnp.stack([pw_re[..., q].reshape(n_sg, n_state), pw_im[..., q].reshape(n_sg, n_state)], axis=1)
    return toep_sg.astype(BF16), inj_sg.astype(BF16), out_sg.astype(BF16), aq


def _s5_kernel(u_ref, toep_ref, inj_ref, out_ref, aq_ref, y_ref, inj_scr, xs_scr, *, n_chunks):
    n_state = aq_ref.shape[2]
    u = u_ref[0]
    inj_scr[...] = jnp.dot(u, inj_ref[0], preferred_element_type=F32)
    aq_re = jnp.broadcast_to(aq_ref[0, 0:1, :], (8, n_state))
    aq_im = jnp.broadcast_to(aq_ref[0, 1:2, :], (8, n_state))
    sub = lax.broadcasted_iota(jnp.int32, (8, n_state), 0)

    def block(blk, carry):
        re, im = carry
        start = pl.multiple_of(blk * 8, 8)
        inj = inj_scr[pl.ds(start, 8), :]
        xs_re = jnp.zeros((8, n_state), F32)
        xs_im = jnp.zeros((8, n_state), F32)
        for j in range(8):
            xs_re = jnp.where(sub == j, re, xs_re)
            xs_im = jnp.where(sub == j, im, xs_im)
            in_re = jnp.broadcast_to(inj[j:j + 1, :n_state], (8, n_state))
            in_im = jnp.broadcast_to(inj[j:j + 1, n_state:], (8, n_state))
            re, im = aq_re * re - aq_im * im + in_re, aq_re * im + aq_im * re + in_im
        xs_scr[pl.ds(start, 8), :n_state] = xs_re
        xs_scr[pl.ds(start, 8), n_state:] = xs_im
        return re, im

    zero = jnp.zeros((8, n_state), F32)
    lax.fori_loop(0, n_chunks // 8, block, (zero, zero))
    y = jnp.dot(u, toep_ref[0], preferred_element_type=F32)
    y_ref[0] = y + jnp.dot(xs_scr[...].astype(BF16), out_ref[0], preferred_element_type=F32)


def _s5_scan(u_sg, tables, batch, n_chunks):
    toep, inj, out, aq = tables
    n_sg, _, width = u_sg.shape
    n_state2 = inj.shape[2]

    def per_sg(shape):
        return pl.BlockSpec((1,) + shape, lambda g, b: (g, 0, 0))

    return pl.pallas_call(
        functools.partial(_s5_kernel, n_chunks=n_chunks),
        grid=(n_sg, batch),
        in_specs=[pl.BlockSpec((1, n_chunks, width), lambda g, b: (g, b, 0)),
                  per_sg((width, width)), per_sg((width, n_state2)), per_sg((n_state2, width)),
                  per_sg((2, n_state2 // 2))],
        out_specs=pl.BlockSpec((1, n_chunks, width), lambda g, b: (g, b, 0)),
        out_shape=jax.ShapeDtypeStruct(u_sg.shape, F32),
        scratch_shapes=[pltpu.VMEM((n_chunks, n_state2), F32)] * 2,
        compiler_params=_params(2),
    )(u_sg, toep, inj, out, aq)


def _even_post_kernel(h_ref, yc_ref, ysg_ref, u_ref, perm_ref, d_ref, wglu_ref, bglu_ref, woc_ref,
                      wos_ref, g_ref, b_ref, rwh_ref, rwl_ref, rb_ref, tri_ref, *out_and_scratch):
    n_sg = ysg_ref.shape[0]
    by_pos = jnp.concatenate(
        [jnp.concatenate([ysg_ref[sg, :, s * LANES:(s + 1) * LANES] for sg in range(n_sg)], axis=1)
         for s in range(SSM_CHUNK)], axis=0)
    hi, lo = _split_bf16(by_pos)
    y_scan = (jnp.dot(perm_ref[...], hi, preferred_element_type=F32)
              + jnp.dot(perm_ref[...], lo, preferred_element_type=F32))
    y = y_scan + d_ref[...] * u_ref[...]
    z = jax.nn.gelu(y)
    glu = jnp.dot(z.astype(BF16), wglu_ref[...], preferred_element_type=F32) + bglu_ref[...]
    z = z * _sigmoid(glu)
    mix = (jnp.dot(yc_ref[...].astype(BF16), woc_ref[...], preferred_element_type=F32)
           + jnp.dot(z.astype(BF16), wos_ref[...], preferred_element_type=F32))
    _norm_and_route(DN_ALPHA * h_ref[...] + mix, g_ref, b_ref, (rwh_ref, rwl_ref, rb_ref, tri_ref),
                    *out_and_scratch)


def _even_post(h, y_conv, y_sg, u, consts):
    t = h.shape[0]
    n_sg = y_sg.shape[0]
    out_specs, out_shape = _post_outputs(t)
    consts = [_chunk_perm(ROW_TILE).T, *consts]
    return pl.pallas_call(
        _even_post_kernel,
        grid=(t // ROW_TILE,),
        in_specs=[_rows(D_MODEL), _rows(CONV_CH),
                  pl.BlockSpec((n_sg, ROW_TILE // SSM_CHUNK, SSM_CHUNK * LANES), lambda i: (0, i, 0)),
                  _rows(SSM_WIDTH)] + [_full(a.shape) for a in consts],
        out_specs=out_specs, out_shape=out_shape,
        scratch_shapes=[pltpu.VMEM((8, LANES), F32)],
        compiler_params=_params(),
    )(h, y_conv, y_sg, u, *consts)


def _qkv_rope_kernel(x_ref, w_ref, cos_ref, sin_ref, p4_ref, p16_ref, o1_ref, o4_ref, o16_ref):
    qkv = jnp.dot(x_ref[...].astype(BF16), w_ref[...], preferred_element_type=F32)
    rows = qkv.shape[0]
    cos = jnp.concatenate([cos_ref[...]] * (D_MODEL // LANES), axis=1)
    sin = jnp.concatenate([sin_ref[...]] * (D_MODEL // LANES), axis=1)
    lane = lax.broadcasted_iota(jnp.int32, (rows, D_MODEL), 1)
    low_half = (lane % HEAD_DIM) < (HEAD_DIM // 2)

    def rope(xs):
        up = pltpu.roll(xs, D_MODEL - HEAD_DIM // 2, 1)
        down = pltpu.roll(xs, HEAD_DIM // 2, 1)
        return xs * cos + jnp.where(low_half, up, down) * sin

    parts = (rope(qkv[:, :D_MODEL]) * (HEAD_DIM ** -0.5), rope(qkv[:, D_MODEL:2 * D_MODEL]),
             qkv[:, 2 * D_MODEL:])
    for which, part in enumerate(parts):
        cols = slice(which * D_MODEL, (which + 1) * D_MODEL)
        part = part.astype(BF16)
        o1_ref[:, cols] = part
        for dil, perm_ref, ref in ((DILATIONS[1], p4_ref, o4_ref), (DILATIONS[2], p16_ref, o16_ref)):
            split = jnp.dot(perm_ref[...], part, preferred_element_type=F32).astype(BF16)
            n = rows // dil
            for r in range(dil):
                ref[0, r, :, cols] = split[r * n:(r + 1) * n, :]


def _rope_tables(seq_len):
    half = HEAD_DIM // 2
    inv = ROPE_THETA ** (-jnp.arange(half, dtype=F32) / half)
    ang = jnp.arange(seq_len, dtype=F32)[:, None] * inv[None, :]
    cos = jnp.tile(jnp.cos(ang), (1, LANES // half))
    sin = jnp.sin(ang)
    sin = jnp.tile(jnp.concatenate([-sin, sin], axis=1), (1, LANES // HEAD_DIM))
    return cos, sin


def _residue_spec(dil, rows, width, tiles_per_seq):
    return pl.BlockSpec((1, dil, rows // dil, width),
                        lambda i: (i // tiles_per_seq, 0, i % tiles_per_seq, 0))


def _qkv_rope(h, w_qkv_bf, cos, sin, batch, seq_len):
    t = h.shape[0]
    tps = seq_len // QKV_TILE
    d4, d16 = DILATIONS[1], DILATIONS[2]
    width = 3 * D_MODEL
    perms = []
    for dil in (d4, d16):
        j = jnp.arange(QKV_TILE, dtype=jnp.int32)
        src = (j % (QKV_TILE // dil)) * dil + j // (QKV_TILE // dil)
        perms.append((src[:, None] == j[None, :]).astype(BF16))
    return pl.pallas_call(
        _qkv_rope_kernel,
        grid=(t // QKV_TILE,),
        in_specs=[_rows(D_MODEL, QKV_TILE), _full(w_qkv_bf.shape),
                  pl.BlockSpec((QKV_TILE, LANES), lambda i: (i % tps, 0)),
                  pl.BlockSpec((QKV_TILE, LANES), lambda i: (i % tps, 0)),
                  _full((QKV_TILE, QKV_TILE)), _full((QKV_TILE, QKV_TILE))],
        out_specs=[_rows(width, QKV_TILE), _residue_spec(d4, QKV_TILE, width, tps),
                   _residue_spec(d16, QKV_TILE, width, tps)],
        out_shape=[jax.ShapeDtypeStruct((t, width), BF16),
                   jax.ShapeDtypeStruct((batch, d4, seq_len // d4, width), BF16),
                   jax.ShapeDtypeStruct((batch, d16, seq_len // d16, width), BF16)],
        compiler_params=_params(),
    )(h, w_qkv_bf, cos, sin, *perms)


def _attn_kernel(q_ref, kp_ref, kc_ref, vp_ref, vc_ref, o_ref, lse_ref):
    blk = ATT_BLOCK
    qi = lax.broadcasted_iota(jnp.int32, (blk, 2 * blk), 0)
    kj = lax.broadcasted_iota(jnp.int32, (blk, 2 * blk), 1)
    dist = blk + qi - kj
    band = jnp.where(dist >= 0, jnp.where(dist <= blk, 0.0, NEG_BIG), NEG_BIG)
    k_min = jnp.where(pl.program_id(1) == 0, blk, 0)
    lane = lax.broadcasted_iota(jnp.int32, (blk, LANES), 1)
    low = lane < HEAD_DIM
    for j in range(ATT_BLOCKS_PER_STEP):
        rows = slice(j * blk, (j + 1) * blk)
        before = slice((j - 1) * blk, j * blk)
        bias = jnp.where(kj >= k_min, band, NEG_BIG) if j == 0 else band
        lse_tile = jnp.zeros((blk, LANES), F32)
        for hp in range(N_HEADS // 2):
            cols = slice(hp * LANES, (hp + 1) * LANES)
            q2 = q_ref[0, rows, cols]
            k_prev = kp_ref[0, :, cols] if j == 0 else kc_ref[0, before, cols]
            v_prev = vp_ref[0, :, cols] if j == 0 else vc_ref[0, before, cols]
            kk = jnp.concatenate([k_prev, kc_ref[0, rows, cols]], axis=0)
            vv = jnp.concatenate([v_prev, vc_ref[0, rows, cols]], axis=0)
            halves = []
            for hh in range(2):
                head = 2 * hp + hh
                qm = jnp.where(low if hh == 0 else jnp.logical_not(low), q2, jnp.zeros_like(q2))
                s = lax.dot_general(qm, kk, (((1,), (1,)), ((), ())),
                                    preferred_element_type=F32) + bias
                m = jnp.max(s, axis=-1, keepdims=True)
                p = jnp.exp(s - m)
                den = jnp.sum(p, axis=-1, keepdims=True)
                pv = jnp.dot(p.astype(BF16), vv, preferred_element_type=F32)
                halves.append(pv / den)
                lse_tile = jnp.where(lane == head, m + jnp.log(den), lse_tile)
            o_ref[0, rows, cols] = jnp.where(low, halves[0], halves[1]).astype(BF16)
        lse_ref[0, rows, :] = lse_tile


def _attn_pattern(qkv_sub):
    n_sub, sub_len, _ = qkv_sub.shape

    step = ATT_BLOCKS_PER_STEP * ATT_BLOCK

    def cur(which):
        return pl.BlockSpec((1, step, D_MODEL), lambda s, n: (s, n, which))

    def prev(which):
        return pl.BlockSpec((1, ATT_BLOCK, D_MODEL),
                            lambda s, n: (s, jnp.maximum(n * ATT_BLOCKS_PER_STEP - 1, 0), which))

    return pl.pallas_call(
        _attn_kernel,
        grid=(n_sub, sub_len // step),
        in_specs=[cur(0), prev(1), cur(1), prev(2), cur(2)],
        out_specs=[pl.BlockSpec((1, step, D_MODEL), lambda s, n: (s, n, 0)),
                   pl.BlockSpec((1, step, LANES), lambda s, n: (s, n, 0))],
        out_shape=[jax.ShapeDtypeStruct((n_sub, sub_len, D_MODEL), BF16),
                   jax.ShapeDtypeStruct((n_sub, sub_len, LANES), F32)],
        compiler_params=_params(2),
    )(*([qkv_sub] * 5))


def _odd_post_kernel(h_ref, o1_ref, l1_ref, o4_ref, l4_ref, o16_ref, l16_ref, expand_ref, wo_ref,
                     g_ref, b_ref, rwh_ref, rwl_ref, rb_ref, tri_ref, hn_ref, hnb_ref, idx_ref,
                     gate_ref, cnt_ref, os4_ref, ls4_ref, os16_ref, ls16_ref, seen_ref):
    rows = h_ref.shape[0]
    n_col = D_MODEL // LANES
    for dil, o_ref, l_ref, os_ref, ls_ref in ((DILATIONS[1], o4_ref, l4_ref, os4_ref, ls4_ref),
                                              (DILATIONS[2], o16_ref, l16_ref, os16_ref, ls16_ref)):
        for r in range(dil):
            sel = pl.ds(r, rows // dil, stride=dil)
            ls_ref[sel, :] = l_ref[0, r]
            for c in range(n_col):
                os_ref[c, sel, :] = o_ref[0, r, :, c * LANES:(c + 1) * LANES].astype(F32)
    lses = (l1_ref[...], ls4_ref[...], ls16_ref[...])
    outs = (o1_ref[...].astype(F32),
            jnp.concatenate([os4_ref[c] for c in range(n_col)], axis=1),
            jnp.concatenate([os16_ref[c] for c in range(n_col)], axis=1))
    mx = jnp.maximum(jnp.maximum(lses[0], lses[1]), lses[2])
    es = [jnp.exp(l - mx) for l in lses]
    den = es[0] + es[1] + es[2]
    o = jnp.zeros((rows, D_MODEL), F32)
    for e, out in zip(es, outs):
        hi, lo = _split_bf16(e / den)
        w = (jnp.dot(hi, expand_ref[...], preferred_element_type=F32)
             + jnp.dot(lo, expand_ref[...], preferred_element_type=F32))
        o = o + w * out
    mix = jnp.dot(o.astype(BF16), wo_ref[...], preferred_element_type=F32)
    _norm_and_route(DN_ALPHA * h_ref[...] + mix, g_ref, b_ref, (rwh_ref, rwl_ref, rb_ref, tri_ref),
                    hn_ref, hnb_ref, idx_ref, gate_ref, cnt_ref, seen_ref)


def _odd_post(h, pattern_outs, consts, seq_len):
    t = h.shape[0]
    tps = seq_len // ROW_TILE
    (o1, l1), (o4, l4), (o16, l16) = pattern_outs
    d4, d16 = DILATIONS[1], DILATIONS[2]
    out_specs, out_shape = _post_outputs(t)
    return pl.pallas_call(
        _odd_post_kernel,
        grid=(t // ROW_TILE,),
        in_specs=[_rows(D_MODEL), _rows(D_MODEL), _rows(LANES),
                  _residue_spec(d4, ROW_TILE, D_MODEL, tps), _residue_spec(d4, ROW_TILE, LANES, tps),
                  _residue_spec(d16, ROW_TILE, D_MODEL, tps), _residue_spec(d16, ROW_TILE, LANES, tps)]
        + [_full(a.shape) for a in consts],
        out_specs=out_specs, out_shape=out_shape,
        scratch_shapes=[pltpu.VMEM((D_MODEL // LANES, ROW_TILE, LANES), F32),
                        pltpu.VMEM((ROW_TILE, LANES), F32)] * 2 + [pltpu.VMEM((8, LANES), F32)],
        compiler_params=_params(),
    )(h, o1, l1, o4, l4, o16, l16, *consts)


def _attention_layer(h, w_qkv, w_o, cos, sin, consts, batch, seq_len):
    t = h.shape[0]
    qkv1, qkv4, qkv16 = _qkv_rope(h, w_qkv.astype(BF16), cos, sin, batch, seq_len)
    outs = []
    for dil, qkv in zip(DILATIONS, (qkv1, qkv4, qkv16)):
        sub_len = seq_len // dil
        o, lse = _attn_pattern(qkv.reshape(batch * dil, sub_len, 3 * D_MODEL))
        if dil == 1:
            outs.append((o.reshape(t, D_MODEL), lse.reshape(t, LANES)))
        else:
            outs.append((o.reshape(batch, dil, sub_len, D_MODEL),
                         lse.reshape(batch, dil, sub_len, LANES)))
    head_of_lane = jnp.arange(D_MODEL, dtype=jnp.int32) // HEAD_DIM
    expand = (jnp.arange(LANES, dtype=jnp.int32)[:, None] == head_of_lane[None, :]).astype(BF16)
    return _odd_post(h, outs, [expand, w_o.astype(BF16), *consts], seq_len)


def _moe_kernel(be_ref, nused_ref, x_ref, wgu_ref, bgu_ref, wd_ref, bd_ref, y_ref,
                wgu_bf_ref, wd_bf_ref):
    i = pl.program_id(0)
    new_expert = jnp.logical_or(i == 0, be_ref[i] != be_ref[jnp.maximum(i - 1, 0)])

    @pl.when(new_expert)
    def _():
        wgu_bf_ref[...] = wgu_ref[0, 0].astype(BF16)
        wd_bf_ref[...] = wd_ref[0, 0].astype(BF16)

    @pl.when(i < nused_ref[0])
    def _():
        hid = jnp.dot(x_ref[...], wgu_bf_ref[...], preferred_element_type=F32) + bgu_ref[0]
        gate = jnp.minimum(hid[:, :D_MODEL], SWIGLU_LIMIT)
        lin = jnp.clip(hid[:, D_MODEL:], -SWIGLU_LIMIT, SWIGLU_LIMIT)
        act = (lin + 1.0) * (gate * _sigmoid(SWIGLU_ALPHA * gate))
        y = jnp.dot(act.astype(BF16), wd_bf_ref[...], preferred_element_type=F32) + bd_ref[0]
        y_ref[...] = y.astype(BF16)

    @pl.when(i >= nused_ref[0])
    def _():
        y_ref[...] = jnp.zeros_like(y_ref)


def _moe_experts(xs, block_e, n_used, layer, w_gu, b_gu, w_down, b_down):
    n_rows = xs.shape[0]
    grid_spec = pltpu.PrefetchScalarGridSpec(
        num_scalar_prefetch=2,
        grid=(n_rows // MOE_ROWS,),
        in_specs=[pl.BlockSpec((MOE_ROWS, D_MODEL), lambda i, be, nu: (i, 0)),
                  pl.BlockSpec((1, 1, D_MODEL, 2 * D_MODEL), lambda i, be, nu: (layer, be[i], 0, 0)),
                  pl.BlockSpec((1, 1, 2 * D_MODEL), lambda i, be, nu: (be[i], 0, 0)),
                  pl.BlockSpec((1, 1, D_MODEL, D_MODEL), lambda i, be, nu: (layer, be[i], 0, 0)),
                  pl.BlockSpec((1, 1, D_MODEL), lambda i, be, nu: (be[i], 0, 0))],
        out_specs=pl.BlockSpec((MOE_ROWS, D_MODEL), lambda i, be, nu: (i, 0)),
        scratch_shapes=[pltpu.VMEM((D_MODEL, 2 * D_MODEL), BF16), pltpu.VMEM((D_MODEL, D_MODEL), BF16)],
    )
    return pl.pallas_call(
        _moe_kernel,
        grid_spec=grid_spec,
        out_shape=jax.ShapeDtypeStruct((n_rows, D_MODEL), BF16),
        compiler_params=_params(),
    )(block_e, n_used, xs, w_gu, b_gu[:, None, :], w_down, b_down[:, None, :])


def _routing_tables(idx_tile, counts, t):
    n_assign = t * TOP_K
    top_idx = idx_tile[:, :TOP_K]
    rank = idx_tile[:, TOP_K:2 * TOP_K]
    counts = counts[0, :N_EXPERTS].astype(jnp.int32)
    padded = (counts + MOE_ROWS - 1) // MOE_ROWS * MOE_ROWS
    pad_end = jnp.cumsum(padded)
    pad_start = pad_end - padded
    experts = jnp.arange(N_EXPERTS, dtype=jnp.int32)
    start_of = jnp.sum(jnp.where(top_idx[..., None] == experts, pad_start, 0), axis=-1)
    dest = (start_of + rank).T.reshape(-1)
    n_blocks = n_assign // MOE_ROWS + N_EXPERTS
    flat_tok = jnp.arange(n_assign, dtype=jnp.int32) % t
    row_tok = (jnp.arange(n_blocks * MOE_ROWS, dtype=jnp.int32) % t).at[dest].set(
        flat_tok, unique_indices=True, mode="promise_in_bounds")
    block_start = jnp.arange(n_blocks, dtype=jnp.int32) * MOE_ROWS
    block_e = jnp.minimum(jnp.sum((block_start[:, None] >= pad_end[None, :]).astype(jnp.int32), axis=1),
                          N_EXPERTS - 1)
    n_used = (pad_end[-1:] // MOE_ROWS).astype(jnp.int32)
    return row_tok, dest, block_e, n_used


def _ffn_ln_kernel(h_ref, y4_ref, gate_ref, g_ref, b_ref, o_ref):
    gates = gate_ref[...]
    acc = DN_ALPHA * h_ref[...]
    for k in range(TOP_K):
        acc = acc + gates[:, k:k + 1] * y4_ref[k].astype(F32)
    o_ref[...] = _layer_norm(acc, g_ref[...], b_ref[...])


def _ffn_ln(h, y4, gates, g, b):
    t = h.shape[0]
    return pl.pallas_call(
        _ffn_ln_kernel,
        grid=(t // ROW_TILE,),
        in_specs=[_rows(D_MODEL), pl.BlockSpec((TOP_K, ROW_TILE, D_MODEL), lambda i: (0, i, 0)),
                  _rows(LANES), _full(g.shape), _full(b.shape)],
        out_specs=_rows(D_MODEL),
        out_shape=jax.ShapeDtypeStruct((t, D_MODEL), F32),
        compiler_params=_params(),
    )(h, y4, gates, g, b)


def _moe_layer(hn, hn_bf, idx_tile, gate_tile, counts, layer, w_gu, b_gu, w_down, b_down, ln_g, ln_b):
    t = hn.shape[0]
    row_tok, dest, block_e, n_used = _routing_tables(idx_tile, counts, t)
    xs = hn_bf.at[row_tok].get(mode="promise_in_bounds")
    ys = _moe_experts(xs, block_e, n_used, layer, w_gu, b_gu, w_down, b_down)
    y4 = ys.at[dest].get(mode="promise_in_bounds").reshape(TOP_K, t, D_MODEL)
    return _ffn_ln(hn, y4, gate_tile, ln_g[None, :], ln_b[None, :])


def _router_consts(router_w, router_b):
    w = jnp.zeros((D_MODEL, LANES), F32).at[:, :N_EXPERTS].set(router_w)
    w_hi, w_lo = _split_bf16(w)
    b = jnp.full((1, LANES), NEG_BIG, F32).at[0, :N_EXPERTS].set(router_b)
    r = jnp.arange(ROW_TILE, dtype=jnp.int32)
    tri = (r[None, :] < r[:, None]).astype(BF16)
    return w_hi, w_lo, b, tri


def kernel(x, hy_w_in, conv_w, ssm_a_re, ssm_a_im, ssm_log_dt, ssm_b_re, ssm_b_im, ssm_c_re,
           ssm_c_im, ssm_d, ssm_w_glu, ssm_b_glu, hy_w_out, att_w_qkv, att_w_o, ln_mix_g,
           ln_mix_b, ln_ffn_g, ln_ffn_b, router_w, router_b, expert_w_gu, expert_b_gu,
           expert_w_down, expert_b_down):
    batch, seq_len, _ = x.shape
    t = batch * seq_len
    h = x.reshape(t, D_MODEL)
    cos, sin = _rope_tables(seq_len)
    for layer in range(DEPTH):
        i = layer // 2
        consts = [ln_mix_g[layer][None, :], ln_mix_b[layer][None, :],
                  *_router_consts(router_w[layer], router_b[layer])]
        if layer % 2 == 0:
            y_conv, u, u_sg = _inproj_conv(h, hy_w_in[i].astype(BF16), conv_w[i], seq_len)
            tables = _s5_tables(ssm_a_re[i], ssm_a_im[i], ssm_log_dt[i], ssm_b_re[i], ssm_b_im[i],
                                ssm_c_re[i], ssm_c_im[i])
            y_sg = _s5_scan(u_sg, tables, batch, seq_len // SSM_CHUNK)
            w_out = hy_w_out[i].astype(BF16)
            hn, hn_bf, idx_tile, gate_tile, counts = _even_post(
                h, y_conv, y_sg, u,
                [ssm_d[i].reshape(1, SSM_WIDTH), ssm_w_glu[i].astype(BF16), ssm_b_glu[i][None, :],
                 w_out[:CONV_CH], w_out[CONV_CH:], *consts])
        else:
            hn, hn_bf, idx_tile, gate_tile, counts = _attention_layer(
                h, att_w_qkv[i], att_w_o[i], cos, sin, consts, batch, seq_len)
        h = _moe_layer(hn, hn_bf, idx_tile, gate_tile, counts, layer, expert_w_gu, expert_b_gu[layer],
                       expert_w_down, expert_b_down[layer], ln_ffn_g[layer], ln_ffn_b[layer])
    return h.reshape(batch, seq_len, D_MODEL)
```

```python
import functools

import jax
import jax.numpy as jnp
from jax import lax
from jax.experimental import pallas as pl
from jax.experimental.pallas import tpu as pltpu

F32 = jnp.float32
BF16 = jnp.bfloat16

D_MODEL = 1024
DEPTH = 4
CONV_CH = 512
SSM_WIDTH = 512
SSM_GROUP = 16
SSM_GROUPS = 32
SSM_STATE = 64
N_HEADS = 16
HEAD_DIM = 64
ROPE_THETA = 10000.0
DILATIONS = (1, 4, 16)
ATT_BLOCK = 128
ATT_BLOCKS_PER_STEP = 4
N_EXPERTS = 32
TOP_K = 4
SWIGLU_LIMIT = 7.0
SWIGLU_ALPHA = 1.702
DN_ALPHA = (2 * DEPTH) ** 0.25
LN_EPS = 1e-5

LANES = 128
ROW_TILE = 512
QKV_TILE = 256
SSM_CHUNK = 8
MOE_ROWS = 512
VMEM_LIMIT = 56 * 1024 * 1024
NEG_BIG = -1e30


def _params(n_axes=1):
    return pltpu.CompilerParams(dimension_semantics=("arbitrary",) * n_axes,
                                vmem_limit_bytes=VMEM_LIMIT)


def _full(shape):
    return pl.BlockSpec(shape, lambda *_: (0,) * len(shape))


def _rows(width, tile=ROW_TILE):
    return pl.BlockSpec((tile, width), lambda i: (i, 0))


def _layer_norm(x, g, b):
    mu = jnp.mean(x, axis=-1, keepdims=True)
    xc = x - mu
    var = jnp.mean(xc * xc, axis=-1, keepdims=True)
    return xc * lax.rsqrt(var + LN_EPS) * g + b


def _sigmoid(x):
    return 1.0 / (1.0 + jnp.exp(-x))


def _split_bf16(x):
    hi = x.astype(BF16)
    return hi, (x - hi.astype(F32)).astype(BF16)


def _route(hn, rw_hi_ref, rw_lo_ref, rb_ref, tri_ref, idx_ref, gate_ref, cnt_ref, seen_ref):
    @pl.when(pl.program_id(0) == 0)
    def _():
        seen_ref[...] = jnp.zeros_like(seen_ref)

    hi, lo = _split_bf16(hn)
    logits = (jnp.dot(hi, rw_hi_ref[...], preferred_element_type=F32)
              + jnp.dot(hi, rw_lo_ref[...], preferred_element_type=F32)
              + jnp.dot(lo, rw_hi_ref[...], preferred_element_type=F32)
              + rb_ref[...])
    lane = lax.broadcasted_iota(jnp.int32, logits.shape, 1)
    lane_f = lane.astype(F32)
    idx_f = jnp.zeros(logits.shape, F32)
    val_tile = jnp.zeros(logits.shape, F32)
    top0 = None
    den = None
    firsts = []
    for k in range(TOP_K):
        mx = jnp.max(logits, axis=-1, keepdims=True)
        first = jnp.min(jnp.where(logits == mx, lane_f, float(LANES)), axis=-1, keepdims=True)
        if k == 0:
            top0 = mx
        e = jnp.exp(mx - top0)
        den = e if k == 0 else den + e
        idx_f = jnp.where(lane == k, first, idx_f)
        val_tile = jnp.where(lane == k, e, val_tile)
        logits = jnp.where(lane_f == first, -jnp.inf, logits)
        firsts.append(first)
    gate_ref[...] = val_tile / den
    picked = jnp.where(logits == -jnp.inf, 1.0, 0.0)
    seen = seen_ref[0:1, :]
    before = jnp.dot(tri_ref[...], picked.astype(BF16), preferred_element_type=F32) + seen
    for k, first in enumerate(firsts):
        rank = jnp.sum(jnp.where(lane_f == first, before, 0.0), axis=-1, keepdims=True)
        idx_f = jnp.where(lane == TOP_K + k, rank, idx_f)
    idx_ref[...] = idx_f.astype(jnp.int32)
    seen = seen + jnp.sum(picked, axis=0, keepdims=True)
    seen_ref[...] = jnp.broadcast_to(seen, seen_ref.shape)
    cnt_ref[...] = jnp.broadcast_to(seen, cnt_ref.shape)


def _post_outputs(t):
    specs = [_rows(D_MODEL), _rows(D_MODEL), _rows(LANES), _rows(LANES), _full((8, LANES))]
    shapes = [jax.ShapeDtypeStruct((t, D_MODEL), F32), jax.ShapeDtypeStruct((t, D_MODEL), BF16),
              jax.ShapeDtypeStruct((t, LANES), jnp.int32), jax.ShapeDtypeStruct((t, LANES), F32),
              jax.ShapeDtypeStruct((8, LANES), F32)]
    return specs, shapes


def _norm_and_route(pre, g_ref, b_ref, route_refs, hn_ref, hnb_ref, idx_ref, gate_ref, cnt_ref,
                    seen_ref):
    hn = _layer_norm(pre, g_ref[...], b_ref[...])
    hn_ref[...] = hn
    hnb_ref[...] = hn.astype(BF16)
    _route(hn, *route_refs, idx_ref, gate_ref, cnt_ref, seen_ref)


def _chunk_perm(rows):
    j = jnp.arange(rows, dtype=jnp.int32)
    n = rows // SSM_CHUNK
    src = (j % n) * SSM_CHUNK + j // n
    return (src[:, None] == j[None, :]).astype(BF16)


def _inproj_conv_kernel(x_ref, w_ref, cw_ref, perm_ref, yconv_ref, u_ref, usg_ref, carry_ref, *,
                        tiles_per_seq):
    @pl.when(pl.program_id(0) % tiles_per_seq == 0)
    def _():
        carry_ref[...] = jnp.zeros_like(carry_ref)

    proj = jnp.dot(x_ref[...].astype(BF16), w_ref[...], preferred_element_type=F32)
    gate_b = proj[:, :CONV_CH]
    gate_c = proj[:, CONV_CH:2 * CONV_CH]
    hid = proj[:, 2 * CONV_CH:3 * CONV_CH]
    v = gate_c * hid
    rows = v.shape[0]
    row = lax.broadcasted_iota(jnp.int32, v.shape, 0)
    prev1 = carry_ref[7:8, :]
    prev2 = carry_ref[6:7, :]
    vm1 = jnp.where(row == 0, prev1, pltpu.roll(v, 1, 0))
    vm2 = jnp.where(row == 0, prev2, jnp.where(row == 1, prev1, pltpu.roll(v, 2, 0)))
    conv = cw_ref[0:1, :] * vm2 + cw_ref[1:2, :] * vm1 + cw_ref[2:3, :] * v
    yconv_ref[...] = gate_b * conv
    u = proj[:, 3 * CONV_CH:]
    u_ref[...] = u
    carry_ref[...] = v[rows - 8:, :]
    by_pos = jnp.dot(perm_ref[...], u.astype(BF16), preferred_element_type=F32).astype(BF16)
    n = rows // SSM_CHUNK
    for sg in range(SSM_WIDTH // LANES):
        for s in range(SSM_CHUNK):
            usg_ref[sg, :, s * LANES:(s + 1) * LANES] = by_pos[s * n:(s + 1) * n,
                                                               sg * LANES:(sg + 1) * LANES]


def _inproj_conv(h, w_in_bf, conv_w, seq_len):
    t = h.shape[0]
    n_sg = SSM_WIDTH // LANES
    perm = _chunk_perm(ROW_TILE)
    return pl.pallas_call(
        functools.partial(_inproj_conv_kernel, tiles_per_seq=seq_len // ROW_TILE),
        grid=(t // ROW_TILE,),
        in_specs=[_rows(D_MODEL), _full(w_in_bf.shape), _full(conv_w.shape), _full(perm.shape)],
        out_specs=[_rows(CONV_CH), _rows(SSM_WIDTH),
                   pl.BlockSpec((n_sg, ROW_TILE // SSM_CHUNK, SSM_CHUNK * LANES), lambda i: (0, i, 0))],
        out_shape=[jax.ShapeDtypeStruct((t, CONV_CH), F32),
                   jax.ShapeDtypeStruct((t, SSM_WIDTH), F32),
                   jax.ShapeDtypeStruct((n_sg, t // SSM_CHUNK, SSM_CHUNK * LANES), BF16)],
        scratch_shapes=[pltpu.VMEM((8, CONV_CH), F32)],
        compiler_params=_params(),
    )(h, w_in_bf, conv_w, perm)


def _s5_tables(a_re, a_im, log_dt, b_re, b_im, c_re, c_im):
    q = SSM_CHUNK
    per_sg = LANES // SSM_GROUP
    n_sg = SSM_GROUPS // per_sg
    lam_re = jnp.minimum(a_re, -1e-4)
    lam_im = a_im
    dt = jnp.exp(log_dt)[:, None]
    mag = jnp.exp(lam_re * dt)
    ab_re = mag * jnp.cos(lam_im * dt)
    ab_im = mag * jnp.sin(lam_im * dt)
    nr, ni = ab_re - 1.0, ab_im
    den = lam_re * lam_re + lam_im * lam_im
    coef_re = ((nr * lam_re + ni * lam_im) / den)[..., None]
    coef_im = ((ni * lam_re - nr * lam_im) / den)[..., None]
    bb_re = coef_re * b_re - coef_im * b_im
    bb_im = coef_re * b_im + coef_im * b_re
    j = jnp.arange(q + 1, dtype=F32)
    pmag = jnp.exp((lam_re * dt)[..., None] * j)
    pang = (lam_im * dt)[..., None] * j
    pw_re = pmag * jnp.cos(pang)
    pw_im = pmag * jnp.sin(pang)
    ca_re = c_re[..., None] * pw_re[:, None] - c_im[..., None] * pw_im[:, None]
    ca_im = c_re[..., None] * pw_im[:, None] + c_im[..., None] * pw_re[:, None]
    kern = (jnp.einsum("gapj,gph->gjah", ca_re, bb_re, precision="highest")
            - jnp.einsum("gapj,gph->gjah", ca_im, bb_im, precision="highest"))[:, :q]
    lag = jnp.arange(q)[None, :] - jnp.arange(q)[:, None]
    toep = kern[:, jnp.clip(lag, 0, q - 1)]
    toep = jnp.where((lag >= 0)[None, :, :, None, None], toep, 0.0).transpose(0, 1, 4, 2, 3)
    rev = q - 1 - jnp.arange(q)
    pr = pw_re[:, :, rev]
    pi = pw_im[:, :, rev]
    inj_re = pr[..., None] * bb_re[:, :, None] - pi[..., None] * bb_im[:, :, None]
    inj_im = pr[..., None] * bb_im[:, :, None] + pi[..., None] * bb_re[:, :, None]
    inj_re = inj_re.transpose(0, 2, 3, 1)
    inj_im = inj_im.transpose(0, 2, 3, 1)
    out_re = ca_re[..., 1:].transpose(0, 2, 3, 1)
    out_im = (-ca_im[..., 1:]).transpose(0, 2, 3, 1)
    eye = jnp.eye(per_sg, dtype=F32)
    width = q * LANES
    n_state = per_sg * SSM_STATE

    def sg(a):
        return a.reshape((n_sg, per_sg) + a.shape[1:])

    toep_sg = jnp.einsum("Scshtk,cd->Sschtdk", sg(toep), eye).reshape(n_sg, width, width)
    inj_sg = jnp.concatenate(
        [jnp.einsum("Scshp,cd->Sschdp", sg(a), eye).reshape(n_sg, width, n_state)
         for a in (inj_re, inj_im)], axis=2)
    out_sg = jnp.concatenate(
        [jnp.einsum("Scpth,cd->Scptdh", sg(a), eye).reshape(n_sg, n_state, width)
         for a in (out_re, out_im)], axis=1)
    aq = jnp.stack([pw_re[..., q].reshape(n_sg, n_state), pw_im[..., q].reshape(n_sg, n_state)], axis=1)
    return toep_sg.astype(BF16), inj_sg.astype(BF16), out_sg.astype(BF16), aq


def _s5_kernel(u_ref, toep_ref, inj_ref, out_ref, aq_ref, y_ref, inj_scr, xs_scr, *, n_chunks):
    n_state = aq_ref.shape[2]
    u = u_ref[0]
    inj_scr[...] = jnp.dot(u, inj_ref[0], preferred_element_type=F32)
    aq_re = jnp.broadcast_to(aq_ref[0, 0:1, :], (8, n_state))
    aq_im = jnp.broadcast_to(aq_ref[0, 1:2, :], (8, n_state))
    sub = lax.broadcasted_iota(jnp.int32, (8, n_state), 0)

    def block(blk, carry):
        re, im = carry
        start = pl.multiple_of(blk * 8, 8)
        inj = inj_scr[pl.ds(start, 8), :]
        xs_re = jnp.zeros((8, n_state), F32)
        xs_im = jnp.zeros((8, n_state), F32)
        for j in range(8):
            xs_re = jnp.where(sub == j, re, xs_re)
            xs_im = jnp.where(sub == j, im, xs_im)
            in_re = jnp.broadcast_to(inj[j:j + 1, :n_state], (8, n_state))
            in_im = jnp.broadcast_to(inj[j:j + 1, n_state:], (8, n_state))
            re, im = aq_re * re - aq_im * im + in_re, aq_re * im + aq_im * re + in_im
        xs_scr[pl.ds(start, 8), :n_state] = xs_re
        xs_scr[pl.ds(start, 8), n_state:] = xs_im
        return re, im

    zero = jnp.zeros((8, n_state), F32)
    lax.fori_loop(0, n_chunks // 8, block, (zero, zero))
    y = jnp.dot(u, toep_ref[0], preferred_element_type=F32)
    y_ref[0] = y + jnp.dot(xs_scr[...].astype(BF16), out_ref[0], preferred_element_type=F32)


def _s5_scan(u_sg, tables, batch, n_chunks):
    toep, inj, out, aq = tables
    n_sg, _, width = u_sg.shape
    n_state2 = inj.shape[2]

    def per_sg(shape):
        return pl.BlockSpec((1,) + shape, lambda g, b: (g, 0, 0))

    return pl.pallas_call(
        functools.partial(_s5_kernel, n_chunks=n_chunks),
        grid=(n_sg, batch),
        in_specs=[pl.BlockSpec((1, n_chunks, width), lambda g, b: (g, b, 0)),
                  per_sg((width, width)), per_sg((width, n_state2)), per_sg((n_state2, width)),
                  per_sg((2, n_state2 // 2))],
        out_specs=pl.BlockSpec((1, n_chunks, width), lambda g, b: (g, b, 0)),
        out_shape=jax.ShapeDtypeStruct(u_sg.shape, F32),
        scratch_shapes=[pltpu.VMEM((n_chunks, n_state2), F32)] * 2,
        compiler_params=_params(2),
    )(u_sg, toep, inj, out, aq)


def _even_post_kernel(h_ref, yc_ref, ysg_ref, u_ref, perm_ref, d_ref, wglu_ref, bglu_ref, woc_ref,
                      wos_ref, g_ref, b_ref, rwh_ref, rwl_ref, rb_ref, tri_ref, *out_and_scratch):
    n_sg = ysg_ref.shape[0]
    by_pos = jnp.concatenate(
        [jnp.concatenate([ysg_ref[sg, :, s * LANES:(s + 1) * LANES] for sg in range(n_sg)], axis=1)
         for s in range(SSM_CHUNK)], axis=0)
    hi, lo = _split_bf16(by_pos)
    y_scan = (jnp.dot(perm_ref[...], hi, preferred_element_type=F32)
              + jnp.dot(perm_ref[...], lo, preferred_element_type=F32))
    y = y_scan + d_ref[...] * u_ref[...]
    z = jax.nn.gelu(y)
    glu = jnp.dot(z.astype(BF16), wglu_ref[...], preferred_element_type=F32) + bglu_ref[...]
    z = z * _sigmoid(glu)
    mix = (jnp.dot(yc_ref[...].astype(BF16), woc_ref[...], preferred_element_type=F32)
           + jnp.dot(z.astype(BF16), wos_ref[...], preferred_element_type=F32))
    _norm_and_route(DN_ALPHA * h_ref[...] + mix, g_ref, b_ref, (rwh_ref, rwl_ref, rb_ref, tri_ref),
                    *out_and_scratch)


def _even_post(h, y_conv, y_sg, u, consts):
    t = h.shape[0]
    n_sg = y_sg.shape[0]
    out_specs, out_shape = _post_outputs(t)
    consts = [_chunk_perm(ROW_TILE).T, *consts]
    return pl.pallas_call(
        _even_post_kernel,
        grid=(t // ROW_TILE,),
        in_specs=[_rows(D_MODEL), _rows(CONV_CH),
                  pl.BlockSpec((n_sg, ROW_TILE // SSM_CHUNK, SSM_CHUNK * LANES), lambda i: (0, i, 0)),
                  _rows(SSM_WIDTH)] + [_full(a.shape) for a in consts],
        out_specs=out_specs, out_shape=out_shape,
        scratch_shapes=[pltpu.VMEM((8, LANES), F32)],
        compiler_params=_params(),
    )(h, y_conv, y_sg, u, *consts)


def _qkv_rope_kernel(x_ref, w_ref, cos_ref, sin_ref, p4_ref, p16_ref, o1_ref, o4_ref, o16_ref):
    qkv = jnp.dot(x_ref[...].astype(BF16), w_ref[...], preferred_element_type=F32)
    rows = qkv.shape[0]
    cos = jnp.concatenate([cos_ref[...]] * (D_MODEL // LANES), axis=1)
    sin = jnp.concatenate([sin_ref[...]] * (D_MODEL // LANES), axis=1)
    lane = lax.broadcasted_iota(jnp.int32, (rows, D_MODEL), 1)
    low_half = (lane % HEAD_DIM) < (HEAD_DIM // 2)

    def rope(xs):
        up = pltpu.roll(xs, D_MODEL - HEAD_DIM // 2, 1)
        down = pltpu.roll(xs, HEAD_DIM // 2, 1)
        return xs * cos + jnp.where(low_half, up, down) * sin

    parts = (rope(qkv[:, :D_MODEL]) * (HEAD_DIM ** -0.5), rope(qkv[:, D_MODEL:2 * D_MODEL]),
             qkv[:, 2 * D_MODEL:])
    for which, part in enumerate(parts):
        cols = slice(which * D_MODEL, (which + 1) * D_MODEL)
        part = part.astype(BF16)
        o1_ref[:, cols] = part
        for dil, perm_ref, ref in ((DILATIONS[1], p4_ref, o4_ref), (DILATIONS[2], p16_ref, o16_ref)):
            split = jnp.dot(perm_ref[...], part, preferred_element_type=F32).astype(BF16)
            n = rows // dil
            for r in range(dil):
                ref[0, r, :, cols] = split[r * n:(r + 1) * n, :]


def _rope_tables(seq_len):
    half = HEAD_DIM // 2
    inv = ROPE_THETA ** (-jnp.arange(half, dtype=F32) / half)
    ang = jnp.arange(seq_len, dtype=F32)[:, None] * inv[None, :]
    cos = jnp.tile(jnp.cos(ang), (1, LANES // half))
    sin = jnp.sin(ang)
    sin = jnp.tile(jnp.concatenate([-sin, sin], axis=1), (1, LANES // HEAD_DIM))
    return cos, sin


def _residue_spec(dil, rows, width, tiles_per_seq):
    return pl.BlockSpec((1, dil, rows // dil, width),
                        lambda i: (i // tiles_per_seq, 0, i % tiles_per_seq, 0))


def _qkv_rope(h, w_qkv_bf, cos, sin, batch, seq_len):
    t = h.shape[0]
    tps = seq_len // QKV_TILE
    d4, d16 = DILATIONS[1], DILATIONS[2]
    width = 3 * D_MODEL
    perms = []
    for dil in (d4, d16):
        j = jnp.arange(QKV_TILE, dtype=jnp.int32)
        src = (j % (QKV_TILE // dil)) * dil + j // (QKV_TILE // dil)
        perms.append((src[:, None] == j[None, :]).astype(BF16))
    return pl.pallas_call(
        _qkv_rope_kernel,
        grid=(t // QKV_TILE,),
        in_specs=[_rows(D_MODEL, QKV_TILE), _full(w_qkv_bf.shape),
                  pl.BlockSpec((QKV_TILE, LANES), lambda i: (i % tps, 0)),
                  pl.BlockSpec((QKV_TILE, LANES), lambda i: (i % tps, 0)),
                  _full((QKV_TILE, QKV_TILE)), _full((QKV_TILE, QKV_TILE))],
        out_specs=[_rows(width, QKV_TILE), _residue_spec(d4, QKV_TILE, width, tps),
                   _residue_spec(d16, QKV_TILE, width, tps)],
        out_shape=[jax.ShapeDtypeStruct((t, width), BF16),
                   jax.ShapeDtypeStruct((batch, d4, seq_len // d4, width), BF16),
                   jax.ShapeDtypeStruct((batch, d16, seq_len // d16, width), BF16)],
        compiler_params=_params(),
    )(h, w_qkv_bf, cos, sin, *perms)


def _attn_kernel(q_ref, kp_ref, kc_ref, vp_ref, vc_ref, o_ref, lse_ref):
    blk = ATT_BLOCK
    qi = lax.broadcasted_iota(jnp.int32, (blk, 2 * blk), 0)
    kj = lax.broadcasted_iota(jnp.int32, (blk, 2 * blk), 1)
    dist = blk + qi - kj
    band = jnp.where(dist >= 0, jnp.where(dist <= blk, 0.0, NEG_BIG), NEG_BIG)
    k_min = jnp.where(pl.program_id(1) == 0, blk, 0)
    lane = lax.broadcasted_iota(jnp.int32, (blk, LANES), 1)
    low = lane < HEAD_DIM
    for j in range(ATT_BLOCKS_PER_STEP):
        rows = slice(j * blk, (j + 1) * blk)
        before = slice((j - 1) * blk, j * blk)
        bias = jnp.where(kj >= k_min, band, NEG_BIG) if j == 0 else band
        lse_tile = jnp.zeros((blk, LANES), F32)
        for hp in range(N_HEADS // 2):
            cols = slice(hp * LANES, (hp + 1) * LANES)
            q2 = q_ref[0, rows, cols]
            k_prev = kp_ref[0, :, cols] if j == 0 else kc_ref[0, before, cols]
            v_prev = vp_ref[0, :, cols] if j == 0 else vc_ref[0, before, cols]
            kk = jnp.concatenate([k_prev, kc_ref[0, rows, cols]], axis=0)
            vv = jnp.concatenate([v_prev, vc_ref[0, rows, cols]], axis=0)
            halves = []
            for hh in range(2):
                head = 2 * hp + hh
                qm = jnp.where(low if hh == 0 else jnp.logical_not(low), q2, jnp.zeros_like(q2))
                s = lax.dot_general(qm, kk, (((1,), (1,)), ((), ())),
                                    preferred_element_type=F32) + bias
                m = jnp.max(s, axis=-1, keepdims=True)
                p = jnp.exp(s - m)
                den = jnp.sum(p, axis=-1, keepdims=True)
                pv = jnp.dot(p.astype(BF16), vv, preferred_element_type=F32)
                halves.append(pv / den)
                lse_tile = jnp.where(lane == head, m + jnp.log(den), lse_tile)
            o_ref[0, rows, cols] = jnp.where(low, halves[0], halves[1]).astype(BF16)
        lse_ref[0, rows, :] = lse_tile


def _attn_pattern(qkv_sub):
    n_sub, sub_len, _ = qkv_sub.shape

    step = ATT_BLOCKS_PER_STEP * ATT_BLOCK

    def cur(which):
        return pl.BlockSpec((1, step, D_MODEL), lambda s, n: (s, n, which))

    def prev(which):
        return pl.BlockSpec((1, ATT_BLOCK, D_MODEL),
                            lambda s, n: (s, jnp.maximum(n * ATT_BLOCKS_PER_STEP - 1, 0), which))

    return pl.pallas_call(
        _attn_kernel,
        grid=(n_sub, sub_len // step),
        in_specs=[cur(0), prev(1), cur(1), prev(2), cur(2)],
        out_specs=[pl.BlockSpec((1, step, D_MODEL), lambda s, n: (s, n, 0)),
                   pl.BlockSpec((1, step, LANES), lambda s, n: (s, n, 0))],
        out_shape=[jax.ShapeDtypeStruct((n_sub, sub_len, D_MODEL), BF16),
                   jax.ShapeDtypeStruct((n_sub, sub_len, LANES), F32)],
        compiler_params=_params(2),
    )(*([qkv_sub] * 5))


def _odd_post_kernel(h_ref, o1_ref, l1_ref, o4_ref, l4_ref, o16_ref, l16_ref, expand_ref, wo_ref,
                     g_ref, b_ref, rwh_ref, rwl_ref, rb_ref, tri_ref, hn_ref, hnb_ref, idx_ref,
                     gate_ref, cnt_ref, os4_ref, ls4_ref, os16_ref, ls16_ref, seen_ref):
    rows = h_ref.shape[0]
    n_col = D_MODEL // LANES
    for dil, o_ref, l_ref, os_ref, ls_ref in ((DILATIONS[1], o4_ref, l4_ref, os4_ref, ls4_ref),
                                              (DILATIONS[2], o16_ref, l16_ref, os16_ref, ls16_ref)):
        for r in range(dil):
            sel = pl.ds(r, rows // dil, stride=dil)
            ls_ref[sel, :] = l_ref[0, r]
            for c in range(n_col):
                os_ref[c, sel, :] = o_ref[0, r, :, c * LANES:(c + 1) * LANES].astype(F32)
    lses = (l1_ref[...], ls4_ref[...], ls16_ref[...])
    outs = (o1_ref[...].astype(F32),
            jnp.concatenate([os4_ref[c] for c in range(n_col)], axis=1),
            jnp.concatenate([os16_ref[c] for c in range(n_col)], axis=1))
    mx = jnp.maximum(jnp.maximum(lses[0], lses[1]), lses[2])
    es = [jnp.exp(l - mx) for l in lses]
    den = es[0] + es[1] + es[2]
    o = jnp.zeros((rows, D_MODEL), F32)
    for e, out in zip(es, outs):
        hi, lo = _split_bf16(e / den)
        w = (jnp.dot(hi, expand_ref[...], preferred_element_type=F32)
             + jnp.dot(lo, expand_ref[...], preferred_element_type=F32))
        o = o + w * out
    mix = jnp.dot(o.astype(BF16), wo_ref[...], preferred_element_type=F32)
    _norm_and_route(DN_ALPHA * h_ref[...] + mix, g_ref, b_ref, (rwh_ref, rwl_ref, rb_ref, tri_ref),
                    hn_ref, hnb_ref, idx_ref, gate_ref, cnt_ref, seen_ref)


def _odd_post(h, pattern_outs, consts, seq_len):
    t = h.shape[0]
    tps = seq_len // ROW_TILE
    (o1, l1), (o4, l4), (o16, l16) = pattern_outs
    d4, d16 = DILATIONS[1], DILATIONS[2]
    out_specs, out_shape = _post_outputs(t)
    return pl.pallas_call(
        _odd_post_kernel,
        grid=(t // ROW_TILE,),
        in_specs=[_rows(D_MODEL), _rows(D_MODEL), _rows(LANES),
                  _residue_spec(d4, ROW_TILE, D_MODEL, tps), _residue_spec(d4, ROW_TILE, LANES, tps),
                  _residue_spec(d16, ROW_TILE, D_MODEL, tps), _residue_spec(d16, ROW_TILE, LANES, tps)]
        + [_full(a.shape) for a in consts],
        out_specs=out_specs, out_shape=out_shape,
        scratch_shapes=[pltpu.VMEM((D_MODEL // LANES, ROW_TILE, LANES), F32),
                        pltpu.VMEM((ROW_TILE, LANES), F32)] * 2 + [pltpu.VMEM((8, LANES), F32)],
        compiler_params=_params(),
    )(h, o1, l1, o4, l4, o16, l16, *consts)


def _attention_layer(h, w_qkv, w_o, cos, sin, consts, batch, seq_len):
    t = h.shape[0]
    qkv1, qkv4, qkv16 = _qkv_rope(h, w_qkv.astype(BF16), cos, sin, batch, seq_len)
    outs = []
    for dil, qkv in zip(DILATIONS, (qkv1, qkv4, qkv16)):
        sub_len = seq_len // dil
        o, lse = _attn_pattern(qkv.reshape(batch * dil, sub_len, 3 * D_MODEL))
        if dil == 1:
            outs.append((o.reshape(t, D_MODEL), lse.reshape(t, LANES)))
        else:
            outs.append((o.reshape(batch, dil, sub_len, D_MODEL),
                         lse.reshape(batch, dil, sub_len, LANES)))
    head_of_lane = jnp.arange(D_MODEL, dtype=jnp.int32) // HEAD_DIM
    expand = (jnp.arange(LANES, dtype=jnp.int32)[:, None] == head_of_lane[None, :]).astype(BF16)
    return _odd_post(h, outs, [expand, w_o.astype(BF16), *consts], seq_len)


def _moe_kernel(be_ref, nused_ref, x_ref, wgu_ref, bgu_ref, wd_ref, bd_ref, y_ref,
                wgu_bf_ref, wd_bf_ref):
    i = pl.program_id(0)
    new_expert = jnp.logical_or(i == 0, be_ref[i] != be_ref[jnp.maximum(i - 1, 0)])

    @pl.when(new_expert)
    def _():
        wgu_bf_ref[...] = wgu_ref[0, 0].astype(BF16)
        wd_bf_ref[...] = wd_ref[0, 0].astype(BF16)

    @pl.when(i < nused_ref[0])
    def _():
        hid = jnp.dot(x_ref[...], wgu_bf_ref[...], preferred_element_type=F32) + bgu_ref[0]
        gate = jnp.minimum(hid[:, :D_MODEL], SWIGLU_LIMIT)
        lin = jnp.clip(hid[:, D_MODEL:], -SWIGLU_LIMIT, SWIGLU_LIMIT)
        act = (lin + 1.0) * (gate * _sigmoid(SWIGLU_ALPHA * gate))
        y = jnp.dot(act.astype(BF16), wd_bf_ref[...], preferred_element_type=F32) + bd_ref[0]
        y_ref[...] = y.astype(BF16)

    @pl.when(i >= nused_ref[0])
    def _():
        y_ref[...] = jnp.zeros_like(y_ref)


def _moe_experts(xs, block_e, n_used, layer, w_gu, b_gu, w_down, b_down):
    n_rows = xs.shape[0]
    grid_spec = pltpu.PrefetchScalarGridSpec(
        num_scalar_prefetch=2,
        grid=(n_rows // MOE_ROWS,),
        in_specs=[pl.BlockSpec((MOE_ROWS, D_MODEL), lambda i, be, nu: (i, 0)),
                  pl.BlockSpec((1, 1, D_MODEL, 2 * D_MODEL), lambda i, be, nu: (layer, be[i], 0, 0)),
                  pl.BlockSpec((1, 1, 2 * D_MODEL), lambda i, be, nu: (be[i], 0, 0)),
                  pl.BlockSpec((1, 1, D_MODEL, D_MODEL), lambda i, be, nu: (layer, be[i], 0, 0)),
                  pl.BlockSpec((1, 1, D_MODEL), lambda i, be, nu: (be[i], 0, 0))],
        out_specs=pl.BlockSpec((MOE_ROWS, D_MODEL), lambda i, be, nu: (i, 0)),
        scratch_shapes=[pltpu.VMEM((D_MODEL, 2 * D_MODEL), BF16), pltpu.VMEM((D_MODEL, D_MODEL), BF16)],
    )
    return pl.pallas_call(
        _moe_kernel,
        grid_spec=grid_spec,
        out_shape=jax.ShapeDtypeStruct((n_rows, D_MODEL), BF16),
        compiler_params=_params(),
    )(block_e, n_used, xs, w_gu, b_gu[:, None, :], w_down, b_down[:, None, :])


def _routing_tables(idx_tile, counts, t):
    n_assign = t * TOP_K
    top_idx = idx_tile[:, :TOP_K]
    rank = idx_tile[:, TOP_K:2 * TOP_K]
    counts = counts[0, :N_EXPERTS].astype(jnp.int32)
    padded = (counts + MOE_ROWS - 1) // MOE_ROWS * MOE_ROWS
    pad_end = jnp.cumsum(padded)
    pad_start = pad_end - padded
    experts = jnp.arange(N_EXPERTS, dtype=jnp.int32)
    start_of = jnp.sum(jnp.where(top_idx[..., None] == experts, pad_start, 0), axis=-1)
    dest = (start_of + rank).T.reshape(-1)
    n_blocks = n_assign // MOE_ROWS + N_EXPERTS
    block_start = jnp.arange(n_blocks, dtype=jnp.int32) * MOE_ROWS
    block_e = jnp.minimum(jnp.sum((block_start[:, None] >= pad_end[None, :]).astype(jnp.int32), axis=1),
                          N_EXPERTS - 1)
    n_used = (pad_end[-1:] // MOE_ROWS).astype(jnp.int32)
    flat_tok = jnp.arange(n_assign, dtype=jnp.int32) % t
    _, tok_sorted = lax.sort_key_val(dest, flat_tok)
    starts = jnp.cumsum(counts) - counts
    rows = jnp.arange(n_blocks * MOE_ROWS, dtype=jnp.int32).reshape(n_blocks, MOE_ROWS)
    shift = (starts - pad_start)[block_e][:, None]
    live_end = (pad_start + counts)[block_e][:, None]
    compact = jnp.clip(rows + shift, 0, n_assign - 1)
    row_tok = jnp.where(rows < live_end, tok_sorted.at[compact].get(mode="promise_in_bounds"),
                        rows % t).reshape(-1)
    return row_tok, dest, block_e, n_used


def _ffn_ln_kernel(h_ref, y4_ref, gate_ref, g_ref, b_ref, o_ref):
    gates = gate_ref[...]
    acc = DN_ALPHA * h_ref[...]
    for k in range(TOP_K):
        acc = acc + gates[:, k:k + 1] * y4_ref[k].astype(F32)
    o_ref[...] = _layer_norm(acc, g_ref[...], b_ref[...])


def _ffn_ln(h, y4, gates, g, b):
    t = h.shape[0]
    return pl.pallas_call(
        _ffn_ln_kernel,
        grid=(t // ROW_TILE,),
        in_specs=[_rows(D_MODEL), pl.BlockSpec((TOP_K, ROW_TILE, D_MODEL), lambda i: (0, i, 0)),
                  _rows(LANES), _full(g.shape), _full(b.shape)],
        out_specs=_rows(D_MODEL),
        out_shape=jax.ShapeDtypeStruct((t, D_MODEL), F32),
        compiler_params=_params(),
    )(h, y4, gates, g, b)


def _moe_layer(hn, hn_bf, idx_tile, gate_tile, counts, layer, w_gu, b_gu, w_down, b_down, ln_g, ln_b):
    t = hn.shape[0]
    row_tok, dest, block_e, n_used = _routing_tables(idx_tile, counts, t)
    xs = hn_bf.at[row_tok].get(mode="promise_in_bounds")
    ys = _moe_experts(xs, block_e, n_used, layer, w_gu, b_gu, w_down, b_down)
    y4 = ys.at[dest].get(mode="promise_in_bounds").reshape(TOP_K, t, D_MODEL)
    return _ffn_ln(hn, y4, gate_tile, ln_g[None, :], ln_b[None, :])


def _router_consts(router_w, router_b):
    w = jnp.zeros((D_MODEL, LANES), F32).at[:, :N_EXPERTS].set(router_w)
    w_hi, w_lo = _split_bf16(w)
    b = jnp.full((1, LANES), NEG_BIG, F32).at[0, :N_EXPERTS].set(router_b)
    r = jnp.arange(ROW_TILE, dtype=jnp.int32)
    tri = (r[None, :] < r[:, None]).astype(BF16)
    return w_hi, w_lo, b, tri


def kernel(x, hy_w_in, conv_w, ssm_a_re, ssm_a_im, ssm_log_dt, ssm_b_re, ssm_b_im, ssm_c_re,
           ssm_c_im, ssm_d, ssm_w_glu, ssm_b_glu, hy_w_out, att_w_qkv, att_w_o, ln_mix_g,
           ln_mix_b, ln_ffn_g, ln_ffn_b, router_w, router_b, expert_w_gu, expert_b_gu,
           expert_w_down, expert_b_down):
    batch, seq_len, _ = x.shape
    t = batch * seq_len
    h = x.reshape(t, D_MODEL)
    cos, sin = _rope_tables(seq_len)
    for layer in range(DEPTH):
        i = layer // 2
        consts = [ln_mix_g[layer][None, :], ln_mix_b[layer][None, :],
                  *_router_consts(router_w[layer], router_b[layer])]
        if layer % 2 == 0:
            y_conv, u, u_sg = _inproj_conv(h, hy_w_in[i].astype(BF16), conv_w[i], seq_len)
            tables = _s5_tables(ssm_a_re[i], ssm_a_im[i], ssm_log_dt[i], ssm_b_re[i], ssm_b_im[i],
                                ssm_c_re[i], ssm_c_im[i])
            y_sg = _s5_scan(u_sg, tables, batch, seq_len // SSM_CHUNK)
            w_out = hy_w_out[i].astype(BF16)
            hn, hn_bf, idx_tile, gate_tile, counts = _even_post(
                h, y_conv, y_sg, u,
                [ssm_d[i].reshape(1, SSM_WIDTH), ssm_w_glu[i].astype(BF16), ssm_b_glu[i][None, :],
                 w_out[:CONV_CH], w_out[CONV_CH:], *consts])
        else:
            hn, hn_bf, idx_tile, gate_tile, counts = _attention_layer(
                h, att_w_qkv[i], att_w_o[i], cos, sin, consts, batch, seq_len)
        h = _moe_layer(hn, hn_bf, idx_tile, gate_tile, counts, layer, expert_w_gu, expert_b_gu[layer],
                       expert_w_down, expert_b_down[layer], ln_ffn_g[layer], ln_ffn_b[layer])
    return h.reshape(batch, seq_len, D_MODEL)
```

```python
import functools

import jax
import jax.numpy as jnp
from jax import lax
from jax.experimental import pallas as pl
from jax.experimental.pallas import tpu as pltpu

F32 = jnp.float32
BF16 = jnp.bfloat16

D_MODEL = 1024
DEPTH = 4
CONV_CH = 512
SSM_WIDTH = 512
SSM_GROUP = 16
SSM_GROUPS = 32
SSM_STATE = 64
N_HEADS = 16
HEAD_DIM = 64
ROPE_THETA = 10000.0
DILATIONS = (1, 4, 16)
ATT_BLOCK = 128
ATT_BLOCKS_PER_STEP = 4
N_EXPERTS = 32
TOP_K = 4
SWIGLU_LIMIT = 7.0
SWIGLU_ALPHA = 1.702
DN_ALPHA = (2 * DEPTH) ** 0.25
LN_EPS = 1e-5

LANES = 128
ROW_TILE = 512
QKV_TILE = 256
SSM_CHUNK = 8
MOE_ROWS = 512
VMEM_LIMIT = 56 * 1024 * 1024
NEG_BIG = -1e30


def _params(n_axes=1):
    return pltpu.CompilerParams(dimension_semantics=("arbitrary",) * n_axes,
                                vmem_limit_bytes=VMEM_LIMIT)


def _full(shape):
    return pl.BlockSpec(shape, lambda *_: (0,) * len(shape))


def _rows(width, tile=ROW_TILE):
    return pl.BlockSpec((tile, width), lambda i: (i, 0))


def _layer_norm(x, g, b):
    mu = jnp.mean(x, axis=-1, keepdims=True)
    xc = x - mu
    var = jnp.mean(xc * xc, axis=-1, keepdims=True)
    return xc * lax.rsqrt(var + LN_EPS) * g + b


def _sigmoid(x):
    return 1.0 / (1.0 + jnp.exp(-x))


def _split_bf16(x):
    hi = x.astype(BF16)
    return hi, (x - hi.astype(F32)).astype(BF16)


def _route(hn, rw_hi_ref, rw_lo_ref, rb_ref, tri_ref, idx_ref, gate_ref, cnt_ref, seen_ref):
    @pl.when(pl.program_id(0) == 0)
    def _():
        seen_ref[...] = jnp.zeros_like(seen_ref)

    hi, lo = _split_bf16(hn)
    logits = (jnp.dot(hi, rw_hi_ref[...], preferred_element_type=F32)
              + jnp.dot(hi, rw_lo_ref[...], preferred_element_type=F32)
              + jnp.dot(lo, rw_hi_ref[...], preferred_element_type=F32)
              + rb_ref[...])
    lane = lax.broadcasted_iota(jnp.int32, logits.shape, 1)
    lane_f = lane.astype(F32)
    idx_f = jnp.zeros(logits.shape, F32)
    val_tile = jnp.zeros(logits.shape, F32)
    top0 = None
    den = None
    firsts = []
    for k in range(TOP_K):
        mx = jnp.max(logits, axis=-1, keepdims=True)
        first = jnp.min(jnp.where(logits == mx, lane_f, float(LANES)), axis=-1, keepdims=True)
        if k == 0:
            top0 = mx
        e = jnp.exp(mx - top0)
        den = e if k == 0 else den + e
        idx_f = jnp.where(lane == k, first, idx_f)
        val_tile = jnp.where(lane == k, e, val_tile)
        logits = jnp.where(lane_f == first, -jnp.inf, logits)
        firsts.append(first)
    gate_ref[...] = val_tile / den
    picked = jnp.where(logits == -jnp.inf, 1.0, 0.0)
    seen = seen_ref[0:1, :]
    before = jnp.dot(tri_ref[...], picked.astype(BF16), preferred_element_type=F32) + seen
    for k, first in enumerate(firsts):
        rank = jnp.sum(jnp.where(lane_f == first, before, 0.0), axis=-1, keepdims=True)
        idx_f = jnp.where(lane == TOP_K + k, rank, idx_f)
    idx_ref[...] = idx_f.astype(jnp.int32)
    seen = seen + jnp.sum(picked, axis=0, keepdims=True)
    seen_ref[...] = jnp.broadcast_to(seen, seen_ref.shape)
    cnt_ref[...] = jnp.broadcast_to(seen, cnt_ref.shape)


def _post_outputs(t):
    specs = [_rows(D_MODEL), _rows(D_MODEL), _rows(LANES), _rows(LANES), _full((8, LANES))]
    shapes = [jax.ShapeDtypeStruct((t, D_MODEL), F32), jax.ShapeDtypeStruct((t, D_MODEL), BF16),
              jax.ShapeDtypeStruct((t, LANES), jnp.int32), jax.ShapeDtypeStruct((t, LANES), F32),
              jax.ShapeDtypeStruct((8, LANES), F32)]
    return specs, shapes


def _norm_and_route(pre, g_ref, b_ref, route_refs, hn_ref, hnb_ref, idx_ref, gate_ref, cnt_ref,
                    seen_ref):
    hn = _layer_norm(pre, g_ref[...], b_ref[...])
    hn_ref[...] = hn
    hnb_ref[...] = hn.astype(BF16)
    _route(hn, *route_refs, idx_ref, gate_ref, cnt_ref, seen_ref)


def _chunk_perm(rows):
    j = jnp.arange(rows, dtype=jnp.int32)
    n = rows // SSM_CHUNK
    src = (j % n) * SSM_CHUNK + j // n
    return (src[:, None] == j[None, :]).astype(BF16)


def _inproj_conv_kernel(x_ref, w_ref, cw_ref, perm_ref, yconv_ref, u_ref, usg_ref, carry_ref, *,
                        tiles_per_seq):
    @pl.when(pl.program_id(0) % tiles_per_seq == 0)
    def _():
        carry_ref[...] = jnp.zeros_like(carry_ref)

    proj = jnp.dot(x_ref[...].astype(BF16), w_ref[...], preferred_element_type=F32)
    gate_b = proj[:, :CONV_CH]
    gate_c = proj[:, CONV_CH:2 * CONV_CH]
    hid = proj[:, 2 * CONV_CH:3 * CONV_CH]
    v = gate_c * hid
    rows = v.shape[0]
    row = lax.broadcasted_iota(jnp.int32, v.shape, 0)
    prev1 = carry_ref[7:8, :]
    prev2 = carry_ref[6:7, :]
    vm1 = jnp.where(row == 0, prev1, pltpu.roll(v, 1, 0))
    vm2 = jnp.where(row == 0, prev2, jnp.where(row == 1, prev1, pltpu.roll(v, 2, 0)))
    conv = cw_ref[0:1, :] * vm2 + cw_ref[1:2, :] * vm1 + cw_ref[2:3, :] * v
    yconv_ref[...] = gate_b * conv
    u = proj[:, 3 * CONV_CH:]
    u_ref[...] = u
    carry_ref[...] = v[rows - 8:, :]
    by_pos = jnp.dot(perm_ref[...], u.astype(BF16), preferred_element_type=F32).astype(BF16)
    n = rows // SSM_CHUNK
    for sg in range(SSM_WIDTH // LANES):
        for s in range(SSM_CHUNK):
            usg_ref[sg, :, s * LANES:(s + 1) * LANES] = by_pos[s * n:(s + 1) * n,
                                                               sg * LANES:(sg + 1) * LANES]


def _inproj_conv(h, w_in_bf, conv_w, seq_len):
    t = h.shape[0]
    n_sg = SSM_WIDTH // LANES
    perm = _chunk_perm(ROW_TILE)
    return pl.pallas_call(
        functools.partial(_inproj_conv_kernel, tiles_per_seq=seq_len // ROW_TILE),
        grid=(t // ROW_TILE,),
        in_specs=[_rows(D_MODEL), _full(w_in_bf.shape), _full(conv_w.shape), _full(perm.shape)],
        out_specs=[_rows(CONV_CH), _rows(SSM_WIDTH),
                   pl.BlockSpec((n_sg, ROW_TILE // SSM_CHUNK, SSM_CHUNK * LANES), lambda i: (0, i, 0))],
        out_shape=[jax.ShapeDtypeStruct((t, CONV_CH), F32),
                   jax.ShapeDtypeStruct((t, SSM_WIDTH), F32),
                   jax.ShapeDtypeStruct((n_sg, t // SSM_CHUNK, SSM_CHUNK * LANES), BF16)],
        scratch_shapes=[pltpu.VMEM((8, CONV_CH), F32)],
        compiler_params=_params(),
    )(h, w_in_bf, conv_w, perm)


def _s5_tables(a_re, a_im, log_dt, b_re, b_im, c_re, c_im):
    q = SSM_CHUNK
    per_sg = LANES // SSM_GROUP
    n_sg = SSM_GROUPS // per_sg
    lam_re = jnp.minimum(a_re, -1e-4)
    lam_im = a_im
    dt = jnp.exp(log_dt)[:, None]
    mag = jnp.exp(lam_re * dt)
    ab_re = mag * jnp.cos(lam_im * dt)
    ab_im = mag * jnp.sin(lam_im * dt)
    nr, ni = ab_re - 1.0, ab_im
    den = lam_re * lam_re + lam_im * lam_im
    coef_re = ((nr * lam_re + ni * lam_im) / den)[..., None]
    coef_im = ((ni * lam_re - nr * lam_im) / den)[..., None]
    bb_re = coef_re * b_re - coef_im * b_im
    bb_im = coef_re * b_im + coef_im * b_re
    j = jnp.arange(q + 1, dtype=F32)
    pmag = jnp.exp((lam_re * dt)[..., None] * j)
    pang = (lam_im * dt)[..., None] * j
    pw_re = pmag * jnp.cos(pang)
    pw_im = pmag * jnp.sin(pang)
    ca_re = c_re[..., None] * pw_re[:, None] - c_im[..., None] * pw_im[:, None]
    ca_im = c_re[..., None] * pw_im[:, None] + c_im[..., None] * pw_re[:, None]
    kern = (jnp.einsum("gapj,gph->gjah", ca_re, bb_re, precision="highest")
            - jnp.einsum("gapj,gph->gjah", ca_im, bb_im, precision="highest"))[:, :q]
    lag = jnp.arange(q)[None, :] - jnp.arange(q)[:, None]
    toep = kern[:, jnp.clip(lag, 0, q - 1)]
    toep = jnp.where((lag >= 0)[None, :, :, None, None], toep, 0.0).transpose(0, 1, 4, 2, 3)
    rev = q - 1 - jnp.arange(q)
    pr = pw_re[:, :, rev]
    pi = pw_im[:, :, rev]
    inj_re = pr[..., None] * bb_re[:, :, None] - pi[..., None] * bb_im[:, :, None]
    inj_im = pr[..., None] * bb_im[:, :, None] + pi[..., None] * bb_re[:, :, None]
    inj_re = inj_re.transpose(0, 2, 3, 1)
    inj_im = inj_im.transpose(0, 2, 3, 1)
    out_re = ca_re[..., 1:].transpose(0, 2, 3, 1)
    out_im = (-ca_im[..., 1:]).transpose(0, 2, 3, 1)
    eye = jnp.eye(per_sg, dtype=F32)
    width = q * LANES
    n_state = per_sg * SSM_STATE

    def sg(a):
        return a.reshape((n_sg, per_sg) + a.shape[1:])

    toep_sg = jnp.einsum("Scshtk,cd->Sschtdk", sg(toep), eye).reshape(n_sg, width, width)
    inj_sg = jnp.concatenate(
        [jnp.einsum("Scshp,cd->Sschdp", sg(a), eye).reshape(n_sg, width, n_state)
         for a in (inj_re, inj_im)], axis=2)
    out_sg = jnp.concatenate(
        [jnp.einsum("Scpth,cd->Scptdh", sg(a), eye).reshape(n_sg, n_state, width)
         for a in (out_re, out_im)], axis=1)
    aq = jnp.stack([pw_re[..., q].reshape(n_sg, n_state), pw_im[..., q].reshape(n_sg, n_state)], axis=1)
    return toep_sg.astype(BF16), inj_sg.astype(BF16), out_sg.astype(BF16), aq


def _s5_kernel(u_ref, toep_ref, inj_ref, out_ref, aq_ref, y_ref, inj_scr, xs_scr, *, n_chunks):
    n_state = aq_ref.shape[3]
    u = u_ref[0]
    inj_scr[...] = jnp.dot(u, inj_ref[0, 0], preferred_element_type=F32)
    aq_re = jnp.broadcast_to(aq_ref[0, 0, 0:1, :], (8, n_state))
    aq_im = jnp.broadcast_to(aq_ref[0, 0, 1:2, :], (8, n_state))
    sub = lax.broadcasted_iota(jnp.int32, (8, n_state), 0)

    def block(blk, carry):
        re, im = carry
        start = pl.multiple_of(blk * 8, 8)
        inj = inj_scr[pl.ds(start, 8), :]
        xs_re = jnp.zeros((8, n_state), F32)
        xs_im = jnp.zeros((8, n_state), F32)
        for j in range(8):
            xs_re = jnp.where(sub == j, re, xs_re)
            xs_im = jnp.where(sub == j, im, xs_im)
            in_re = jnp.broadcast_to(inj[j:j + 1, :n_state], (8, n_state))
            in_im = jnp.broadcast_to(inj[j:j + 1, n_state:], (8, n_state))
            re, im = aq_re * re - aq_im * im + in_re, aq_re * im + aq_im * re + in_im
        xs_scr[pl.ds(start, 8), :n_state] = xs_re
        xs_scr[pl.ds(start, 8), n_state:] = xs_im
        return re, im

    zero = jnp.zeros((8, n_state), F32)
    lax.fori_loop(0, n_chunks // 8, block, (zero, zero))
    y = jnp.dot(u, toep_ref[0, 0], preferred_element_type=F32)
    y_ref[0] = y + jnp.dot(xs_scr[...].astype(BF16), out_ref[0, 0], preferred_element_type=F32)


def _s5_scan(u_sg, tables, layer, batch, n_chunks):
    toep, inj, out, aq = tables
    n_sg, _, width = u_sg.shape
    n_state2 = inj.shape[3]

    def per_sg(shape):
        return pl.BlockSpec((1, 1) + shape, lambda g, b: (layer, g, 0, 0))

    return pl.pallas_call(
        functools.partial(_s5_kernel, n_chunks=n_chunks),
        grid=(n_sg, batch),
        in_specs=[pl.BlockSpec((1, n_chunks, width), lambda g, b: (g, b, 0)),
                  per_sg((width, width)), per_sg((width, n_state2)), per_sg((n_state2, width)),
                  per_sg((2, n_state2 // 2))],
        out_specs=pl.BlockSpec((1, n_chunks, width), lambda g, b: (g, b, 0)),
        out_shape=jax.ShapeDtypeStruct(u_sg.shape, F32),
        scratch_shapes=[pltpu.VMEM((n_chunks, n_state2), F32)] * 2,
        compiler_params=_params(2),
    )(u_sg, toep, inj, out, aq)


def _even_post_kernel(h_ref, yc_ref, ysg_ref, u_ref, perm_ref, d_ref, wglu_ref, bglu_ref, woc_ref,
                      wos_ref, g_ref, b_ref, rwh_ref, rwl_ref, rb_ref, tri_ref, *out_and_scratch):
    n_sg = ysg_ref.shape[0]
    by_pos = jnp.concatenate(
        [jnp.concatenate([ysg_ref[sg, :, s * LANES:(s + 1) * LANES] for sg in range(n_sg)], axis=1)
         for s in range(SSM_CHUNK)], axis=0)
    hi, lo = _split_bf16(by_pos)
    y_scan = (jnp.dot(perm_ref[...], hi, preferred_element_type=F32)
              + jnp.dot(perm_ref[...], lo, preferred_element_type=F32))
    y = y_scan + d_ref[...] * u_ref[...]
    z = jax.nn.gelu(y)
    glu = jnp.dot(z.astype(BF16), wglu_ref[...], preferred_element_type=F32) + bglu_ref[...]
    z = z * _sigmoid(glu)
    mix = (jnp.dot(yc_ref[...].astype(BF16), woc_ref[...], preferred_element_type=F32)
           + jnp.dot(z.astype(BF16), wos_ref[...], preferred_element_type=F32))
    _norm_and_route(DN_ALPHA * h_ref[...] + mix, g_ref, b_ref, (rwh_ref, rwl_ref, rb_ref, tri_ref),
                    *out_and_scratch)


def _even_post(h, y_conv, y_sg, u, consts):
    t = h.shape[0]
    n_sg = y_sg.shape[0]
    out_specs, out_shape = _post_outputs(t)
    consts = [_chunk_perm(ROW_TILE).T, *consts]
    return pl.pallas_call(
        _even_post_kernel,
        grid=(t // ROW_TILE,),
        in_specs=[_rows(D_MODEL), _rows(CONV_CH),
                  pl.BlockSpec((n_sg, ROW_TILE // SSM_CHUNK, SSM_CHUNK * LANES), lambda i: (0, i, 0)),
                  _rows(SSM_WIDTH)] + [_full(a.shape) for a in consts],
        out_specs=out_specs, out_shape=out_shape,
        scratch_shapes=[pltpu.VMEM((8, LANES), F32)],
        compiler_params=_params(),
    )(h, y_conv, y_sg, u, *consts)


def _qkv_rope_kernel(x_ref, w_ref, cos_ref, sin_ref, p4_ref, p16_ref, o1_ref, o4_ref, o16_ref):
    qkv = jnp.dot(x_ref[...].astype(BF16), w_ref[...], preferred_element_type=F32)
    rows = qkv.shape[0]
    cos = jnp.concatenate([cos_ref[...]] * (D_MODEL // LANES), axis=1)
    sin = jnp.concatenate([sin_ref[...]] * (D_MODEL // LANES), axis=1)
    lane = lax.broadcasted_iota(jnp.int32, (rows, D_MODEL), 1)
    low_half = (lane % HEAD_DIM) < (HEAD_DIM // 2)

    def rope(xs):
        up = pltpu.roll(xs, D_MODEL - HEAD_DIM // 2, 1)
        down = pltpu.roll(xs, HEAD_DIM // 2, 1)
        return xs * cos + jnp.where(low_half, up, down) * sin

    parts = (rope(qkv[:, :D_MODEL]) * (HEAD_DIM ** -0.5), rope(qkv[:, D_MODEL:2 * D_MODEL]),
             qkv[:, 2 * D_MODEL:])
    for which, part in enumerate(parts):
        cols = slice(which * D_MODEL, (which + 1) * D_MODEL)
        part = part.astype(BF16)
        o1_ref[:, cols] = part
        for dil, perm_ref, ref in ((DILATIONS[1], p4_ref, o4_ref), (DILATIONS[2], p16_ref, o16_ref)):
            split = jnp.dot(perm_ref[...], part, preferred_element_type=F32).astype(BF16)
            n = rows // dil
            for r in range(dil):
                ref[0, r, :, cols] = split[r * n:(r + 1) * n, :]


def _rope_tables(seq_len):
    half = HEAD_DIM // 2
    inv = ROPE_THETA ** (-jnp.arange(half, dtype=F32) / half)
    ang = jnp.arange(seq_len, dtype=F32)[:, None] * inv[None, :]
    cos = jnp.tile(jnp.cos(ang), (1, LANES // half))
    sin = jnp.sin(ang)
    sin = jnp.tile(jnp.concatenate([-sin, sin], axis=1), (1, LANES // HEAD_DIM))
    return cos, sin


def _residue_spec(dil, rows, width, tiles_per_seq):
    return pl.BlockSpec((1, dil, rows // dil, width),
                        lambda i: (i // tiles_per_seq, 0, i % tiles_per_seq, 0))


def _qkv_rope(h, w_qkv_bf, cos, sin, batch, seq_len):
    t = h.shape[0]
    tps = seq_len // QKV_TILE
    d4, d16 = DILATIONS[1], DILATIONS[2]
    width = 3 * D_MODEL
    perms = []
    for dil in (d4, d16):
        j = jnp.arange(QKV_TILE, dtype=jnp.int32)
        src = (j % (QKV_TILE // dil)) * dil + j // (QKV_TILE // dil)
        perms.append((src[:, None] == j[None, :]).astype(BF16))
    return pl.pallas_call(
        _qkv_rope_kernel,
        grid=(t // QKV_TILE,),
        in_specs=[_rows(D_MODEL, QKV_TILE), _full(w_qkv_bf.shape),
                  pl.BlockSpec((QKV_TILE, LANES), lambda i: (i % tps, 0)),
                  pl.BlockSpec((QKV_TILE, LANES), lambda i: (i % tps, 0)),
                  _full((QKV_TILE, QKV_TILE)), _full((QKV_TILE, QKV_TILE))],
        out_specs=[_rows(width, QKV_TILE), _residue_spec(d4, QKV_TILE, width, tps),
                   _residue_spec(d16, QKV_TILE, width, tps)],
        out_shape=[jax.ShapeDtypeStruct((t, width), BF16),
                   jax.ShapeDtypeStruct((batch, d4, seq_len // d4, width), BF16),
                   jax.ShapeDtypeStruct((batch, d16, seq_len // d16, width), BF16)],
        compiler_params=_params(),
    )(h, w_qkv_bf, cos, sin, *perms)


def _attn_kernel(q_ref, kp_ref, kc_ref, vp_ref, vc_ref, o_ref, lse_ref):
    blk = ATT_BLOCK
    qi = lax.broadcasted_iota(jnp.int32, (blk, 2 * blk), 0)
    kj = lax.broadcasted_iota(jnp.int32, (blk, 2 * blk), 1)
    dist = blk + qi - kj
    band = jnp.where(dist >= 0, jnp.where(dist <= blk, 0.0, NEG_BIG), NEG_BIG)
    k_min = jnp.where(pl.program_id(1) == 0, blk, 0)
    lane = lax.broadcasted_iota(jnp.int32, (blk, LANES), 1)
    low = lane < HEAD_DIM
    for j in range(ATT_BLOCKS_PER_STEP):
        rows = slice(j * blk, (j + 1) * blk)
        before = slice((j - 1) * blk, j * blk)
        bias = jnp.where(kj >= k_min, band, NEG_BIG) if j == 0 else band
        lse_tile = jnp.zeros((blk, LANES), F32)
        for hp in range(N_HEADS // 2):
            cols = slice(hp * LANES, (hp + 1) * LANES)
            q2 = q_ref[0, rows, cols]
            k_prev = kp_ref[0, :, cols] if j == 0 else kc_ref[0, before, cols]
            v_prev = vp_ref[0, :, cols] if j == 0 else vc_ref[0, before, cols]
            kk = jnp.concatenate([k_prev, kc_ref[0, rows, cols]], axis=0)
            vv = jnp.concatenate([v_prev, vc_ref[0, rows, cols]], axis=0)
            halves = []
            for hh in range(2):
                head = 2 * hp + hh
                qm = jnp.where(low if hh == 0 else jnp.logical_not(low), q2, jnp.zeros_like(q2))
                s = lax.dot_general(qm, kk, (((1,), (1,)), ((), ())),
                                    preferred_element_type=F32) + bias
                m = jnp.max(s, axis=-1, keepdims=True)
                p = jnp.exp(s - m)
                den = jnp.sum(p, axis=-1, keepdims=True)
                pv = jnp.dot(p.astype(BF16), vv, preferred_element_type=F32)
                halves.append(pv / den)
                lse_tile = jnp.where(lane == head, m + jnp.log(den), lse_tile)
            o_ref[0, rows, cols] = jnp.where(low, halves[0], halves[1]).astype(BF16)
        lse_ref[0, rows, :] = lse_tile


def _attn_pattern(qkv_sub):
    n_sub, sub_len, _ = qkv_sub.shape

    step = ATT_BLOCKS_PER_STEP * ATT_BLOCK

    def cur(which):
        return pl.BlockSpec((1, step, D_MODEL), lambda s, n: (s, n, which))

    def prev(which):
        return pl.BlockSpec((1, ATT_BLOCK, D_MODEL),
                            lambda s, n: (s, jnp.maximum(n * ATT_BLOCKS_PER_STEP - 1, 0), which))

    return pl.pallas_call(
        _attn_kernel,
        grid=(n_sub, sub_len // step),
        in_specs=[cur(0), prev(1), cur(1), prev(2), cur(2)],
        out_specs=[pl.BlockSpec((1, step, D_MODEL), lambda s, n: (s, n, 0)),
                   pl.BlockSpec((1, step, LANES), lambda s, n: (s, n, 0))],
        out_shape=[jax.ShapeDtypeStruct((n_sub, sub_len, D_MODEL), BF16),
                   jax.ShapeDtypeStruct((n_sub, sub_len, LANES), F32)],
        compiler_params=_params(2),
    )(*([qkv_sub] * 5))


def _odd_post_kernel(h_ref, o1_ref, l1_ref, o4_ref, l4_ref, o16_ref, l16_ref, expand_ref, wo_ref,
                     g_ref, b_ref, rwh_ref, rwl_ref, rb_ref, tri_ref, hn_ref, hnb_ref, idx_ref,
                     gate_ref, cnt_ref, os4_ref, ls4_ref, os16_ref, ls16_ref, seen_ref):
    rows = h_ref.shape[0]
    n_col = D_MODEL // LANES
    for dil, o_ref, l_ref, os_ref, ls_ref in ((DILATIONS[1], o4_ref, l4_ref, os4_ref, ls4_ref),
                                              (DILATIONS[2], o16_ref, l16_ref, os16_ref, ls16_ref)):
        for r in range(dil):
            sel = pl.ds(r, rows // dil, stride=dil)
            ls_ref[sel, :] = l_ref[0, r]
            for c in range(n_col):
                os_ref[c, sel, :] = o_ref[0, r, :, c * LANES:(c + 1) * LANES].astype(F32)
    lses = (l1_ref[...], ls4_ref[...], ls16_ref[...])
    outs = (o1_ref[...].astype(F32),
            jnp.concatenate([os4_ref[c] for c in range(n_col)], axis=1),
            jnp.concatenate([os16_ref[c] for c in range(n_col)], axis=1))
    mx = jnp.maximum(jnp.maximum(lses[0], lses[1]), lses[2])
    es = [jnp.exp(l - mx) for l in lses]
    den = es[0] + es[1] + es[2]
    o = jnp.zeros((rows, D_MODEL), F32)
    for e, out in zip(es, outs):
        hi, lo = _split_bf16(e / den)
        w = (jnp.dot(hi, expand_ref[...], preferred_element_type=F32)
             + jnp.dot(lo, expand_ref[...], preferred_element_type=F32))
        o = o + w * out
    mix = jnp.dot(o.astype(BF16), wo_ref[...], preferred_element_type=F32)
    _norm_and_route(DN_ALPHA * h_ref[...] + mix, g_ref, b_ref, (rwh_ref, rwl_ref, rb_ref, tri_ref),
                    hn_ref, hnb_ref, idx_ref, gate_ref, cnt_ref, seen_ref)


def _odd_post(h, pattern_outs, consts, seq_len):
    t = h.shape[0]
    tps = seq_len // ROW_TILE
    (o1, l1), (o4, l4), (o16, l16) = pattern_outs
    d4, d16 = DILATIONS[1], DILATIONS[2]
    out_specs, out_shape = _post_outputs(t)
    return pl.pallas_call(
        _odd_post_kernel,
        grid=(t // ROW_TILE,),
        in_specs=[_rows(D_MODEL), _rows(D_MODEL), _rows(LANES),
                  _residue_spec(d4, ROW_TILE, D_MODEL, tps), _residue_spec(d4, ROW_TILE, LANES, tps),
                  _residue_spec(d16, ROW_TILE, D_MODEL, tps), _residue_spec(d16, ROW_TILE, LANES, tps)]
        + [_full(a.shape) for a in consts],
        out_specs=out_specs, out_shape=out_shape,
        scratch_shapes=[pltpu.VMEM((D_MODEL // LANES, ROW_TILE, LANES), F32),
                        pltpu.VMEM((ROW_TILE, LANES), F32)] * 2 + [pltpu.VMEM((8, LANES), F32)],
        compiler_params=_params(),
    )(h, o1, l1, o4, l4, o16, l16, *consts)


def _attention_layer(h, w_qkv, w_o, cos, sin, consts, batch, seq_len):
    t = h.shape[0]
    qkv1, qkv4, qkv16 = _qkv_rope(h, w_qkv.astype(BF16), cos, sin, batch, seq_len)
    outs = []
    for dil, qkv in zip(DILATIONS, (qkv1, qkv4, qkv16)):
        sub_len = seq_len // dil
        o, lse = _attn_pattern(qkv.reshape(batch * dil, sub_len, 3 * D_MODEL))
        if dil == 1:
            outs.append((o.reshape(t, D_MODEL), lse.reshape(t, LANES)))
        else:
            outs.append((o.reshape(batch, dil, sub_len, D_MODEL),
                         lse.reshape(batch, dil, sub_len, LANES)))
    head_of_lane = jnp.arange(D_MODEL, dtype=jnp.int32) // HEAD_DIM
    expand = (jnp.arange(LANES, dtype=jnp.int32)[:, None] == head_of_lane[None, :]).astype(BF16)
    return _odd_post(h, outs, [expand, w_o.astype(BF16), *consts], seq_len)


def _moe_kernel(be_ref, nused_ref, x_ref, wgu_ref, bgu_ref, wd_ref, bd_ref, y_ref,
                wgu_bf_ref, wd_bf_ref):
    i = pl.program_id(0)
    new_expert = jnp.logical_or(i == 0, be_ref[i] != be_ref[jnp.maximum(i - 1, 0)])

    @pl.when(new_expert)
    def _():
        wgu_bf_ref[...] = wgu_ref[0, 0].astype(BF16)
        wd_bf_ref[...] = wd_ref[0, 0].astype(BF16)

    @pl.when(i < nused_ref[0])
    def _():
        hid = jnp.dot(x_ref[...], wgu_bf_ref[...], preferred_element_type=F32) + bgu_ref[0]
        gate = jnp.minimum(hid[:, :D_MODEL], SWIGLU_LIMIT)
        lin = jnp.clip(hid[:, D_MODEL:], -SWIGLU_LIMIT, SWIGLU_LIMIT)
        act = (lin + 1.0) * (gate * _sigmoid(SWIGLU_ALPHA * gate))
        y = jnp.dot(act.astype(BF16), wd_bf_ref[...], preferred_element_type=F32) + bd_ref[0]
        y_ref[...] = y.astype(BF16)

    @pl.when(i >= nused_ref[0])
    def _():
        y_ref[...] = jnp.zeros_like(y_ref)


def _moe_experts(xs, block_e, n_used, layer, w_gu, b_gu, w_down, b_down):
    n_rows = xs.shape[0]
    grid_spec = pltpu.PrefetchScalarGridSpec(
        num_scalar_prefetch=2,
        grid=(n_rows // MOE_ROWS,),
        in_specs=[pl.BlockSpec((MOE_ROWS, D_MODEL), lambda i, be, nu: (i, 0)),
                  pl.BlockSpec((1, 1, D_MODEL, 2 * D_MODEL), lambda i, be, nu: (layer, be[i], 0, 0)),
                  pl.BlockSpec((1, 1, 2 * D_MODEL), lambda i, be, nu: (be[i], 0, 0)),
                  pl.BlockSpec((1, 1, D_MODEL, D_MODEL), lambda i, be, nu: (layer, be[i], 0, 0)),
                  pl.BlockSpec((1, 1, D_MODEL), lambda i, be, nu: (be[i], 0, 0))],
        out_specs=pl.BlockSpec((MOE_ROWS, D_MODEL), lambda i, be, nu: (i, 0)),
        scratch_shapes=[pltpu.VMEM((D_MODEL, 2 * D_MODEL), BF16), pltpu.VMEM((D_MODEL, D_MODEL), BF16)],
    )
    return pl.pallas_call(
        _moe_kernel,
        grid_spec=grid_spec,
        out_shape=jax.ShapeDtypeStruct((n_rows, D_MODEL), BF16),
        compiler_params=_params(),
    )(block_e, n_used, xs, w_gu, b_gu[:, None, :], w_down, b_down[:, None, :])


def _routing_tables(idx_tile, counts, t):
    n_assign = t * TOP_K
    top_idx = idx_tile[:, :TOP_K]
    rank = idx_tile[:, TOP_K:2 * TOP_K]
    counts = counts[0, :N_EXPERTS].astype(jnp.int32)
    padded = (counts + MOE_ROWS - 1) // MOE_ROWS * MOE_ROWS
    pad_end = jnp.cumsum(padded)
    pad_start = pad_end - padded
    experts = jnp.arange(N_EXPERTS, dtype=jnp.int32)
    start_of = jnp.sum(jnp.where(top_idx[..., None] == experts, pad_start, 0), axis=-1)
    dest = (start_of + rank).T.reshape(-1)
    n_blocks = n_assign // MOE_ROWS + N_EXPERTS
    block_start = jnp.arange(n_blocks, dtype=jnp.int32) * MOE_ROWS
    block_e = jnp.minimum(jnp.sum((block_start[:, None] >= pad_end[None, :]).astype(jnp.int32), axis=1),
                          N_EXPERTS - 1)
    n_used = (pad_end[-1:] // MOE_ROWS).astype(jnp.int32)
    flat_tok = jnp.arange(n_assign, dtype=jnp.int32) % t
    _, tok_sorted = lax.sort_key_val(dest, flat_tok)
    starts = jnp.cumsum(counts) - counts
    rows = jnp.arange(n_blocks * MOE_ROWS, dtype=jnp.int32).reshape(n_blocks, MOE_ROWS)
    shift = (starts - pad_start)[block_e][:, None]
    live_end = (pad_start + counts)[block_e][:, None]
    compact = jnp.clip(rows + shift, 0, n_assign - 1)
    row_tok = jnp.where(rows < live_end, tok_sorted.at[compact].get(mode="promise_in_bounds"),
                        rows % t).reshape(-1)
    return row_tok, dest, block_e, n_used


def _ffn_ln_kernel(h_ref, y4_ref, gate_ref, g_ref, b_ref, o_ref):
    gates = gate_ref[...]
    acc = DN_ALPHA * h_ref[...]
    for k in range(TOP_K):
        acc = acc + gates[:, k:k + 1] * y4_ref[k].astype(F32)
    o_ref[...] = _layer_norm(acc, g_ref[...], b_ref[...])


def _ffn_ln(h, y4, gates, g, b):
    t = h.shape[0]
    return pl.pallas_call(
        _ffn_ln_kernel,
        grid=(t // ROW_TILE,),
        in_specs=[_rows(D_MODEL), pl.BlockSpec((TOP_K, ROW_TILE, D_MODEL), lambda i: (0, i, 0)),
                  _rows(LANES), _full(g.shape), _full(b.shape)],
        out_specs=_rows(D_MODEL),
        out_shape=jax.ShapeDtypeStruct((t, D_MODEL), F32),
        compiler_params=_params(),
    )(h, y4, gates, g, b)


def _moe_layer(hn, hn_bf, idx_tile, gate_tile, counts, layer, w_gu, b_gu, w_down, b_down, ln_g, ln_b):
    t = hn.shape[0]
    row_tok, dest, block_e, n_used = _routing_tables(idx_tile, counts, t)
    xs = hn_bf.at[row_tok].get(mode="promise_in_bounds")
    ys = _moe_experts(xs, block_e, n_used, layer, w_gu, b_gu, w_down, b_down)
    y4 = ys.at[dest].get(mode="promise_in_bounds").reshape(TOP_K, t, D_MODEL)
    return _ffn_ln(hn, y4, gate_tile, ln_g[None, :], ln_b[None, :])


def _router_consts(router_w, router_b):
    w = jnp.zeros((D_MODEL, LANES), F32).at[:, :N_EXPERTS].set(router_w)
    w_hi, w_lo = _split_bf16(w)
    b = jnp.full((1, LANES), NEG_BIG, F32).at[0, :N_EXPERTS].set(router_b)
    r = jnp.arange(ROW_TILE, dtype=jnp.int32)
    tri = (r[None, :] < r[:, None]).astype(BF16)
    return w_hi, w_lo, b, tri


def kernel(x, hy_w_in, conv_w, ssm_a_re, ssm_a_im, ssm_log_dt, ssm_b_re, ssm_b_im, ssm_c_re,
           ssm_c_im, ssm_d, ssm_w_glu, ssm_b_glu, hy_w_out, att_w_qkv, att_w_o, ln_mix_g,
           ln_mix_b, ln_ffn_g, ln_ffn_b, router_w, router_b, expert_w_gu, expert_b_gu,
           expert_w_down, expert_b_down):
    batch, seq_len, _ = x.shape
    t = batch * seq_len
    h = x.reshape(t, D_MODEL)
    cos, sin = _rope_tables(seq_len)
    s5_tables = jax.vmap(_s5_tables)(ssm_a_re, ssm_a_im, ssm_log_dt, ssm_b_re, ssm_b_im,
                                     ssm_c_re, ssm_c_im)
    for layer in range(DEPTH):
        i = layer // 2
        consts = [ln_mix_g[layer][None, :], ln_mix_b[layer][None, :],
                  *_router_consts(router_w[layer], router_b[layer])]
        if layer % 2 == 0:
            y_conv, u, u_sg = _inproj_conv(h, hy_w_in[i].astype(BF16), conv_w[i], seq_len)
            y_sg = _s5_scan(u_sg, s5_tables, i, batch, seq_len // SSM_CHUNK)
            w_out = hy_w_out[i].astype(BF16)
            hn, hn_bf, idx_tile, gate_tile, counts = _even_post(
                h, y_conv, y_sg, u,
                [ssm_d[i].reshape(1, SSM_WIDTH), ssm_w_glu[i].astype(BF16), ssm_b_glu[i][None, :],
                 w_out[:CONV_CH], w_out[CONV_CH:], *consts])
        else:
            hn, hn_bf, idx_tile, gate_tile, counts = _attention_layer(
                h, att_w_qkv[i], att_w_o[i], cos, sin, consts, batch, seq_len)
        h = _moe_layer(hn, hn_bf, idx_tile, gate_tile, counts, layer, expert_w_gu, expert_b_gu[layer],
                       expert_w_down, expert_b_down[layer], ln_ffn_g[layer], ln_ffn_b[layer])
    return h.reshape(batch, seq_len, D_MODEL)
```

```python
import functools

import jax
import jax.numpy as jnp
from jax import lax
from jax.experimental import pallas as pl
from jax.experimental.pallas import tpu as pltpu

F32 = jnp.float32
BF16 = jnp.bfloat16

D_MODEL = 1024
DEPTH = 4
CONV_CH = 512
SSM_WIDTH = 512
SSM_GROUP = 16
SSM_GROUPS = 32
SSM_STATE = 64
N_HEADS = 16
HEAD_DIM = 64
ROPE_THETA = 10000.0
DILATIONS = (1, 4, 16)
ATT_BLOCK = 128
ATT_BLOCKS_PER_STEP = 4
N_EXPERTS = 32
TOP_K = 4
SWIGLU_LIMIT = 7.0
SWIGLU_ALPHA = 1.702
DN_ALPHA = (2 * DEPTH) ** 0.25
LN_EPS = 1e-5

LANES = 128
ROW_TILE = 512
QKV_TILE = 256
SSM_CHUNK = 8
MOE_ROWS = 512
VMEM_LIMIT = 56 * 1024 * 1024
NEG_BIG = -1e30


def _params(n_axes=1):
    return pltpu.CompilerParams(dimension_semantics=("arbitrary",) * n_axes,
                                vmem_limit_bytes=VMEM_LIMIT)


def _full(shape):
    return pl.BlockSpec(shape, lambda *_: (0,) * len(shape))


def _rows(width, tile=ROW_TILE):
    return pl.BlockSpec((tile, width), lambda i: (i, 0))


def _layer_norm(x, g, b):
    mu = jnp.mean(x, axis=-1, keepdims=True)
    xc = x - mu
    var = jnp.mean(xc * xc, axis=-1, keepdims=True)
    return xc * lax.rsqrt(var + LN_EPS) * g + b


def _sigmoid(x):
    return 1.0 / (1.0 + jnp.exp(-x))


def _split_bf16(x):
    hi = x.astype(BF16)
    return hi, (x - hi.astype(F32)).astype(BF16)


def _route(hn, rw_both_ref, rw_hi_ref, rb_ref, tri_ref, idx_ref, gate_ref, cnt_ref, seen_ref):
    @pl.when(pl.program_id(0) == 0)
    def _():
        seen_ref[...] = jnp.zeros_like(seen_ref)

    hi, lo = _split_bf16(hn)
    both = jnp.dot(hi, rw_both_ref[...], preferred_element_type=F32)
    logits = (both[:, :LANES] + both[:, LANES:]
              + jnp.dot(lo, rw_hi_ref[...], preferred_element_type=F32)
              + rb_ref[...])
    lane = lax.broadcasted_iota(jnp.int32, logits.shape, 1)
    lane_f = lane.astype(F32)
    idx_f = jnp.zeros(logits.shape, F32)
    val_tile = jnp.zeros(logits.shape, F32)
    top0 = None
    den = None
    firsts = []
    for k in range(TOP_K):
        mx = jnp.max(logits, axis=-1, keepdims=True)
        first = jnp.min(jnp.where(logits == mx, lane_f, float(LANES)), axis=-1, keepdims=True)
        if k == 0:
            top0 = mx
        e = jnp.exp(mx - top0)
        den = e if k == 0 else den + e
        idx_f = jnp.where(lane == k, first, idx_f)
        val_tile = jnp.where(lane == k, e, val_tile)
        logits = jnp.where(lane_f == first, -jnp.inf, logits)
        firsts.append(first)
    gate_ref[...] = val_tile / den
    picked = jnp.where(logits == -jnp.inf, 1.0, 0.0)
    seen = seen_ref[0:1, :]
    before = jnp.dot(tri_ref[...], picked.astype(BF16), preferred_element_type=F32) + seen
    for k, first in enumerate(firsts):
        rank = jnp.sum(jnp.where(lane_f == first, before, 0.0), axis=-1, keepdims=True)
        idx_f = jnp.where(lane == TOP_K + k, rank, idx_f)
    idx_ref[...] = idx_f.T[:2 * TOP_K, :].astype(jnp.int32)
    seen = seen + jnp.sum(picked, axis=0, keepdims=True)
    seen_ref[...] = jnp.broadcast_to(seen, seen_ref.shape)
    cnt_ref[...] = jnp.broadcast_to(seen, cnt_ref.shape)


def _post_outputs(t):
    specs = [_rows(D_MODEL), _rows(D_MODEL), pl.BlockSpec((2 * TOP_K, ROW_TILE), lambda i: (0, i)),
             _rows(LANES), _full((8, LANES))]
    shapes = [jax.ShapeDtypeStruct((t, D_MODEL), F32), jax.ShapeDtypeStruct((t, D_MODEL), BF16),
              jax.ShapeDtypeStruct((2 * TOP_K, t), jnp.int32), jax.ShapeDtypeStruct((t, LANES), F32),
              jax.ShapeDtypeStruct((8, LANES), F32)]
    return specs, shapes


def _norm_and_route(pre, g_ref, b_ref, route_refs, hn_ref, hnb_ref, idx_ref, gate_ref, cnt_ref,
                    seen_ref):
    hn = _layer_norm(pre, g_ref[...], b_ref[...])
    hn_ref[...] = hn
    hnb_ref[...] = hn.astype(BF16)
    _route(hn, *route_refs, idx_ref, gate_ref, cnt_ref, seen_ref)


def _chunk_perm(rows):
    j = jnp.arange(rows, dtype=jnp.int32)
    n = rows // SSM_CHUNK
    src = (j % n) * SSM_CHUNK + j // n
    return (src[:, None] == j[None, :]).astype(BF16)


def _inproj_conv_kernel(x_ref, w_ref, cw_ref, perm_ref, yconv_ref, u_ref, usg_ref, carry_ref, *,
                        tiles_per_seq):
    @pl.when(pl.program_id(0) % tiles_per_seq == 0)
    def _():
        carry_ref[...] = jnp.zeros_like(carry_ref)

    proj = jnp.dot(x_ref[...].astype(BF16), w_ref[...], preferred_element_type=F32)
    gate_b = proj[:, :CONV_CH]
    gate_c = proj[:, CONV_CH:2 * CONV_CH]
    hid = proj[:, 2 * CONV_CH:3 * CONV_CH]
    v = gate_c * hid
    rows = v.shape[0]
    row = lax.broadcasted_iota(jnp.int32, v.shape, 0)
    prev1 = carry_ref[7:8, :]
    prev2 = carry_ref[6:7, :]
    vm1 = jnp.where(row == 0, prev1, pltpu.roll(v, 1, 0))
    vm2 = jnp.where(row == 0, prev2, jnp.where(row == 1, prev1, pltpu.roll(v, 2, 0)))
    conv = cw_ref[0:1, :] * vm2 + cw_ref[1:2, :] * vm1 + cw_ref[2:3, :] * v
    yconv_ref[...] = gate_b * conv
    u = proj[:, 3 * CONV_CH:]
    u_ref[...] = u
    carry_ref[...] = v[rows - 8:, :]
    by_pos = jnp.dot(perm_ref[...], u.astype(BF16), preferred_element_type=F32).astype(BF16)
    n = rows // SSM_CHUNK
    for sg in range(SSM_WIDTH // LANES):
        for s in range(SSM_CHUNK):
            usg_ref[sg, :, s * LANES:(s + 1) * LANES] = by_pos[s * n:(s + 1) * n,
                                                               sg * LANES:(sg + 1) * LANES]


def _inproj_conv(h, w_in_bf, conv_w, seq_len):
    t = h.shape[0]
    n_sg = SSM_WIDTH // LANES
    perm = _chunk_perm(ROW_TILE)
    return pl.pallas_call(
        functools.partial(_inproj_conv_kernel, tiles_per_seq=seq_len // ROW_TILE),
        grid=(t // ROW_TILE,),
        in_specs=[_rows(D_MODEL), _full(w_in_bf.shape), _full(conv_w.shape), _full(perm.shape)],
        out_specs=[_rows(CONV_CH), _rows(SSM_WIDTH),
                   pl.BlockSpec((n_sg, ROW_TILE // SSM_CHUNK, SSM_CHUNK * LANES), lambda i: (0, i, 0))],
        out_shape=[jax.ShapeDtypeStruct((t, CONV_CH), F32),
                   jax.ShapeDtypeStruct((t, SSM_WIDTH), F32),
                   jax.ShapeDtypeStruct((n_sg, t // SSM_CHUNK, SSM_CHUNK * LANES), BF16)],
        scratch_shapes=[pltpu.VMEM((8, CONV_CH), F32)],
        compiler_params=_params(),
    )(h, w_in_bf, conv_w, perm)


def _s5_tables(a_re, a_im, log_dt, b_re, b_im, c_re, c_im):
    q = SSM_CHUNK
    per_sg = LANES // SSM_GROUP
    n_sg = SSM_GROUPS // per_sg
    lam_re = jnp.minimum(a_re, -1e-4)
    lam_im = a_im
    dt = jnp.exp(log_dt)[:, None]
    mag = jnp.exp(lam_re * dt)
    ab_re = mag * jnp.cos(lam_im * dt)
    ab_im = mag * jnp.sin(lam_im * dt)
    nr, ni = ab_re - 1.0, ab_im
    den = lam_re * lam_re + lam_im * lam_im
    coef_re = ((nr * lam_re + ni * lam_im) / den)[..., None]
    coef_im = ((ni * lam_re - nr * lam_im) / den)[..., None]
    bb_re = coef_re * b_re - coef_im * b_im
    bb_im = coef_re * b_im + coef_im * b_re
    j = jnp.arange(q + 1, dtype=F32)
    pmag = jnp.exp((lam_re * dt)[..., None] * j)
    pang = (lam_im * dt)[..., None] * j
    pw_re = pmag * jnp.cos(pang)
    pw_im = pmag * jnp.sin(pang)
    ca_re = c_re[..., None] * pw_re[:, None] - c_im[..., None] * pw_im[:, None]
    ca_im = c_re[..., None] * pw_im[:, None] + c_im[..., None] * pw_re[:, None]
    kern = (jnp.einsum("gapj,gph->gjah", ca_re, bb_re, precision="highest")
            - jnp.einsum("gapj,gph->gjah", ca_im, bb_im, precision="highest"))[:, :q]
    lag = jnp.arange(q)[None, :] - jnp.arange(q)[:, None]
    toep = kern[:, jnp.clip(lag, 0, q - 1)]
    toep = jnp.where((lag >= 0)[None, :, :, None, None], toep, 0.0).transpose(0, 1, 4, 2, 3)
    rev = q - 1 - jnp.arange(q)
    pr = pw_re[:, :, rev]
    pi = pw_im[:, :, rev]
    inj_re = pr[..., None] * bb_re[:, :, None] - pi[..., None] * bb_im[:, :, None]
    inj_im = pr[..., None] * bb_im[:, :, None] + pi[..., None] * bb_re[:, :, None]
    inj_re = inj_re.transpose(0, 2, 3, 1)
    inj_im = inj_im.transpose(0, 2, 3, 1)
    out_re = ca_re[..., 1:].transpose(0, 2, 3, 1)
    out_im = (-ca_im[..., 1:]).transpose(0, 2, 3, 1)
    eye = jnp.eye(per_sg, dtype=F32)
    width = q * LANES
    n_state = per_sg * SSM_STATE

    def sg(a):
        return a.reshape((n_sg, per_sg) + a.shape[1:])

    toep_sg = jnp.einsum("Scshtk,cd->Sschtdk", sg(toep), eye).reshape(n_sg, width, width)
    inj_sg = jnp.concatenate(
        [jnp.einsum("Scshp,cd->Sschdp", sg(a), eye).reshape(n_sg, width, n_state)
         for a in (inj_re, inj_im)], axis=2)
    out_sg = jnp.concatenate(
        [jnp.einsum("Scpth,cd->Scptdh", sg(a), eye).reshape(n_sg, n_state, width)
         for a in (out_re, out_im)], axis=1)
    aq = jnp.stack([pw_re[..., q].reshape(n_sg, n_state), pw_im[..., q].reshape(n_sg, n_state)], axis=1)
    return toep_sg.astype(BF16), inj_sg.astype(BF16), out_sg.astype(BF16), aq


def _s5_kernel(u_ref, toep_ref, inj_ref, out_ref, aq_ref, y_ref, inj_scr, xs_scr, *, n_chunks):
    n_state = aq_ref.shape[2]
    u = u_ref[0]
    inj_scr[...] = jnp.dot(u, inj_ref[0], preferred_element_type=F32)
    aq_re = jnp.broadcast_to(aq_ref[0, 0:1, :], (8, n_state))
    aq_im = jnp.broadcast_to(aq_ref[0, 1:2, :], (8, n_state))
    sub = lax.broadcasted_iota(jnp.int32, (8, n_state), 0)

    def block(blk, carry):
        re, im = carry
        start = pl.multiple_of(blk * 8, 8)
        inj = inj_scr[pl.ds(start, 8), :]
        xs_re = jnp.zeros((8, n_state), F32)
        xs_im = jnp.zeros((8, n_state), F32)
        for j in range(8):
            xs_re = jnp.where(sub == j, re, xs_re)
            xs_im = jnp.where(sub == j, im, xs_im)
            in_re = jnp.broadcast_to(inj[j:j + 1, :n_state], (8, n_state))
            in_im = jnp.broadcast_to(inj[j:j + 1, n_state:], (8, n_state))
            re, im = aq_re * re - aq_im * im + in_re, aq_re * im + aq_im * re + in_im
        xs_scr[pl.ds(start, 8), :n_state] = xs_re
        xs_scr[pl.ds(start, 8), n_state:] = xs_im
        return re, im

    zero = jnp.zeros((8, n_state), F32)
    lax.fori_loop(0, n_chunks // 8, block, (zero, zero))
    y = jnp.dot(u, toep_ref[0], preferred_element_type=F32)
    y_ref[0] = y + jnp.dot(xs_scr[...].astype(BF16), out_ref[0], preferred_element_type=F32)


def _s5_scan(u_sg, tables, batch, n_chunks):
    toep, inj, out, aq = tables
    n_sg, _, width = u_sg.shape
    n_state2 = inj.shape[2]

    def per_sg(shape):
        return pl.BlockSpec((1,) + shape, lambda g, b: (g, 0, 0))

    return pl.pallas_call(
        functools.partial(_s5_kernel, n_chunks=n_chunks),
        grid=(n_sg, batch),
        in_specs=[pl.BlockSpec((1, n_chunks, width), lambda g, b: (g, b, 0)),
                  per_sg((width, width)), per_sg((width, n_state2)), per_sg((n_state2, width)),
                  per_sg((2, n_state2 // 2))],
        out_specs=pl.BlockSpec((1, n_chunks, width), lambda g, b: (g, b, 0)),
        out_shape=jax.ShapeDtypeStruct(u_sg.shape, F32),
        scratch_shapes=[pltpu.VMEM((n_chunks, n_state2), F32)] * 2,
        compiler_params=_params(2),
    )(u_sg, toep, inj, out, aq)


def _even_post_kernel(h_ref, yc_ref, ysg_ref, u_ref, perm_ref, d_ref, wglu_ref, bglu_ref, woc_ref,
                      wos_ref, g_ref, b_ref, rwh_ref, rwl_ref, rb_ref, tri_ref, *out_and_scratch):
    n_sg = ysg_ref.shape[0]
    by_pos = jnp.concatenate(
        [jnp.concatenate([ysg_ref[sg, :, s * LANES:(s + 1) * LANES] for sg in range(n_sg)], axis=1)
         for s in range(SSM_CHUNK)], axis=0)
    hi, lo = _split_bf16(by_pos)
    y_scan = (jnp.dot(perm_ref[...], hi, preferred_element_type=F32)
              + jnp.dot(perm_ref[...], lo, preferred_element_type=F32))
    y = y_scan + d_ref[...] * u_ref[...]
    z = jax.nn.gelu(y)
    glu = jnp.dot(z.astype(BF16), wglu_ref[...], preferred_element_type=F32) + bglu_ref[...]
    z = z * _sigmoid(glu)
    mix = (jnp.dot(yc_ref[...].astype(BF16), woc_ref[...], preferred_element_type=F32)
           + jnp.dot(z.astype(BF16), wos_ref[...], preferred_element_type=F32))
    _norm_and_route(DN_ALPHA * h_ref[...] + mix, g_ref, b_ref, (rwh_ref, rwl_ref, rb_ref, tri_ref),
                    *out_and_scratch)


def _even_post(h, y_conv, y_sg, u, consts):
    t = h.shape[0]
    n_sg = y_sg.shape[0]
    out_specs, out_shape = _post_outputs(t)
    consts = [_chunk_perm(ROW_TILE).T, *consts]
    return pl.pallas_call(
        _even_post_kernel,
        grid=(t // ROW_TILE,),
        in_specs=[_rows(D_MODEL), _rows(CONV_CH),
                  pl.BlockSpec((n_sg, ROW_TILE // SSM_CHUNK, SSM_CHUNK * LANES), lambda i: (0, i, 0)),
                  _rows(SSM_WIDTH)] + [_full(a.shape) for a in consts],
        out_specs=out_specs, out_shape=out_shape,
        scratch_shapes=[pltpu.VMEM((8, LANES), F32)],
        compiler_params=_params(),
    )(h, y_conv, y_sg, u, *consts)


def _qkv_rope_kernel(x_ref, w_ref, cos_ref, sin_ref, p4_ref, p16_ref, o1_ref, o4_ref, o16_ref):
    qkv = jnp.dot(x_ref[...].astype(BF16), w_ref[...], preferred_element_type=F32)
    rows = qkv.shape[0]
    cos = jnp.concatenate([cos_ref[...]] * (D_MODEL // LANES), axis=1)
    sin = jnp.concatenate([sin_ref[...]] * (D_MODEL // LANES), axis=1)
    lane = lax.broadcasted_iota(jnp.int32, (rows, D_MODEL), 1)
    low_half = (lane % HEAD_DIM) < (HEAD_DIM // 2)

    def rope(xs):
        up = pltpu.roll(xs, D_MODEL - HEAD_DIM // 2, 1)
        down = pltpu.roll(xs, HEAD_DIM // 2, 1)
        return xs * cos + jnp.where(low_half, up, down) * sin

    parts = (rope(qkv[:, :D_MODEL]) * (HEAD_DIM ** -0.5), rope(qkv[:, D_MODEL:2 * D_MODEL]),
             qkv[:, 2 * D_MODEL:])
    for which, part in enumerate(parts):
        cols = slice(which * D_MODEL, (which + 1) * D_MODEL)
        part = part.astype(BF16)
        o1_ref[:, cols] = part
        for dil, perm_ref, ref in ((DILATIONS[1], p4_ref, o4_ref), (DILATIONS[2], p16_ref, o16_ref)):
            split = jnp.dot(perm_ref[...], part, preferred_element_type=F32).astype(BF16)
            n = rows // dil
            for r in range(dil):
                ref[0, r, :, cols] = split[r * n:(r + 1) * n, :]


def _rope_tables(seq_len):
    half = HEAD_DIM // 2
    inv = ROPE_THETA ** (-jnp.arange(half, dtype=F32) / half)
    ang = jnp.arange(seq_len, dtype=F32)[:, None] * inv[None, :]
    cos = jnp.tile(jnp.cos(ang), (1, LANES // half))
    sin = jnp.sin(ang)
    sin = jnp.tile(jnp.concatenate([-sin, sin], axis=1), (1, LANES // HEAD_DIM))
    return cos, sin


def _residue_spec(dil, rows, width, tiles_per_seq):
    return pl.BlockSpec((1, dil, rows // dil, width),
                        lambda i: (i // tiles_per_seq, 0, i % tiles_per_seq, 0))


def _qkv_rope(h, w_qkv_bf, cos, sin, batch, seq_len):
    t = h.shape[0]
    tps = seq_len // QKV_TILE
    d4, d16 = DILATIONS[1], DILATIONS[2]
    width = 3 * D_MODEL
    perms = []
    for dil in (d4, d16):
        j = jnp.arange(QKV_TILE, dtype=jnp.int32)
        src = (j % (QKV_TILE // dil)) * dil + j // (QKV_TILE // dil)
        perms.append((src[:, None] == j[None, :]).astype(BF16))
    return pl.pallas_call(
        _qkv_rope_kernel,
        grid=(t // QKV_TILE,),
        in_specs=[_rows(D_MODEL, QKV_TILE), _full(w_qkv_bf.shape),
                  pl.BlockSpec((QKV_TILE, LANES), lambda i: (i % tps, 0)),
                  pl.BlockSpec((QKV_TILE, LANES), lambda i: (i % tps, 0)),
                  _full((QKV_TILE, QKV_TILE)), _full((QKV_TILE, QKV_TILE))],
        out_specs=[_rows(width, QKV_TILE), _residue_spec(d4, QKV_TILE, width, tps),
                   _residue_spec(d16, QKV_TILE, width, tps)],
        out_shape=[jax.ShapeDtypeStruct((t, width), BF16),
                   jax.ShapeDtypeStruct((batch, d4, seq_len // d4, width), BF16),
                   jax.ShapeDtypeStruct((batch, d16, seq_len // d16, width), BF16)],
        compiler_params=_params(),
    )(h, w_qkv_bf, cos, sin, *perms)


def _attn_kernel(q_ref, kp_ref, kc_ref, vp_ref, vc_ref, o_ref, lse_ref):
    blk = ATT_BLOCK
    qi = lax.broadcasted_iota(jnp.int32, (blk, 2 * blk), 0)
    kj = lax.broadcasted_iota(jnp.int32, (blk, 2 * blk), 1)
    dist = blk + qi - kj
    band = jnp.where(dist >= 0, jnp.where(dist <= blk, 0.0, NEG_BIG), NEG_BIG)
    k_min = jnp.where(pl.program_id(1) == 0, blk, 0)
    lane = lax.broadcasted_iota(jnp.int32, (blk, LANES), 1)
    low = lane < HEAD_DIM
    for j in range(ATT_BLOCKS_PER_STEP):
        rows = slice(j * blk, (j + 1) * blk)
        before = slice((j - 1) * blk, j * blk)
        bias = jnp.where(kj >= k_min, band, NEG_BIG) if j == 0 else band
        lse_tile = jnp.zeros((blk, LANES), F32)
        for hp in range(N_HEADS // 2):
            cols = slice(hp * LANES, (hp + 1) * LANES)
            q2 = q_ref[0, rows, cols]
            k_prev = kp_ref[0, :, cols] if j == 0 else kc_ref[0, before, cols]
            v_prev = vp_ref[0, :, cols] if j == 0 else vc_ref[0, before, cols]
            kk = jnp.concatenate([k_prev, kc_ref[0, rows, cols]], axis=0)
            vv = jnp.concatenate([v_prev, vc_ref[0, rows, cols]], axis=0)
            halves = []
            for hh in range(2):
                head = 2 * hp + hh
                qm = jnp.where(low if hh == 0 else jnp.logical_not(low), q2, jnp.zeros_like(q2))
                s = lax.dot_general(qm, kk, (((1,), (1,)), ((), ())),
                                    preferred_element_type=F32) + bias
                m = jnp.max(s, axis=-1, keepdims=True)
                p = jnp.exp(s - m)
                den = jnp.sum(p, axis=-1, keepdims=True)
                pv = jnp.dot(p.astype(BF16), vv, preferred_element_type=F32)
                halves.append(pv / den)
                lse_tile = jnp.where(lane == head, m + jnp.log(den), lse_tile)
            o_ref[0, rows, cols] = jnp.where(low, halves[0], halves[1]).astype(BF16)
        lse_ref[0, rows, :] = lse_tile


def _attn_pattern(qkv_sub):
    n_sub, sub_len, _ = qkv_sub.shape

    step = ATT_BLOCKS_PER_STEP * ATT_BLOCK

    def cur(which):
        return pl.BlockSpec((1, step, D_MODEL), lambda s, n: (s, n, which))

    def prev(which):
        return pl.BlockSpec((1, ATT_BLOCK, D_MODEL),
                            lambda s, n: (s, jnp.maximum(n * ATT_BLOCKS_PER_STEP - 1, 0), which))

    return pl.pallas_call(
        _attn_kernel,
        grid=(n_sub, sub_len // step),
        in_specs=[cur(0), prev(1), cur(1), prev(2), cur(2)],
        out_specs=[pl.BlockSpec((1, step, D_MODEL), lambda s, n: (s, n, 0)),
                   pl.BlockSpec((1, step, LANES), lambda s, n: (s, n, 0))],
        out_shape=[jax.ShapeDtypeStruct((n_sub, sub_len, D_MODEL), BF16),
                   jax.ShapeDtypeStruct((n_sub, sub_len, LANES), F32)],
        compiler_params=_params(2),
    )(*([qkv_sub] * 5))


def _odd_post_kernel(h_ref, o1_ref, l1_ref, o4_ref, l4_ref, o16_ref, l16_ref, expand_ref, wo_ref,
                     g_ref, b_ref, rwh_ref, rwl_ref, rb_ref, tri_ref, hn_ref, hnb_ref, idx_ref,
                     gate_ref, cnt_ref, os4_ref, ls4_ref, os16_ref, ls16_ref, seen_ref):
    rows = h_ref.shape[0]
    n_col = D_MODEL // LANES
    for dil, o_ref, l_ref, os_ref, ls_ref in ((DILATIONS[1], o4_ref, l4_ref, os4_ref, ls4_ref),
                                              (DILATIONS[2], o16_ref, l16_ref, os16_ref, ls16_ref)):
        for r in range(dil):
            sel = pl.ds(r, rows // dil, stride=dil)
            ls_ref[sel, :] = l_ref[0, r]
            for c in range(n_col):
                os_ref[c, sel, :] = o_ref[0, r, :, c * LANES:(c + 1) * LANES].astype(F32)
    lses = (l1_ref[...], ls4_ref[...], ls16_ref[...])
    outs = (o1_ref[...].astype(F32),
            jnp.concatenate([os4_ref[c] for c in range(n_col)], axis=1),
            jnp.concatenate([os16_ref[c] for c in range(n_col)], axis=1))
    mx = jnp.maximum(jnp.maximum(lses[0], lses[1]), lses[2])
    es = [jnp.exp(l - mx) for l in lses]
    den = es[0] + es[1] + es[2]
    o = jnp.zeros((rows, D_MODEL), F32)
    for e, out in zip(es, outs):
        hi, lo = _split_bf16(e / den)
        w = jnp.dot(jnp.concatenate([hi, lo], axis=1), expand_ref[...], preferred_element_type=F32)
        o = o + w * out
    mix = jnp.dot(o.astype(BF16), wo_ref[...], preferred_element_type=F32)
    _norm_and_route(DN_ALPHA * h_ref[...] + mix, g_ref, b_ref, (rwh_ref, rwl_ref, rb_ref, tri_ref),
                    hn_ref, hnb_ref, idx_ref, gate_ref, cnt_ref, seen_ref)


def _odd_post(h, pattern_outs, consts, seq_len):
    t = h.shape[0]
    tps = seq_len // ROW_TILE
    (o1, l1), (o4, l4), (o16, l16) = pattern_outs
    d4, d16 = DILATIONS[1], DILATIONS[2]
    out_specs, out_shape = _post_outputs(t)
    return pl.pallas_call(
        _odd_post_kernel,
        grid=(t // ROW_TILE,),
        in_specs=[_rows(D_MODEL), _rows(D_MODEL), _rows(LANES),
                  _residue_spec(d4, ROW_TILE, D_MODEL, tps), _residue_spec(d4, ROW_TILE, LANES, tps),
                  _residue_spec(d16, ROW_TILE, D_MODEL, tps), _residue_spec(d16, ROW_TILE, LANES, tps)]
        + [_full(a.shape) for a in consts],
        out_specs=out_specs, out_shape=out_shape,
        scratch_shapes=[pltpu.VMEM((D_MODEL // LANES, ROW_TILE, LANES), F32),
                        pltpu.VMEM((ROW_TILE, LANES), F32)] * 2 + [pltpu.VMEM((8, LANES), F32)],
        compiler_params=_params(),
    )(h, o1, l1, o4, l4, o16, l16, *consts)


def _attention_layer(h, w_qkv, w_o, cos, sin, consts, batch, seq_len):
    t = h.shape[0]
    qkv1, qkv4, qkv16 = _qkv_rope(h, w_qkv.astype(BF16), cos, sin, batch, seq_len)
    outs = []
    for dil, qkv in zip(DILATIONS, (qkv1, qkv4, qkv16)):
        sub_len = seq_len // dil
        o, lse = _attn_pattern(qkv.reshape(batch * dil, sub_len, 3 * D_MODEL))
        if dil == 1:
            outs.append((o.reshape(t, D_MODEL), lse.reshape(t, LANES)))
        else:
            outs.append((o.reshape(batch, dil, sub_len, D_MODEL),
                         lse.reshape(batch, dil, sub_len, LANES)))
    head_of_lane = jnp.arange(D_MODEL, dtype=jnp.int32) // HEAD_DIM
    expand = (jnp.arange(LANES, dtype=jnp.int32)[:, None] == head_of_lane[None, :]).astype(BF16)
    expand = jnp.concatenate([expand, expand], axis=0)
    return _odd_post(h, outs, [expand, w_o.astype(BF16), *consts], seq_len)


def _moe_kernel(be_ref, nused_ref, x_ref, wgu_ref, bgu_ref, wd_ref, bd_ref, y_ref,
                wgu_bf_ref, wd_bf_ref):
    i = pl.program_id(0)
    new_expert = jnp.logical_or(i == 0, be_ref[i] != be_ref[jnp.maximum(i - 1, 0)])

    @pl.when(new_expert)
    def _():
        wgu_bf_ref[...] = wgu_ref[0, 0].astype(BF16)
        wd_bf_ref[...] = wd_ref[0, 0].astype(BF16)

    @pl.when(i < nused_ref[0])
    def _():
        hid = jnp.dot(x_ref[...], wgu_bf_ref[...], preferred_element_type=F32) + bgu_ref[0]
        gate = jnp.minimum(hid[:, :D_MODEL], SWIGLU_LIMIT)
        lin = jnp.clip(hid[:, D_MODEL:], -SWIGLU_LIMIT, SWIGLU_LIMIT)
        act = (lin + 1.0) * (gate * _sigmoid(SWIGLU_ALPHA * gate))
        y = jnp.dot(act.astype(BF16), wd_bf_ref[...], preferred_element_type=F32) + bd_ref[0]
        y_ref[...] = y.astype(BF16)

    @pl.when(i >= nused_ref[0])
    def _():
        y_ref[...] = jnp.zeros_like(y_ref)


def _moe_experts(xs, block_e, n_used, layer, w_gu, b_gu, w_down, b_down):
    n_rows = xs.shape[0]
    grid_spec = pltpu.PrefetchScalarGridSpec(
        num_scalar_prefetch=2,
        grid=(n_rows // MOE_ROWS,),
        in_specs=[pl.BlockSpec((MOE_ROWS, D_MODEL), lambda i, be, nu: (i, 0)),
                  pl.BlockSpec((1, 1, D_MODEL, 2 * D_MODEL), lambda i, be, nu: (layer, be[i], 0, 0)),
                  pl.BlockSpec((1, 1, 2 * D_MODEL), lambda i, be, nu: (be[i], 0, 0)),
                  pl.BlockSpec((1, 1, D_MODEL, D_MODEL), lambda i, be, nu: (layer, be[i], 0, 0)),
                  pl.BlockSpec((1, 1, D_MODEL), lambda i, be, nu: (be[i], 0, 0))],
        out_specs=pl.BlockSpec((MOE_ROWS, D_MODEL), lambda i, be, nu: (i, 0)),
        scratch_shapes=[pltpu.VMEM((D_MODEL, 2 * D_MODEL), BF16), pltpu.VMEM((D_MODEL, D_MODEL), BF16)],
    )
    return pl.pallas_call(
        _moe_kernel,
        grid_spec=grid_spec,
        out_shape=jax.ShapeDtypeStruct((n_rows, D_MODEL), BF16),
        compiler_params=_params(),
    )(block_e, n_used, xs, w_gu, b_gu[:, None, :], w_down, b_down[:, None, :])


def _routing_tables(idx_tile, counts, t):
    n_assign = t * TOP_K
    top_idx = idx_tile[:TOP_K]
    rank = idx_tile[TOP_K:]
    counts = counts[0, :N_EXPERTS].astype(jnp.int32)
    padded = (counts + MOE_ROWS - 1) // MOE_ROWS * MOE_ROWS
    pad_end = jnp.cumsum(padded)
    pad_start = pad_end - padded
    start_of = jnp.zeros_like(rank)
    for e in range(N_EXPERTS):
        start_of = jnp.where(top_idx == e, pad_start[e], start_of)
    dest = (start_of + rank).reshape(-1)
    n_blocks = n_assign // MOE_ROWS + N_EXPERTS
    block_start = jnp.arange(n_blocks, dtype=jnp.int32) * MOE_ROWS
    block_e = jnp.minimum(jnp.sum((block_start[:, None] >= pad_end[None, :]).astype(jnp.int32), axis=1),
                          N_EXPERTS - 1)
    n_used = (pad_end[-1:] // MOE_ROWS).astype(jnp.int32)
    flat_tok = jnp.arange(n_assign, dtype=jnp.int32) % t
    _, tok_sorted = lax.sort_key_val(dest, flat_tok)
    starts = jnp.cumsum(counts) - counts
    rows = jnp.arange(n_blocks * MOE_ROWS, dtype=jnp.int32).reshape(n_blocks, MOE_ROWS)
    shift = (starts - pad_start)[block_e][:, None]
    live_end = (pad_start + counts)[block_e][:, None]
    compact = jnp.clip(rows + shift, 0, n_assign - 1)
    row_tok = jnp.where(rows < live_end, tok_sorted.at[compact].get(mode="promise_in_bounds"),
                        rows % t).reshape(-1)
    return row_tok, dest, block_e, n_used


def _ffn_ln_kernel(h_ref, y4_ref, gate_ref, g_ref, b_ref, o_ref):
    gates = gate_ref[...]
    acc = DN_ALPHA * h_ref[...]
    for k in range(TOP_K):
        acc = acc + gates[:, k:k + 1] * y4_ref[k].astype(F32)
    o_ref[...] = _layer_norm(acc, g_ref[...], b_ref[...])


def _ffn_ln(h, y4, gates, g, b):
    t = h.shape[0]
    return pl.pallas_call(
        _ffn_ln_kernel,
        grid=(t // ROW_TILE,),
        in_specs=[_rows(D_MODEL), pl.BlockSpec((TOP_K, ROW_TILE, D_MODEL), lambda i: (0, i, 0)),
                  _rows(LANES), _full(g.shape), _full(b.shape)],
        out_specs=_rows(D_MODEL),
        out_shape=jax.ShapeDtypeStruct((t, D_MODEL), F32),
        compiler_params=_params(),
    )(h, y4, gates, g, b)


def _moe_layer(hn, hn_bf, idx_tile, gate_tile, counts, layer, w_gu, b_gu, w_down, b_down, ln_g, ln_b):
    t = hn.shape[0]
    row_tok, dest, block_e, n_used = _routing_tables(idx_tile, counts, t)
    xs = hn_bf.at[row_tok].get(mode="promise_in_bounds")
    ys = _moe_experts(xs, block_e, n_used, layer, w_gu, b_gu, w_down, b_down)
    y4 = ys.at[dest].get(mode="promise_in_bounds").reshape(TOP_K, t, D_MODEL)
    return _ffn_ln(hn, y4, gate_tile, ln_g[None, :], ln_b[None, :])


def _router_consts(router_w, router_b):
    w = jnp.zeros((D_MODEL, LANES), F32).at[:, :N_EXPERTS].set(router_w)
    w_hi, w_lo = _split_bf16(w)
    b = jnp.full((1, LANES), NEG_BIG, F32).at[0, :N_EXPERTS].set(router_b)
    r = jnp.arange(ROW_TILE, dtype=jnp.int32)
    tri = (r[None, :] < r[:, None]).astype(BF16)
    return jnp.concatenate([w_hi, w_lo], axis=1), w_hi, b, tri


def kernel(x, hy_w_in, conv_w, ssm_a_re, ssm_a_im, ssm_log_dt, ssm_b_re, ssm_b_im, ssm_c_re,
           ssm_c_im, ssm_d, ssm_w_glu, ssm_b_glu, hy_w_out, att_w_qkv, att_w_o, ln_mix_g,
           ln_mix_b, ln_ffn_g, ln_ffn_b, router_w, router_b, expert_w_gu, expert_b_gu,
           expert_w_down, expert_b_down):
    batch, seq_len, _ = x.shape
    t = batch * seq_len
    h = x.reshape(t, D_MODEL)
    cos, sin = _rope_tables(seq_len)
    for layer in range(DEPTH):
        i = layer // 2
        consts = [ln_mix_g[layer][None, :], ln_mix_b[layer][None, :],
                  *_router_consts(router_w[layer], router_b[layer])]
        if layer % 2 == 0:
            y_conv, u, u_sg = _inproj_conv(h, hy_w_in[i].astype(BF16), conv_w[i], seq_len)
            tables = _s5_tables(ssm_a_re[i], ssm_a_im[i], ssm_log_dt[i], ssm_b_re[i], ssm_b_im[i],
                                ssm_c_re[i], ssm_c_im[i])
            y_sg = _s5_scan(u_sg, tables, batch, seq_len // SSM_CHUNK)
            w_out = hy_w_out[i].astype(BF16)
            hn, hn_bf, idx_tile, gate_tile, counts = _even_post(
                h, y_conv, y_sg, u,
                [ssm_d[i].reshape(1, SSM_WIDTH), ssm_w_glu[i].astype(BF16), ssm_b_glu[i][None, :],
                 w_out[:CONV_CH], w_out[CONV_CH:], *consts])
        else:
            hn, hn_bf, idx_tile, gate_tile, counts = _attention_layer(
                h, att_w_qkv[i], att_w_o[i], cos, sin, consts, batch, seq_len)
        h = _moe_layer(hn, hn_bf, idx_tile, gate_tile, counts, layer, expert_w_gu, expert_b_gu[layer],
                       expert_w_down, expert_b_down[layer], ln_ffn_g[layer], ln_ffn_b[layer])
    return h.reshape(batch, seq_len, D_MODEL)
```

```python
import functools

import jax
import jax.numpy as jnp
from jax import lax
from jax.experimental import pallas as pl
from jax.experimental.pallas import tpu as pltpu

F32 = jnp.float32
BF16 = jnp.bfloat16

D_MODEL = 1024
DEPTH = 4
CONV_CH = 512
SSM_WIDTH = 512
SSM_GROUP = 16
SSM_GROUPS = 32
SSM_STATE = 64
N_HEADS = 16
HEAD_DIM = 64
ROPE_THETA = 10000.0
DILATIONS = (1, 4, 16)
ATT_BLOCK = 128
ATT_BLOCKS_PER_STEP = 4
N_EXPERTS = 32
TOP_K = 4
SWIGLU_LIMIT = 7.0
SWIGLU_ALPHA = 1.702
DN_ALPHA = (2 * DEPTH) ** 0.25
LN_EPS = 1e-5

LANES = 128
ROW_TILE = 512
QKV_TILE = 256
SSM_CHUNK = 8
MOE_ROWS = 512
VMEM_LIMIT = 56 * 1024 * 1024
NEG_BIG = -1e30


def _params(n_axes=1):
    return pltpu.CompilerParams(dimension_semantics=("arbitrary",) * n_axes,
                                vmem_limit_bytes=VMEM_LIMIT)


def _full(shape):
    return pl.BlockSpec(shape, lambda *_: (0,) * len(shape))


def _rows(width, tile=ROW_TILE):
    return pl.BlockSpec((tile, width), lambda i: (i, 0))


def _layer_norm(x, g, b):
    mu = jnp.mean(x, axis=-1, keepdims=True)
    xc = x - mu
    var = jnp.mean(xc * xc, axis=-1, keepdims=True)
    return xc * lax.rsqrt(var + LN_EPS) * g + b


def _sigmoid(x):
    return 1.0 / (1.0 + jnp.exp(-x))


def _split_bf16(x):
    hi = x.astype(BF16)
    return hi, (x - hi.astype(F32)).astype(BF16)


def _route(hn, rw_both_ref, rw_hi_ref, rb_ref, tri_ref, idx_ref, gate_ref, cnt_ref, seen_ref):
    @pl.when(pl.program_id(0) == 0)
    def _():
        seen_ref[...] = jnp.zeros_like(seen_ref)

    hi, lo = _split_bf16(hn)
    both = jnp.dot(hi, rw_both_ref[...], preferred_element_type=F32)
    logits = (both[:, :LANES] + both[:, LANES:]
              + jnp.dot(lo, rw_hi_ref[...], preferred_element_type=F32)
              + rb_ref[...])
    lane = lax.broadcasted_iota(jnp.int32, logits.shape, 1)
    lane_f = lane.astype(F32)
    idx_f = jnp.zeros(logits.shape, F32)
    val_tile = jnp.zeros(logits.shape, F32)
    top0 = None
    den = None
    firsts = []
    for k in range(TOP_K):
        mx = jnp.max(logits, axis=-1, keepdims=True)
        first = jnp.min(jnp.where(logits == mx, lane_f, float(LANES)), axis=-1, keepdims=True)
        if k == 0:
            top0 = mx
        e = jnp.exp(mx - top0)
        den = e if k == 0 else den + e
        idx_f = jnp.where(lane == k, first, idx_f)
        val_tile = jnp.where(lane == k, e, val_tile)
        logits = jnp.where(lane_f == first, -jnp.inf, logits)
        firsts.append(first)
    gate_ref[...] = val_tile / den
    picked = jnp.where(logits == -jnp.inf, 1.0, 0.0)
    seen = seen_ref[0:1, :]
    before = jnp.dot(tri_ref[...], picked.astype(BF16), preferred_element_type=F32) + seen
    for k, first in enumerate(firsts):
        rank = jnp.sum(jnp.where(lane_f == first, before, 0.0), axis=-1, keepdims=True)
        idx_f = jnp.where(lane == TOP_K + k, rank, idx_f)
    idx_ref[...] = idx_f.T[:2 * TOP_K, :].astype(jnp.int32)
    seen = seen + jnp.sum(picked, axis=0, keepdims=True)
    seen_ref[...] = jnp.broadcast_to(seen, seen_ref.shape)
    cnt_ref[...] = jnp.broadcast_to(seen, cnt_ref.shape)


def _post_outputs(t):
    specs = [_rows(D_MODEL), _rows(D_MODEL), pl.BlockSpec((2 * TOP_K, ROW_TILE), lambda i: (0, i)),
             _rows(LANES), _full((8, LANES))]
    shapes = [jax.ShapeDtypeStruct((t, D_MODEL), F32), jax.ShapeDtypeStruct((t, D_MODEL), BF16),
              jax.ShapeDtypeStruct((2 * TOP_K, t), jnp.int32), jax.ShapeDtypeStruct((t, LANES), F32),
              jax.ShapeDtypeStruct((8, LANES), F32)]
    return specs, shapes


def _norm_and_route(pre, g_ref, b_ref, route_refs, hn_ref, hnb_ref, idx_ref, gate_ref, cnt_ref,
                    seen_ref):
    hn = _layer_norm(pre, g_ref[...], b_ref[...])
    hn_ref[...] = hn
    hnb_ref[...] = hn.astype(BF16)
    _route(hn, *route_refs, idx_ref, gate_ref, cnt_ref, seen_ref)


def _chunk_perm(rows):
    j = jnp.arange(rows, dtype=jnp.int32)
    n = rows // SSM_CHUNK
    src = (j % n) * SSM_CHUNK + j // n
    return (src[:, None] == j[None, :]).astype(BF16)


def _inproj_conv_kernel(x_ref, w_ref, cw_ref, perm_ref, yconv_ref, u_ref, usg_ref, carry_ref, *,
                        tiles_per_seq):
    @pl.when(pl.program_id(0) % tiles_per_seq == 0)
    def _():
        carry_ref[...] = jnp.zeros_like(carry_ref)

    proj = jnp.dot(x_ref[...].astype(BF16), w_ref[...], preferred_element_type=F32)
    gate_b = proj[:, :CONV_CH]
    gate_c = proj[:, CONV_CH:2 * CONV_CH]
    hid = proj[:, 2 * CONV_CH:3 * CONV_CH]
    v = gate_c * hid
    rows = v.shape[0]
    row = lax.broadcasted_iota(jnp.int32, v.shape, 0)
    prev1 = carry_ref[7:8, :]
    prev2 = carry_ref[6:7, :]
    vm1 = jnp.where(row == 0, prev1, pltpu.roll(v, 1, 0))
    vm2 = jnp.where(row == 0, prev2, jnp.where(row == 1, prev1, pltpu.roll(v, 2, 0)))
    conv = cw_ref[0:1, :] * vm2 + cw_ref[1:2, :] * vm1 + cw_ref[2:3, :] * v
    yconv_ref[...] = gate_b * conv
    u = proj[:, 3 * CONV_CH:]
    u_ref[...] = u
    carry_ref[...] = v[rows - 8:, :]
    by_pos = jnp.dot(perm_ref[...], u.astype(BF16), preferred_element_type=F32).astype(BF16)
    n = rows // SSM_CHUNK
    for sg in range(SSM_WIDTH // LANES):
        for s in range(SSM_CHUNK):
            usg_ref[sg, :, s * LANES:(s + 1) * LANES] = by_pos[s * n:(s + 1) * n,
                                                               sg * LANES:(sg + 1) * LANES]


def _inproj_conv(h, w_in_bf, conv_w, seq_len):
    t = h.shape[0]
    n_sg = SSM_WIDTH // LANES
    perm = _chunk_perm(ROW_TILE)
    return pl.pallas_call(
        functools.partial(_inproj_conv_kernel, tiles_per_seq=seq_len // ROW_TILE),
        grid=(t // ROW_TILE,),
        in_specs=[_rows(D_MODEL), _full(w_in_bf.shape), _full(conv_w.shape), _full(perm.shape)],
        out_specs=[_rows(CONV_CH), _rows(SSM_WIDTH),
                   pl.BlockSpec((n_sg, ROW_TILE // SSM_CHUNK, SSM_CHUNK * LANES), lambda i: (0, i, 0))],
        out_shape=[jax.ShapeDtypeStruct((t, CONV_CH), F32),
                   jax.ShapeDtypeStruct((t, SSM_WIDTH), F32),
                   jax.ShapeDtypeStruct((n_sg, t // SSM_CHUNK, SSM_CHUNK * LANES), BF16)],
        scratch_shapes=[pltpu.VMEM((8, CONV_CH), F32)],
        compiler_params=_params(),
    )(h, w_in_bf, conv_w, perm)


def _s5_tables(a_re, a_im, log_dt, b_re, b_im, c_re, c_im):
    q = SSM_CHUNK
    per_sg = LANES // SSM_GROUP
    n_sg = SSM_GROUPS // per_sg
    lam_re = jnp.minimum(a_re, -1e-4)
    lam_im = a_im
    dt = jnp.exp(log_dt)[:, None]
    mag = jnp.exp(lam_re * dt)
    ab_re = mag * jnp.cos(lam_im * dt)
    ab_im = mag * jnp.sin(lam_im * dt)
    nr, ni = ab_re - 1.0, ab_im
    den = lam_re * lam_re + lam_im * lam_im
    coef_re = ((nr * lam_re + ni * lam_im) / den)[..., None]
    coef_im = ((ni * lam_re - nr * lam_im) / den)[..., None]
    bb_re = coef_re * b_re - coef_im * b_im
    bb_im = coef_re * b_im + coef_im * b_re
    j = jnp.arange(q + 1, dtype=F32)
    pmag = jnp.exp((lam_re * dt)[..., None] * j)
    pang = (lam_im * dt)[..., None] * j
    pw_re = pmag * jnp.cos(pang)
    pw_im = pmag * jnp.sin(pang)
    ca_re = c_re[..., None] * pw_re[:, None] - c_im[..., None] * pw_im[:, None]
    ca_im = c_re[..., None] * pw_im[:, None] + c_im[..., None] * pw_re[:, None]
    kern = (jnp.einsum("gapj,gph->gjah", ca_re, bb_re, precision="highest")
            - jnp.einsum("gapj,gph->gjah", ca_im, bb_im, precision="highest"))[:, :q]
    lag = jnp.arange(q)[None, :] - jnp.arange(q)[:, None]
    toep = kern[:, jnp.clip(lag, 0, q - 1)]
    toep = jnp.where((lag >= 0)[None, :, :, None, None], toep, 0.0).transpose(0, 1, 4, 2, 3)
    rev = q - 1 - jnp.arange(q)
    pr = pw_re[:, :, rev]
    pi = pw_im[:, :, rev]
    inj_re = pr[..., None] * bb_re[:, :, None] - pi[..., None] * bb_im[:, :, None]
    inj_im = pr[..., None] * bb_im[:, :, None] + pi[..., None] * bb_re[:, :, None]
    inj_re = inj_re.transpose(0, 2, 3, 1)
    inj_im = inj_im.transpose(0, 2, 3, 1)
    out_re = ca_re[..., 1:].transpose(0, 2, 3, 1)
    out_im = (-ca_im[..., 1:]).transpose(0, 2, 3, 1)
    width = q * LANES
    n_state = per_sg * SSM_STATE

    def member_second(a):
        return jnp.moveaxis(a.reshape((n_sg, per_sg) + a.shape[1:]), 1, 2)

    toep_c = member_second(toep.reshape(SSM_GROUPS, q, SSM_GROUP, q * SSM_GROUP)).reshape(
        n_sg, width, LANES)
    inj_c = member_second(jnp.concatenate([inj_re, inj_im], axis=-1)).reshape(
        n_sg, width, 2 * SSM_STATE)
    out_c = jnp.concatenate([a.reshape(n_sg, n_state, q * SSM_GROUP) for a in (out_re, out_im)],
                            axis=1)
    aq = jnp.stack([pw_re[..., q].reshape(n_sg, n_state), pw_im[..., q].reshape(n_sg, n_state)], axis=1)
    col = jnp.arange(width, dtype=jnp.int32)
    src_out = (col // LANES) * SSM_GROUP + col % SSM_GROUP
    src_state = (col // n_state) * SSM_STATE + col % SSM_STATE
    rows = jnp.arange(LANES, dtype=jnp.int32)[:, None]
    spread_out = (rows == src_out[None, :]).astype(BF16)
    spread_state = (rows == src_state[None, :]).astype(BF16)
    return (toep_c.astype(BF16), inj_c.astype(BF16), out_c.astype(BF16), aq,
            spread_out, spread_state)


def _s5_kernel(u_ref, toepc_ref, injc_ref, outc_ref, aq_ref, spo_ref, sps_ref, y_ref,
               inj_scr, xs_scr, toep_scr, injm_scr, outm_scr, *, n_chunks):
    n_state = aq_ref.shape[2]
    width = toep_scr.shape[0]

    @pl.when(pl.program_id(1) == 0)
    def _():
        r = lax.broadcasted_iota(jnp.int32, (width, width), 0)
        c = lax.broadcasted_iota(jnp.int32, (width, width), 1)
        per_sg = LANES // SSM_GROUP
        chan_member_r = (r // SSM_GROUP) % per_sg
        chan_member_c = (c // SSM_GROUP) % per_sg
        state_member_r = (r % n_state) // SSM_STATE
        state_member_c = (c % n_state) // SSM_STATE
        full = jnp.dot(toepc_ref[0], spo_ref[...], preferred_element_type=F32)
        toep_scr[...] = jnp.where(chan_member_r == chan_member_c, full, 0.0).astype(BF16)
        full = jnp.dot(injc_ref[0], sps_ref[...], preferred_element_type=F32)
        injm_scr[...] = jnp.where(chan_member_r == state_member_c, full, 0.0).astype(BF16)
        full = jnp.dot(outc_ref[0], spo_ref[...], preferred_element_type=F32)
        outm_scr[...] = jnp.where(state_member_r == chan_member_c, full, 0.0).astype(BF16)

    u = u_ref[0]
    inj_scr[...] = jnp.dot(u, injm_scr[...], preferred_element_type=F32)
    aq_re = jnp.broadcast_to(aq_ref[0, 0:1, :], (8, n_state))
    aq_im = jnp.broadcast_to(aq_ref[0, 1:2, :], (8, n_state))
    sub = lax.broadcasted_iota(jnp.int32, (8, n_state), 0)

    def block(blk, carry):
        re, im = carry
        start = pl.multiple_of(blk * 8, 8)
        inj = inj_scr[pl.ds(start, 8), :]
        xs_re = jnp.zeros((8, n_state), F32)
        xs_im = jnp.zeros((8, n_state), F32)
        for j in range(8):
            xs_re = jnp.where(sub == j, re, xs_re)
            xs_im = jnp.where(sub == j, im, xs_im)
            in_re = jnp.broadcast_to(inj[j:j + 1, :n_state], (8, n_state))
            in_im = jnp.broadcast_to(inj[j:j + 1, n_state:], (8, n_state))
            re, im = aq_re * re - aq_im * im + in_re, aq_re * im + aq_im * re + in_im
        xs_scr[pl.ds(start, 8), :n_state] = xs_re
        xs_scr[pl.ds(start, 8), n_state:] = xs_im
        return re, im

    zero = jnp.zeros((8, n_state), F32)
    lax.fori_loop(0, n_chunks // 8, block, (zero, zero))
    y = jnp.dot(u, toep_scr[...], preferred_element_type=F32)
    y_ref[0] = y + jnp.dot(xs_scr[...].astype(BF16), outm_scr[...], preferred_element_type=F32)


def _s5_scan(u_sg, tables, batch, n_chunks):
    toep_c, inj_c, out_c, aq, spread_out, spread_state = tables
    n_sg, _, width = u_sg.shape
    n_state2 = spread_state.shape[1]

    def per_sg(shape):
        return pl.BlockSpec((1,) + shape, lambda g, b: (g, 0, 0))

    return pl.pallas_call(
        functools.partial(_s5_kernel, n_chunks=n_chunks),
        grid=(n_sg, batch),
        in_specs=[pl.BlockSpec((1, n_chunks, width), lambda g, b: (g, b, 0)),
                  per_sg(toep_c.shape[1:]), per_sg(inj_c.shape[1:]), per_sg(out_c.shape[1:]),
                  per_sg(aq.shape[1:]), _full(spread_out.shape), _full(spread_state.shape)],
        out_specs=pl.BlockSpec((1, n_chunks, width), lambda g, b: (g, b, 0)),
        out_shape=jax.ShapeDtypeStruct(u_sg.shape, F32),
        scratch_shapes=[pltpu.VMEM((n_chunks, n_state2), F32)] * 2
        + [pltpu.VMEM((width, width), BF16), pltpu.VMEM((width, n_state2), BF16),
           pltpu.VMEM((n_state2, width), BF16)],
        compiler_params=_params(2),
    )(u_sg, toep_c, inj_c, out_c, aq, spread_out, spread_state)


def _even_post_kernel(h_ref, yc_ref, ysg_ref, u_ref, perm_ref, d_ref, wglu_ref, bglu_ref, woc_ref,
                      wos_ref, g_ref, b_ref, rwh_ref, rwl_ref, rb_ref, tri_ref, *out_and_scratch):
    n_sg = ysg_ref.shape[0]
    by_pos = jnp.concatenate(
        [jnp.concatenate([ysg_ref[sg, :, s * LANES:(s + 1) * LANES] for sg in range(n_sg)], axis=1)
         for s in range(SSM_CHUNK)], axis=0)
    hi, lo = _split_bf16(by_pos)
    y_scan = (jnp.dot(perm_ref[...], hi, preferred_element_type=F32)
              + jnp.dot(perm_ref[...], lo, preferred_element_type=F32))
    y = y_scan + d_ref[...] * u_ref[...]
    z = jax.nn.gelu(y)
    glu = jnp.dot(z.astype(BF16), wglu_ref[...], preferred_element_type=F32) + bglu_ref[...]
    z = z * _sigmoid(glu)
    mix = (jnp.dot(yc_ref[...].astype(BF16), woc_ref[...], preferred_element_type=F32)
           + jnp.dot(z.astype(BF16), wos_ref[...], preferred_element_type=F32))
    _norm_and_route(DN_ALPHA * h_ref[...] + mix, g_ref, b_ref, (rwh_ref, rwl_ref, rb_ref, tri_ref),
                    *out_and_scratch)


def _even_post(h, y_conv, y_sg, u, consts):
    t = h.shape[0]
    n_sg = y_sg.shape[0]
    out_specs, out_shape = _post_outputs(t)
    consts = [_chunk_perm(ROW_TILE).T, *consts]
    return pl.pallas_call(
        _even_post_kernel,
        grid=(t // ROW_TILE,),
        in_specs=[_rows(D_MODEL), _rows(CONV_CH),
                  pl.BlockSpec((n_sg, ROW_TILE // SSM_CHUNK, SSM_CHUNK * LANES), lambda i: (0, i, 0)),
                  _rows(SSM_WIDTH)] + [_full(a.shape) for a in consts],
        out_specs=out_specs, out_shape=out_shape,
        scratch_shapes=[pltpu.VMEM((8, LANES), F32)],
        compiler_params=_params(),
    )(h, y_conv, y_sg, u, *consts)


def _qkv_rope_kernel(x_ref, w_ref, cos_ref, sin_ref, p4_ref, p16_ref, o1_ref, o4_ref, o16_ref):
    qkv = jnp.dot(x_ref[...].astype(BF16), w_ref[...], preferred_element_type=F32)
    rows = qkv.shape[0]
    cos = jnp.concatenate([cos_ref[...]] * (D_MODEL // LANES), axis=1)
    sin = jnp.concatenate([sin_ref[...]] * (D_MODEL // LANES), axis=1)
    lane = lax.broadcasted_iota(jnp.int32, (rows, D_MODEL), 1)
    low_half = (lane % HEAD_DIM) < (HEAD_DIM // 2)

    def rope(xs):
        up = pltpu.roll(xs, D_MODEL - HEAD_DIM // 2, 1)
        down = pltpu.roll(xs, HEAD_DIM // 2, 1)
        return xs * cos + jnp.where(low_half, up, down) * sin

    parts = (rope(qkv[:, :D_MODEL]) * (HEAD_DIM ** -0.5), rope(qkv[:, D_MODEL:2 * D_MODEL]),
             qkv[:, 2 * D_MODEL:])
    for which, part in enumerate(parts):
        cols = slice(which * D_MODEL, (which + 1) * D_MODEL)
        part = part.astype(BF16)
        o1_ref[:, cols] = part
        for dil, perm_ref, ref in ((DILATIONS[1], p4_ref, o4_ref), (DILATIONS[2], p16_ref, o16_ref)):
            split = jnp.dot(perm_ref[...], part, preferred_element_type=F32).astype(BF16)
            n = rows // dil
            for r in range(dil):
                ref[0, r, :, cols] = split[r * n:(r + 1) * n, :]


def _rope_tables(seq_len):
    half = HEAD_DIM // 2
    inv = ROPE_THETA ** (-jnp.arange(half, dtype=F32) / half)
    ang = jnp.arange(seq_len, dtype=F32)[:, None] * inv[None, :]
    cos = jnp.tile(jnp.cos(ang), (1, LANES // half))
    sin = jnp.sin(ang)
    sin = jnp.tile(jnp.concatenate([-sin, sin], axis=1), (1, LANES // HEAD_DIM))
    return cos, sin


def _residue_spec(dil, rows, width, tiles_per_seq):
    return pl.BlockSpec((1, dil, rows // dil, width),
                        lambda i: (i // tiles_per_seq, 0, i % tiles_per_seq, 0))


def _qkv_rope(h, w_qkv_bf, cos, sin, batch, seq_len):
    t = h.shape[0]
    tps = seq_len // QKV_TILE
    d4, d16 = DILATIONS[1], DILATIONS[2]
    width = 3 * D_MODEL
    perms = []
    for dil in (d4, d16):
        j = jnp.arange(QKV_TILE, dtype=jnp.int32)
        src = (j % (QKV_TILE // dil)) * dil + j // (QKV_TILE // dil)
        perms.append((src[:, None] == j[None, :]).astype(BF16))
    return pl.pallas_call(
        _qkv_rope_kernel,
        grid=(t // QKV_TILE,),
        in_specs=[_rows(D_MODEL, QKV_TILE), _full(w_qkv_bf.shape),
                  pl.BlockSpec((QKV_TILE, LANES), lambda i: (i % tps, 0)),
                  pl.BlockSpec((QKV_TILE, LANES), lambda i: (i % tps, 0)),
                  _full((QKV_TILE, QKV_TILE)), _full((QKV_TILE, QKV_TILE))],
        out_specs=[_rows(width, QKV_TILE), _residue_spec(d4, QKV_TILE, width, tps),
                   _residue_spec(d16, QKV_TILE, width, tps)],
        out_shape=[jax.ShapeDtypeStruct((t, width), BF16),
                   jax.ShapeDtypeStruct((batch, d4, seq_len // d4, width), BF16),
                   jax.ShapeDtypeStruct((batch, d16, seq_len // d16, width), BF16)],
        compiler_params=_params(),
    )(h, w_qkv_bf, cos, sin, *perms)


def _attn_kernel(q_ref, kp_ref, kc_ref, vp_ref, vc_ref, o_ref, lse_ref):
    blk = ATT_BLOCK
    qi = lax.broadcasted_iota(jnp.int32, (blk, 2 * blk), 0)
    kj = lax.broadcasted_iota(jnp.int32, (blk, 2 * blk), 1)
    dist = blk + qi - kj
    band = jnp.where(dist >= 0, jnp.where(dist <= blk, 0.0, NEG_BIG), NEG_BIG)
    k_min = jnp.where(pl.program_id(1) == 0, blk, 0)
    lane = lax.broadcasted_iota(jnp.int32, (blk, LANES), 1)
    low = lane < HEAD_DIM
    for j in range(ATT_BLOCKS_PER_STEP):
        rows = slice(j * blk, (j + 1) * blk)
        before = slice((j - 1) * blk, j * blk)
        bias = jnp.where(kj >= k_min, band, NEG_BIG) if j == 0 else band
        lse_tile = jnp.zeros((blk, LANES), F32)
        for hp in range(N_HEADS // 2):
            cols = slice(hp * LANES, (hp + 1) * LANES)
            q2 = q_ref[0, rows, cols]
            k_prev = kp_ref[0, :, cols] if j == 0 else kc_ref[0, before, cols]
            v_prev = vp_ref[0, :, cols] if j == 0 else vc_ref[0, before, cols]
            kk = jnp.concatenate([k_prev, kc_ref[0, rows, cols]], axis=0)
            vv = jnp.concatenate([v_prev, vc_ref[0, rows, cols]], axis=0)
            halves = []
            for hh in range(2):
                head = 2 * hp + hh
                qm = jnp.where(low if hh == 0 else jnp.logical_not(low), q2, jnp.zeros_like(q2))
                s = lax.dot_general(qm, kk, (((1,), (1,)), ((), ())),
                                    preferred_element_type=F32) + bias
                m = jnp.max(s, axis=-1, keepdims=True)
                p = jnp.exp(s - m)
                den = jnp.sum(p, axis=-1, keepdims=True)
                pv = jnp.dot(p.astype(BF16), vv, preferred_element_type=F32)
                halves.append(pv / den)
                lse_tile = jnp.where(lane == head, m + jnp.log(den), lse_tile)
            o_ref[0, rows, cols] = jnp.where(low, halves[0], halves[1]).astype(BF16)
        lse_ref[0, rows, :] = lse_tile


def _attn_pattern(qkv_sub):
    n_sub, sub_len, _ = qkv_sub.shape

    step = ATT_BLOCKS_PER_STEP * ATT_BLOCK

    def cur(which):
        return pl.BlockSpec((1, step, D_MODEL), lambda s, n: (s, n, which))

    def prev(which):
        return pl.BlockSpec((1, ATT_BLOCK, D_MODEL),
                            lambda s, n: (s, jnp.maximum(n * ATT_BLOCKS_PER_STEP - 1, 0), which))

    return pl.pallas_call(
        _attn_kernel,
        grid=(n_sub, sub_len // step),
        in_specs=[cur(0), prev(1), cur(1), prev(2), cur(2)],
        out_specs=[pl.BlockSpec((1, step, D_MODEL), lambda s, n: (s, n, 0)),
                   pl.BlockSpec((1, step, LANES), lambda s, n: (s, n, 0))],
        out_shape=[jax.ShapeDtypeStruct((n_sub, sub_len, D_MODEL), BF16),
                   jax.ShapeDtypeStruct((n_sub, sub_len, LANES), F32)],
        compiler_params=_params(2),
    )(*([qkv_sub] * 5))


def _odd_post_kernel(h_ref, o1_ref, l1_ref, o4_ref, l4_ref, o16_ref, l16_ref, expand_ref, wo_ref,
                     g_ref, b_ref, rwh_ref, rwl_ref, rb_ref, tri_ref, hn_ref, hnb_ref, idx_ref,
                     gate_ref, cnt_ref, os4_ref, ls4_ref, os16_ref, ls16_ref, seen_ref):
    rows = h_ref.shape[0]
    n_col = D_MODEL // LANES
    for dil, o_ref, l_ref, os_ref, ls_ref in ((DILATIONS[1], o4_ref, l4_ref, os4_ref, ls4_ref),
                                              (DILATIONS[2], o16_ref, l16_ref, os16_ref, ls16_ref)):
        for r in range(dil):
            sel = pl.ds(r, rows // dil, stride=dil)
            ls_ref[sel, :] = l_ref[0, r]
            for c in range(n_col):
                os_ref[c, sel, :] = o_ref[0, r, :, c * LANES:(c + 1) * LANES].astype(F32)
    lses = (l1_ref[...], ls4_ref[...], ls16_ref[...])
    outs = (o1_ref[...].astype(F32),
            jnp.concatenate([os4_ref[c] for c in range(n_col)], axis=1),
            jnp.concatenate([os16_ref[c] for c in range(n_col)], axis=1))
    mx = jnp.maximum(jnp.maximum(lses[0], lses[1]), lses[2])
    es = [jnp.exp(l - mx) for l in lses]
    den = es[0] + es[1] + es[2]
    o = jnp.zeros((rows, D_MODEL), F32)
    for e, out in zip(es, outs):
        hi, lo = _split_bf16(e / den)
        w = jnp.dot(jnp.concatenate([hi, lo], axis=1), expand_ref[...], preferred_element_type=F32)
        o = o + w * out
    mix = jnp.dot(o.astype(BF16), wo_ref[...], preferred_element_type=F32)
    _norm_and_route(DN_ALPHA * h_ref[...] + mix, g_ref, b_ref, (rwh_ref, rwl_ref, rb_ref, tri_ref),
                    hn_ref, hnb_ref, idx_ref, gate_ref, cnt_ref, seen_ref)


def _odd_post(h, pattern_outs, consts, seq_len):
    t = h.shape[0]
    tps = seq_len // ROW_TILE
    (o1, l1), (o4, l4), (o16, l16) = pattern_outs
    d4, d16 = DILATIONS[1], DILATIONS[2]
    out_specs, out_shape = _post_outputs(t)
    return pl.pallas_call(
        _odd_post_kernel,
        grid=(t // ROW_TILE,),
        in_specs=[_rows(D_MODEL), _rows(D_MODEL), _rows(LANES),
                  _residue_spec(d4, ROW_TILE, D_MODEL, tps), _residue_spec(d4, ROW_TILE, LANES, tps),
                  _residue_spec(d16, ROW_TILE, D_MODEL, tps), _residue_spec(d16, ROW_TILE, LANES, tps)]
        + [_full(a.shape) for a in consts],
        out_specs=out_specs, out_shape=out_shape,
        scratch_shapes=[pltpu.VMEM((D_MODEL // LANES, ROW_TILE, LANES), F32),
                        pltpu.VMEM((ROW_TILE, LANES), F32)] * 2 + [pltpu.VMEM((8, LANES), F32)],
        compiler_params=_params(),
    )(h, o1, l1, o4, l4, o16, l16, *consts)


def _attention_layer(h, w_qkv, w_o, cos, sin, consts, batch, seq_len):
    t = h.shape[0]
    qkv1, qkv4, qkv16 = _qkv_rope(h, w_qkv.astype(BF16), cos, sin, batch, seq_len)
    outs = []
    for dil, qkv in zip(DILATIONS, (qkv1, qkv4, qkv16)):
        sub_len = seq_len // dil
        o, lse = _attn_pattern(qkv.reshape(batch * dil, sub_len, 3 * D_MODEL))
        if dil == 1:
            outs.append((o.reshape(t, D_MODEL), lse.reshape(t, LANES)))
        else:
            outs.append((o.reshape(batch, dil, sub_len, D_MODEL),
                         lse.reshape(batch, dil, sub_len, LANES)))
    head_of_lane = jnp.arange(D_MODEL, dtype=jnp.int32) // HEAD_DIM
    expand = (jnp.arange(LANES, dtype=jnp.int32)[:, None] == head_of_lane[None, :]).astype(BF16)
    expand = jnp.concatenate([expand, expand], axis=0)
    return _odd_post(h, outs, [expand, w_o.astype(BF16), *consts], seq_len)


def _moe_kernel(be_ref, nused_ref, x_ref, wgu_ref, bgu_ref, wd_ref, bd_ref, y_ref,
                wgu_bf_ref, wd_bf_ref):
    i = pl.program_id(0)
    new_expert = jnp.logical_or(i == 0, be_ref[i] != be_ref[jnp.maximum(i - 1, 0)])

    @pl.when(new_expert)
    def _():
        wgu_bf_ref[...] = wgu_ref[0, 0].astype(BF16)
        wd_bf_ref[...] = wd_ref[0, 0].astype(BF16)

    @pl.when(i < nused_ref[0])
    def _():
        hid = jnp.dot(x_ref[...], wgu_bf_ref[...], preferred_element_type=F32) + bgu_ref[0]
        gate = jnp.minimum(hid[:, :D_MODEL], SWIGLU_LIMIT)
        lin = jnp.clip(hid[:, D_MODEL:], -SWIGLU_LIMIT, SWIGLU_LIMIT)
        act = (lin + 1.0) * (gate * _sigmoid(SWIGLU_ALPHA * gate))
        y = jnp.dot(act.astype(BF16), wd_bf_ref[...], preferred_element_type=F32) + bd_ref[0]
        y_ref[...] = y.astype(BF16)

    @pl.when(i >= nused_ref[0])
    def _():
        y_ref[...] = jnp.zeros_like(y_ref)


def _moe_experts(xs, block_e, n_used, layer, w_gu, b_gu, w_down, b_down):
    n_rows = xs.shape[0]
    grid_spec = pltpu.PrefetchScalarGridSpec(
        num_scalar_prefetch=2,
        grid=(n_rows // MOE_ROWS,),
        in_specs=[pl.BlockSpec((MOE_ROWS, D_MODEL), lambda i, be, nu: (i, 0)),
                  pl.BlockSpec((1, 1, D_MODEL, 2 * D_MODEL), lambda i, be, nu: (layer, be[i], 0, 0)),
                  pl.BlockSpec((1, 1, 2 * D_MODEL), lambda i, be, nu: (be[i], 0, 0)),
                  pl.BlockSpec((1, 1, D_MODEL, D_MODEL), lambda i, be, nu: (layer, be[i], 0, 0)),
                  pl.BlockSpec((1, 1, D_MODEL), lambda i, be, nu: (be[i], 0, 0))],
        out_specs=pl.BlockSpec((MOE_ROWS, D_MODEL), lambda i, be, nu: (i, 0)),
        scratch_shapes=[pltpu.VMEM((D_MODEL, 2 * D_MODEL), BF16), pltpu.VMEM((D_MODEL, D_MODEL), BF16)],
    )
    return pl.pallas_call(
        _moe_kernel,
        grid_spec=grid_spec,
        out_shape=jax.ShapeDtypeStruct((n_rows, D_MODEL), BF16),
        compiler_params=_params(),
    )(block_e, n_used, xs, w_gu, b_gu[:, None, :], w_down, b_down[:, None, :])


def _routing_tables(idx_tile, counts, t):
    n_assign = t * TOP_K
    top_idx = idx_tile[:TOP_K]
    rank = idx_tile[TOP_K:]
    counts = counts[0, :N_EXPERTS].astype(jnp.int32)
    padded = (counts + MOE_ROWS - 1) // MOE_ROWS * MOE_ROWS
    pad_end = jnp.cumsum(padded)
    pad_start = pad_end - padded
    start_of = jnp.zeros_like(rank)
    for e in range(N_EXPERTS):
        start_of = jnp.where(top_idx == e, pad_start[e], start_of)
    dest = (start_of + rank).reshape(-1)
    n_blocks = n_assign // MOE_ROWS + N_EXPERTS
    block_start = jnp.arange(n_blocks, dtype=jnp.int32) * MOE_ROWS
    block_e = jnp.minimum(jnp.sum((block_start[:, None] >= pad_end[None, :]).astype(jnp.int32), axis=1),
                          N_EXPERTS - 1)
    n_used = (pad_end[-1:] // MOE_ROWS).astype(jnp.int32)
    flat_tok = jnp.arange(n_assign, dtype=jnp.int32) % t
    _, tok_sorted = lax.sort_key_val(dest, flat_tok)
    starts = jnp.cumsum(counts) - counts
    rows = jnp.arange(n_blocks * MOE_ROWS, dtype=jnp.int32).reshape(n_blocks, MOE_ROWS)
    shift = (starts - pad_start)[block_e][:, None]
    live_end = (pad_start + counts)[block_e][:, None]
    compact = jnp.clip(rows + shift, 0, n_assign - 1)
    row_tok = jnp.where(rows < live_end, tok_sorted.at[compact].get(mode="promise_in_bounds"),
                        rows % t).reshape(-1)
    return row_tok, dest, block_e, n_used


def _ffn_ln_kernel(h_ref, y4_ref, gate_ref, g_ref, b_ref, o_ref):
    gates = gate_ref[...]
    acc = DN_ALPHA * h_ref[...]
    for k in range(TOP_K):
        acc = acc + gates[:, k:k + 1] * y4_ref[k].astype(F32)
    o_ref[...] = _layer_norm(acc, g_ref[...], b_ref[...])


def _ffn_ln(h, y4, gates, g, b):
    t = h.shape[0]
    return pl.pallas_call(
        _ffn_ln_kernel,
        grid=(t // ROW_TILE,),
        in_specs=[_rows(D_MODEL), pl.BlockSpec((TOP_K, ROW_TILE, D_MODEL), lambda i: (0, i, 0)),
                  _rows(LANES), _full(g.shape), _full(b.shape)],
        out_specs=_rows(D_MODEL),
        out_shape=jax.ShapeDtypeStruct((t, D_MODEL), F32),
        compiler_params=_params(),
    )(h, y4, gates, g, b)


def _moe_layer(hn, hn_bf, idx_tile, gate_tile, counts, layer, w_gu, b_gu, w_down, b_down, ln_g, ln_b):
    t = hn.shape[0]
    row_tok, dest, block_e, n_used = _routing_tables(idx_tile, counts, t)
    xs = hn_bf.at[row_tok].get(mode="promise_in_bounds")
    ys = _moe_experts(xs, block_e, n_used, layer, w_gu, b_gu, w_down, b_down)
    y4 = ys.at[dest].get(mode="promise_in_bounds").reshape(TOP_K, t, D_MODEL)
    return _ffn_ln(hn, y4, gate_tile, ln_g[None, :], ln_b[None, :])


def _router_consts(router_w, router_b):
    w = jnp.zeros((D_MODEL, LANES), F32).at[:, :N_EXPERTS].set(router_w)
    w_hi, w_lo = _split_bf16(w)
    b = jnp.full((1, LANES), NEG_BIG, F32).at[0, :N_EXPERTS].set(router_b)
    r = jnp.arange(ROW_TILE, dtype=jnp.int32)
    tri = (r[None, :] < r[:, None]).astype(BF16)
    return jnp.concatenate([w_hi, w_lo], axis=1), w_hi, b, tri


def kernel(x, hy_w_in, conv_w, ssm_a_re, ssm_a_im, ssm_log_dt, ssm_b_re, ssm_b_im, ssm_c_re,
           ssm_c_im, ssm_d, ssm_w_glu, ssm_b_glu, hy_w_out, att_w_qkv, att_w_o, ln_mix_g,
           ln_mix_b, ln_ffn_g, ln_ffn_b, router_w, router_b, expert_w_gu, expert_b_gu,
           expert_w_down, expert_b_down):
    batch, seq_len, _ = x.shape
    t = batch * seq_len
    h = x.reshape(t, D_MODEL)
    cos, sin = _rope_tables(seq_len)
    for layer in range(DEPTH):
        i = layer // 2
        consts = [ln_mix_g[layer][None, :], ln_mix_b[layer][None, :],
                  *_router_consts(router_w[layer], router_b[layer])]
        if layer % 2 == 0:
            y_conv, u, u_sg = _inproj_conv(h, hy_w_in[i].astype(BF16), conv_w[i], seq_len)
            tables = _s5_tables(ssm_a_re[i], ssm_a_im[i], ssm_log_dt[i], ssm_b_re[i], ssm_b_im[i],
                                ssm_c_re[i], ssm_c_im[i])
            y_sg = _s5_scan(u_sg, tables, batch, seq_len // SSM_CHUNK)
            w_out = hy_w_out[i].astype(BF16)
            hn, hn_bf, idx_tile, gate_tile, counts = _even_post(
                h, y_conv, y_sg, u,
                [ssm_d[i].reshape(1, SSM_WIDTH), ssm_w_glu[i].astype(BF16), ssm_b_glu[i][None, :],
                 w_out[:CONV_CH], w_out[CONV_CH:], *consts])
        else:
            hn, hn_bf, idx_tile, gate_tile, counts = _attention_layer(
                h, att_w_qkv[i], att_w_o[i], cos, sin, consts, batch, seq_len)
        h = _moe_layer(hn, hn_bf, idx_tile, gate_tile, counts, layer, expert_w_gu, expert_b_gu[layer],
                       expert_w_down, expert_b_down[layer], ln_ffn_g[layer], ln_ffn_b[layer])
    return h.reshape(batch, seq_len, D_MODEL)
```

```python
import functools

import jax
import jax.numpy as jnp
from jax import lax
from jax.experimental import pallas as pl
from jax.experimental.pallas import tpu as pltpu

F32 = jnp.float32
BF16 = jnp.bfloat16

D_MODEL = 1024
DEPTH = 4
CONV_CH = 512
SSM_WIDTH = 512
SSM_GROUP = 16
SSM_GROUPS = 32
SSM_STATE = 64
N_HEADS = 16
HEAD_DIM = 64
ROPE_THETA = 10000.0
DILATIONS = (1, 4, 16)
ATT_BLOCK = 128
ATT_BLOCKS_PER_STEP = 4
N_EXPERTS = 32
TOP_K = 4
SWIGLU_LIMIT = 7.0
SWIGLU_ALPHA = 1.702
DN_ALPHA = (2 * DEPTH) ** 0.25
LN_EPS = 1e-5

LANES = 128
ROW_TILE = 512
QKV_TILE = 256
SSM_CHUNK = 8
MOE_ROWS = 512
VMEM_LIMIT = 56 * 1024 * 1024
NEG_BIG = -1e30


def _params(n_axes=1):
    return pltpu.CompilerParams(dimension_semantics=("arbitrary",) * n_axes,
                                vmem_limit_bytes=VMEM_LIMIT)


def _full(shape):
    return pl.BlockSpec(shape, lambda *_: (0,) * len(shape))


def _rows(width, tile=ROW_TILE):
    return pl.BlockSpec((tile, width), lambda i: (i, 0))


def _layer_norm(x, g, b):
    mu = jnp.mean(x, axis=-1, keepdims=True)
    xc = x - mu
    var = jnp.mean(xc * xc, axis=-1, keepdims=True)
    return xc * lax.rsqrt(var + LN_EPS) * g + b


def _sigmoid(x):
    return 1.0 / (1.0 + jnp.exp(-x))


def _split_bf16(x):
    hi = x.astype(BF16)
    return hi, (x - hi.astype(F32)).astype(BF16)


def _route(hn, rw_both_ref, rw_hi_ref, rb_ref, tri_ref, idx_ref, gate_ref, cnt_ref, seen_ref):
    @pl.when(pl.program_id(0) == 0)
    def _():
        seen_ref[...] = jnp.zeros_like(seen_ref)

    hi, lo = _split_bf16(hn)
    both = jnp.dot(hi, rw_both_ref[...], preferred_element_type=F32)
    logits = (both[:, :LANES] + both[:, LANES:]
              + jnp.dot(lo, rw_hi_ref[...], preferred_element_type=F32)
              + rb_ref[...])
    lane = lax.broadcasted_iota(jnp.int32, logits.shape, 1)
    lane_f = lane.astype(F32)
    idx_f = jnp.zeros(logits.shape, F32)
    val_tile = jnp.zeros(logits.shape, F32)
    top0 = None
    den = None
    firsts = []
    for k in range(TOP_K):
        mx = jnp.max(logits, axis=-1, keepdims=True)
        first = jnp.min(jnp.where(logits == mx, lane_f, float(LANES)), axis=-1, keepdims=True)
        if k == 0:
            top0 = mx
        e = jnp.exp(mx - top0)
        den = e if k == 0 else den + e
        idx_f = jnp.where(lane == k, first, idx_f)
        val_tile = jnp.where(lane == k, e, val_tile)
        logits = jnp.where(lane_f == first, -jnp.inf, logits)
        firsts.append(first)
    gate_ref[...] = val_tile / den
    picked = jnp.where(logits == -jnp.inf, 1.0, 0.0)
    seen = seen_ref[0:1, :]
    before = jnp.dot(tri_ref[...], picked.astype(BF16), preferred_element_type=F32) + seen
    for k, first in enumerate(firsts):
        rank = jnp.sum(jnp.where(lane_f == first, before, 0.0), axis=-1, keepdims=True)
        idx_f = jnp.where(lane == TOP_K + k, rank, idx_f)
    idx_ref[...] = idx_f.T[:2 * TOP_K, :].astype(jnp.int32)
    seen = seen + jnp.sum(picked, axis=0, keepdims=True)
    seen_ref[...] = jnp.broadcast_to(seen, seen_ref.shape)
    cnt_ref[...] = jnp.broadcast_to(seen, cnt_ref.shape)


def _post_outputs(t):
    specs = [_rows(D_MODEL), _rows(D_MODEL), pl.BlockSpec((2 * TOP_K, ROW_TILE), lambda i: (0, i)),
             _rows(LANES), _full((8, LANES))]
    shapes = [jax.ShapeDtypeStruct((t, D_MODEL), F32), jax.ShapeDtypeStruct((t, D_MODEL), BF16),
              jax.ShapeDtypeStruct((2 * TOP_K, t), jnp.int32), jax.ShapeDtypeStruct((t, LANES), F32),
              jax.ShapeDtypeStruct((8, LANES), F32)]
    return specs, shapes


def _norm_and_route(pre, g_ref, b_ref, route_refs, hn_ref, hnb_ref, idx_ref, gate_ref, cnt_ref,
                    seen_ref):
    hn = _layer_norm(pre, g_ref[...], b_ref[...])
    hn_ref[...] = hn
    hnb_ref[...] = hn.astype(BF16)
    _route(hn, *route_refs, idx_ref, gate_ref, cnt_ref, seen_ref)


def _chunk_perm(rows):
    j = jnp.arange(rows, dtype=jnp.int32)
    n = rows // SSM_CHUNK
    src = (j % n) * SSM_CHUNK + j // n
    return (src[:, None] == j[None, :]).astype(BF16)


def _inproj_conv_kernel(x_ref, w_ref, cw_ref, perm_ref, yconv_ref, u_ref, usg_ref, carry_ref, *,
                        tiles_per_seq):
    @pl.when(pl.program_id(0) % tiles_per_seq == 0)
    def _():
        carry_ref[...] = jnp.zeros_like(carry_ref)

    proj = jnp.dot(x_ref[...].astype(BF16), w_ref[...], preferred_element_type=F32)
    gate_b = proj[:, :CONV_CH]
    gate_c = proj[:, CONV_CH:2 * CONV_CH]
    hid = proj[:, 2 * CONV_CH:3 * CONV_CH]
    v = gate_c * hid
    rows = v.shape[0]
    row = lax.broadcasted_iota(jnp.int32, v.shape, 0)
    prev1 = carry_ref[7:8, :]
    prev2 = carry_ref[6:7, :]
    vm1 = jnp.where(row == 0, prev1, pltpu.roll(v, 1, 0))
    vm2 = jnp.where(row == 0, prev2, jnp.where(row == 1, prev1, pltpu.roll(v, 2, 0)))
    conv = cw_ref[0:1, :] * vm2 + cw_ref[1:2, :] * vm1 + cw_ref[2:3, :] * v
    yconv_ref[...] = gate_b * conv
    u = proj[:, 3 * CONV_CH:]
    u_ref[...] = u
    carry_ref[...] = v[rows - 8:, :]
    by_pos = jnp.dot(perm_ref[...], u.astype(BF16), preferred_element_type=F32).astype(BF16)
    n = rows // SSM_CHUNK
    for sg in range(SSM_WIDTH // LANES):
        for s in range(SSM_CHUNK):
            usg_ref[sg, :, s * LANES:(s + 1) * LANES] = by_pos[s * n:(s + 1) * n,
                                                               sg * LANES:(sg + 1) * LANES]


def _inproj_conv(h, w_in_bf, conv_w, seq_len):
    t = h.shape[0]
    n_sg = SSM_WIDTH // LANES
    perm = _chunk_perm(ROW_TILE)
    return pl.pallas_call(
        functools.partial(_inproj_conv_kernel, tiles_per_seq=seq_len // ROW_TILE),
        grid=(t // ROW_TILE,),
        in_specs=[_rows(D_MODEL), _full(w_in_bf.shape), _full(conv_w.shape), _full(perm.shape)],
        out_specs=[_rows(CONV_CH), _rows(SSM_WIDTH),
                   pl.BlockSpec((n_sg, ROW_TILE // SSM_CHUNK, SSM_CHUNK * LANES), lambda i: (0, i, 0))],
        out_shape=[jax.ShapeDtypeStruct((t, CONV_CH), F32),
                   jax.ShapeDtypeStruct((t, SSM_WIDTH), F32),
                   jax.ShapeDtypeStruct((n_sg, t // SSM_CHUNK, SSM_CHUNK * LANES), BF16)],
        scratch_shapes=[pltpu.VMEM((8, CONV_CH), F32)],
        compiler_params=_params(),
    )(h, w_in_bf, conv_w, perm)


def _s5_tables(a_re, a_im, log_dt, b_re, b_im, c_re, c_im):
    q = SSM_CHUNK
    per_sg = LANES // SSM_GROUP
    n_sg = SSM_GROUPS // per_sg
    lam_re = jnp.minimum(a_re, -1e-4)
    lam_im = a_im
    dt = jnp.exp(log_dt)[:, None]
    mag = jnp.exp(lam_re * dt)
    ab_re = mag * jnp.cos(lam_im * dt)
    ab_im = mag * jnp.sin(lam_im * dt)
    nr, ni = ab_re - 1.0, ab_im
    den = lam_re * lam_re + lam_im * lam_im
    coef_re = ((nr * lam_re + ni * lam_im) / den)[..., None]
    coef_im = ((ni * lam_re - nr * lam_im) / den)[..., None]
    bb_re = coef_re * b_re - coef_im * b_im
    bb_im = coef_re * b_im + coef_im * b_re
    j = jnp.arange(q + 1, dtype=F32)
    pmag = jnp.exp((lam_re * dt)[..., None] * j)
    pang = (lam_im * dt)[..., None] * j
    pw_re = pmag * jnp.cos(pang)
    pw_im = pmag * jnp.sin(pang)
    ca_re = c_re[..., None] * pw_re[:, None] - c_im[..., None] * pw_im[:, None]
    ca_im = c_re[..., None] * pw_im[:, None] + c_im[..., None] * pw_re[:, None]
    kern = (jnp.einsum("gapj,gph->gjah", ca_re, bb_re, precision="highest")
            - jnp.einsum("gapj,gph->gjah", ca_im, bb_im, precision="highest"))[:, :q]
    lag = jnp.arange(q)[None, :] - jnp.arange(q)[:, None]
    toep = kern[:, jnp.clip(lag, 0, q - 1)]
    toep = jnp.where((lag >= 0)[None, :, :, None, None], toep, 0.0).transpose(0, 1, 4, 2, 3)
    rev = q - 1 - jnp.arange(q)
    pr = pw_re[:, :, rev]
    pi = pw_im[:, :, rev]
    inj_re = pr[..., None] * bb_re[:, :, None] - pi[..., None] * bb_im[:, :, None]
    inj_im = pr[..., None] * bb_im[:, :, None] + pi[..., None] * bb_re[:, :, None]
    inj_re = inj_re.transpose(0, 2, 3, 1)
    inj_im = inj_im.transpose(0, 2, 3, 1)
    out_re = ca_re[..., 1:].transpose(0, 2, 3, 1)
    out_im = (-ca_im[..., 1:]).transpose(0, 2, 3, 1)
    width = q * LANES
    n_state = per_sg * SSM_STATE

    def member_second(a):
        return jnp.moveaxis(a.reshape((n_sg, per_sg) + a.shape[1:]), 1, 2)

    toep_c = member_second(toep.reshape(SSM_GROUPS, q, SSM_GROUP, q * SSM_GROUP)).reshape(
        n_sg, width, LANES)
    inj_c = member_second(jnp.concatenate([inj_re, inj_im], axis=-1)).reshape(
        n_sg, width, 2 * SSM_STATE)
    out_c = jnp.concatenate([a.reshape(n_sg, n_state, q * SSM_GROUP) for a in (out_re, out_im)],
                            axis=1)
    aq = jnp.stack([pw_re[..., q].reshape(n_sg, n_state), pw_im[..., q].reshape(n_sg, n_state)], axis=1)
    col = jnp.arange(width, dtype=jnp.int32)
    src_out = (col // LANES) * SSM_GROUP + col % SSM_GROUP
    src_state = (col // n_state) * SSM_STATE + col % SSM_STATE
    rows = jnp.arange(LANES, dtype=jnp.int32)[:, None]
    spread_out = (rows == src_out[None, :]).astype(BF16)
    spread_state = (rows == src_state[None, :]).astype(BF16)
    return (toep_c.astype(BF16), inj_c.astype(BF16), out_c.astype(BF16), aq,
            spread_out, spread_state)


def _s5_kernel(u_ref, toepc_ref, injc_ref, outc_ref, aq_ref, spo_ref, sps_ref, y_ref,
               inj_scr, xs_scr, toep_scr, injm_scr, outm_scr, *, n_chunks):
    n_state = aq_ref.shape[2]
    width = toep_scr.shape[0]

    @pl.when(pl.program_id(1) == 0)
    def _():
        r = lax.broadcasted_iota(jnp.int32, (width, width), 0)
        c = lax.broadcasted_iota(jnp.int32, (width, width), 1)
        per_sg = LANES // SSM_GROUP
        chan_member_r = (r // SSM_GROUP) % per_sg
        chan_member_c = (c // SSM_GROUP) % per_sg
        state_member_r = (r % n_state) // SSM_STATE
        state_member_c = (c % n_state) // SSM_STATE
        full = jnp.dot(toepc_ref[0], spo_ref[...], preferred_element_type=F32)
        toep_scr[...] = jnp.where(chan_member_r == chan_member_c, full, 0.0).astype(BF16)
        full = jnp.dot(injc_ref[0], sps_ref[...], preferred_element_type=F32)
        injm_scr[...] = jnp.where(chan_member_r == state_member_c, full, 0.0).astype(BF16)
        full = jnp.dot(outc_ref[0], spo_ref[...], preferred_element_type=F32)
        outm_scr[...] = jnp.where(state_member_r == chan_member_c, full, 0.0).astype(BF16)

    u = u_ref[0]
    inj_scr[...] = jnp.dot(u, injm_scr[...], preferred_element_type=F32)
    aq_re = jnp.broadcast_to(aq_ref[0, 0:1, :], (8, n_state))
    aq_im = jnp.broadcast_to(aq_ref[0, 1:2, :], (8, n_state))
    sub = lax.broadcasted_iota(jnp.int32, (8, n_state), 0)

    def block(blk, carry):
        re, im = carry
        start = pl.multiple_of(blk * 8, 8)
        inj = inj_scr[pl.ds(start, 8), :]
        xs_re = jnp.zeros((8, n_state), F32)
        xs_im = jnp.zeros((8, n_state), F32)
        for j in range(8):
            xs_re = jnp.where(sub == j, re, xs_re)
            xs_im = jnp.where(sub == j, im, xs_im)
            in_re = jnp.broadcast_to(inj[j:j + 1, :n_state], (8, n_state))
            in_im = jnp.broadcast_to(inj[j:j + 1, n_state:], (8, n_state))
            re, im = aq_re * re - aq_im * im + in_re, aq_re * im + aq_im * re + in_im
        xs_scr[pl.ds(start, 8), :n_state] = xs_re
        xs_scr[pl.ds(start, 8), n_state:] = xs_im
        return re, im

    zero = jnp.zeros((8, n_state), F32)
    lax.fori_loop(0, n_chunks // 8, block, (zero, zero))
    y = jnp.dot(u, toep_scr[...], preferred_element_type=F32)
    y_ref[0] = y + jnp.dot(xs_scr[...].astype(BF16), outm_scr[...], preferred_element_type=F32)


def _s5_scan(u_sg, tables, batch, n_chunks):
    toep_c, inj_c, out_c, aq, spread_out, spread_state = tables
    n_sg, _, width = u_sg.shape
    n_state2 = spread_state.shape[1]

    def per_sg(shape):
        return pl.BlockSpec((1,) + shape, lambda g, b: (g, 0, 0))

    return pl.pallas_call(
        functools.partial(_s5_kernel, n_chunks=n_chunks),
        grid=(n_sg, batch),
        in_specs=[pl.BlockSpec((1, n_chunks, width), lambda g, b: (g, b, 0)),
                  per_sg(toep_c.shape[1:]), per_sg(inj_c.shape[1:]), per_sg(out_c.shape[1:]),
                  per_sg(aq.shape[1:]), _full(spread_out.shape), _full(spread_state.shape)],
        out_specs=pl.BlockSpec((1, n_chunks, width), lambda g, b: (g, b, 0)),
        out_shape=jax.ShapeDtypeStruct(u_sg.shape, F32),
        scratch_shapes=[pltpu.VMEM((n_chunks, n_state2), F32)] * 2
        + [pltpu.VMEM((width, width), BF16), pltpu.VMEM((width, n_state2), BF16),
           pltpu.VMEM((n_state2, width), BF16)],
        compiler_params=_params(2),
    )(u_sg, toep_c, inj_c, out_c, aq, spread_out, spread_state)


def _even_post_kernel(h_ref, yc_ref, ysg_ref, u_ref, perm_ref, d_ref, wglu_ref, bglu_ref, woc_ref,
                      wos_ref, g_ref, b_ref, rwh_ref, rwl_ref, rb_ref, tri_ref, *out_and_scratch):
    n_sg = ysg_ref.shape[0]
    by_pos = jnp.concatenate(
        [jnp.concatenate([ysg_ref[sg, :, s * LANES:(s + 1) * LANES] for sg in range(n_sg)], axis=1)
         for s in range(SSM_CHUNK)], axis=0)
    hi, lo = _split_bf16(by_pos)
    y_scan = (jnp.dot(perm_ref[...], hi, preferred_element_type=F32)
              + jnp.dot(perm_ref[...], lo, preferred_element_type=F32))
    y = y_scan + d_ref[...] * u_ref[...]
    z = jax.nn.gelu(y)
    glu = jnp.dot(z.astype(BF16), wglu_ref[...], preferred_element_type=F32) + bglu_ref[...]
    z = z * _sigmoid(glu)
    mix = (jnp.dot(yc_ref[...].astype(BF16), woc_ref[...], preferred_element_type=F32)
           + jnp.dot(z.astype(BF16), wos_ref[...], preferred_element_type=F32))
    _norm_and_route(DN_ALPHA * h_ref[...] + mix, g_ref, b_ref, (rwh_ref, rwl_ref, rb_ref, tri_ref),
                    *out_and_scratch)


def _even_post(h, y_conv, y_sg, u, consts):
    t = h.shape[0]
    n_sg = y_sg.shape[0]
    out_specs, out_shape = _post_outputs(t)
    consts = [_chunk_perm(ROW_TILE).T, *consts]
    return pl.pallas_call(
        _even_post_kernel,
        grid=(t // ROW_TILE,),
        in_specs=[_rows(D_MODEL), _rows(CONV_CH),
                  pl.BlockSpec((n_sg, ROW_TILE // SSM_CHUNK, SSM_CHUNK * LANES), lambda i: (0, i, 0)),
                  _rows(SSM_WIDTH)] + [_full(a.shape) for a in consts],
        out_specs=out_specs, out_shape=out_shape,
        scratch_shapes=[pltpu.VMEM((8, LANES), F32)],
        compiler_params=_params(),
    )(h, y_conv, y_sg, u, *consts)


def _qkv_rope_kernel(x_ref, w_ref, cos_ref, sin_ref, p4_ref, p16_ref, o1_ref, o4_ref, o16_ref):
    qkv = jnp.dot(x_ref[...].astype(BF16), w_ref[...], preferred_element_type=F32)
    rows = qkv.shape[0]
    cos = jnp.concatenate([cos_ref[...]] * (D_MODEL // LANES), axis=1)
    sin = jnp.concatenate([sin_ref[...]] * (D_MODEL // LANES), axis=1)
    lane = lax.broadcasted_iota(jnp.int32, (rows, D_MODEL), 1)
    low_half = (lane % HEAD_DIM) < (HEAD_DIM // 2)

    def rope(xs):
        up = pltpu.roll(xs, D_MODEL - HEAD_DIM // 2, 1)
        down = pltpu.roll(xs, HEAD_DIM // 2, 1)
        return xs * cos + jnp.where(low_half, up, down) * sin

    parts = (rope(qkv[:, :D_MODEL]) * (HEAD_DIM ** -0.5), rope(qkv[:, D_MODEL:2 * D_MODEL]),
             qkv[:, 2 * D_MODEL:])
    for which, part in enumerate(parts):
        cols = slice(which * D_MODEL, (which + 1) * D_MODEL)
        part = part.astype(BF16)
        o1_ref[:, cols] = part
        for dil, perm_ref, ref in ((DILATIONS[1], p4_ref, o4_ref), (DILATIONS[2], p16_ref, o16_ref)):
            split = jnp.dot(perm_ref[...], part, preferred_element_type=F32).astype(BF16)
            n = rows // dil
            for r in range(dil):
                ref[0, r, :, cols] = split[r * n:(r + 1) * n, :]


def _rope_tables(seq_len):
    half = HEAD_DIM // 2
    inv = ROPE_THETA ** (-jnp.arange(half, dtype=F32) / half)
    ang = jnp.arange(seq_len, dtype=F32)[:, None] * inv[None, :]
    cos = jnp.tile(jnp.cos(ang), (1, LANES // half))
    sin = jnp.sin(ang)
    sin = jnp.tile(jnp.concatenate([-sin, sin], axis=1), (1, LANES // HEAD_DIM))
    return cos, sin


def _residue_spec(dil, rows, width, tiles_per_seq):
    return pl.BlockSpec((1, dil, rows // dil, width),
                        lambda i: (i // tiles_per_seq, 0, i % tiles_per_seq, 0))


def _qkv_rope(h, w_qkv_bf, cos, sin, batch, seq_len):
    t = h.shape[0]
    tps = seq_len // QKV_TILE
    d4, d16 = DILATIONS[1], DILATIONS[2]
    width = 3 * D_MODEL
    perms = []
    for dil in (d4, d16):
        j = jnp.arange(QKV_TILE, dtype=jnp.int32)
        src = (j % (QKV_TILE // dil)) * dil + j // (QKV_TILE // dil)
        perms.append((src[:, None] == j[None, :]).astype(BF16))
    return pl.pallas_call(
        _qkv_rope_kernel,
        grid=(t // QKV_TILE,),
        in_specs=[_rows(D_MODEL, QKV_TILE), _full(w_qkv_bf.shape),
                  pl.BlockSpec((QKV_TILE, LANES), lambda i: (i % tps, 0)),
                  pl.BlockSpec((QKV_TILE, LANES), lambda i: (i % tps, 0)),
                  _full((QKV_TILE, QKV_TILE)), _full((QKV_TILE, QKV_TILE))],
        out_specs=[_rows(width, QKV_TILE), _residue_spec(d4, QKV_TILE, width, tps),
                   _residue_spec(d16, QKV_TILE, width, tps)],
        out_shape=[jax.ShapeDtypeStruct((t, width), BF16),
                   jax.ShapeDtypeStruct((batch, d4, seq_len // d4, width), BF16),
                   jax.ShapeDtypeStruct((batch, d16, seq_len // d16, width), BF16)],
        compiler_params=_params(),
    )(h, w_qkv_bf, cos, sin, *perms)


def _attn_kernel(q_ref, kp_ref, kc_ref, vp_ref, vc_ref, o_ref, lse_ref):
    blk = ATT_BLOCK
    qi = lax.broadcasted_iota(jnp.int32, (blk, 2 * blk), 0)
    kj = lax.broadcasted_iota(jnp.int32, (blk, 2 * blk), 1)
    dist = blk + qi - kj
    band = jnp.where(dist >= 0, jnp.where(dist <= blk, 0.0, NEG_BIG), NEG_BIG)
    k_min = jnp.where(pl.program_id(1) == 0, blk, 0)
    lane = lax.broadcasted_iota(jnp.int32, (blk, LANES), 1)
    low = lane < HEAD_DIM
    for j in range(ATT_BLOCKS_PER_STEP):
        rows = slice(j * blk, (j + 1) * blk)
        before = slice((j - 1) * blk, j * blk)
        bias = jnp.where(kj >= k_min, band, NEG_BIG) if j == 0 else band
        lse_tile = jnp.zeros((blk, LANES), F32)
        for hp in range(N_HEADS // 2):
            cols = slice(hp * LANES, (hp + 1) * LANES)
            q2 = q_ref[0, rows, cols]
            k_prev = kp_ref[0, :, cols] if j == 0 else kc_ref[0, before, cols]
            v_prev = vp_ref[0, :, cols] if j == 0 else vc_ref[0, before, cols]
            kk = jnp.concatenate([k_prev, kc_ref[0, rows, cols]], axis=0)
            vv = jnp.concatenate([v_prev, vc_ref[0, rows, cols]], axis=0)
            halves = []
            for hh in range(2):
                head = 2 * hp + hh
                qm = jnp.where(low if hh == 0 else jnp.logical_not(low), q2, jnp.zeros_like(q2))
                s = lax.dot_general(qm, kk, (((1,), (1,)), ((), ())),
                                    preferred_element_type=F32) + bias
                m = jnp.max(s, axis=-1, keepdims=True)
                p = jnp.exp(s - m)
                den = jnp.sum(p, axis=-1, keepdims=True)
                pv = jnp.dot(p.astype(BF16), vv, preferred_element_type=F32)
                halves.append(pv / den)
                lse_tile = jnp.where(lane == head, m + jnp.log(den), lse_tile)
            o_ref[0, rows, cols] = jnp.where(low, halves[0], halves[1]).astype(BF16)
        lse_ref[0, rows, :] = lse_tile


def _attn_pattern(qkv_sub):
    n_sub, sub_len, _ = qkv_sub.shape

    step = ATT_BLOCKS_PER_STEP * ATT_BLOCK

    def cur(which):
        return pl.BlockSpec((1, step, D_MODEL), lambda s, n: (s, n, which))

    def prev(which):
        return pl.BlockSpec((1, ATT_BLOCK, D_MODEL),
                            lambda s, n: (s, jnp.maximum(n * ATT_BLOCKS_PER_STEP - 1, 0), which))

    return pl.pallas_call(
        _attn_kernel,
        grid=(n_sub, sub_len // step),
        in_specs=[cur(0), prev(1), cur(1), prev(2), cur(2)],
        out_specs=[pl.BlockSpec((1, step, D_MODEL), lambda s, n: (s, n, 0)),
                   pl.BlockSpec((1, step, LANES), lambda s, n: (s, n, 0))],
        out_shape=[jax.ShapeDtypeStruct((n_sub, sub_len, D_MODEL), BF16),
                   jax.ShapeDtypeStruct((n_sub, sub_len, LANES), F32)],
        compiler_params=_params(2),
    )(*([qkv_sub] * 5))


def _odd_post_kernel(h_ref, o1_ref, l1_ref, o4_ref, l4_ref, o16_ref, l16_ref, expand_ref, wo_ref,
                     g_ref, b_ref, rwh_ref, rwl_ref, rb_ref, tri_ref, hn_ref, hnb_ref, idx_ref,
                     gate_ref, cnt_ref, os4_ref, ls4_ref, os16_ref, ls16_ref, seen_ref):
    rows = h_ref.shape[0]
    n_col = D_MODEL // LANES
    for dil, o_ref, l_ref, os_ref, ls_ref in ((DILATIONS[1], o4_ref, l4_ref, os4_ref, ls4_ref),
                                              (DILATIONS[2], o16_ref, l16_ref, os16_ref, ls16_ref)):
        for r in range(dil):
            sel = pl.ds(r, rows // dil, stride=dil)
            ls_ref[sel, :] = l_ref[0, r]
            for c in range(n_col):
                os_ref[c, sel, :] = o_ref[0, r, :, c * LANES:(c + 1) * LANES].astype(F32)
    lses = (l1_ref[...], ls4_ref[...], ls16_ref[...])
    outs = (o1_ref[...].astype(F32),
            jnp.concatenate([os4_ref[c] for c in range(n_col)], axis=1),
            jnp.concatenate([os16_ref[c] for c in range(n_col)], axis=1))
    mx = jnp.maximum(jnp.maximum(lses[0], lses[1]), lses[2])
    es = [jnp.exp(l - mx) for l in lses]
    den = es[0] + es[1] + es[2]
    o = jnp.zeros((rows, D_MODEL), F32)
    for e, out in zip(es, outs):
        hi, lo = _split_bf16(e / den)
        w = jnp.dot(jnp.concatenate([hi, lo], axis=1), expand_ref[...], preferred_element_type=F32)
        o = o + w * out
    mix = jnp.dot(o.astype(BF16), wo_ref[...], preferred_element_type=F32)
    _norm_and_route(DN_ALPHA * h_ref[...] + mix, g_ref, b_ref, (rwh_ref, rwl_ref, rb_ref, tri_ref),
                    hn_ref, hnb_ref, idx_ref, gate_ref, cnt_ref, seen_ref)


def _odd_post(h, pattern_outs, consts, seq_len):
    t = h.shape[0]
    tps = seq_len // ROW_TILE
    (o1, l1), (o4, l4), (o16, l16) = pattern_outs
    d4, d16 = DILATIONS[1], DILATIONS[2]
    out_specs, out_shape = _post_outputs(t)
    return pl.pallas_call(
        _odd_post_kernel,
        grid=(t // ROW_TILE,),
        in_specs=[_rows(D_MODEL), _rows(D_MODEL), _rows(LANES),
                  _residue_spec(d4, ROW_TILE, D_MODEL, tps), _residue_spec(d4, ROW_TILE, LANES, tps),
                  _residue_spec(d16, ROW_TILE, D_MODEL, tps), _residue_spec(d16, ROW_TILE, LANES, tps)]
        + [_full(a.shape) for a in consts],
        out_specs=out_specs, out_shape=out_shape,
        scratch_shapes=[pltpu.VMEM((D_MODEL // LANES, ROW_TILE, LANES), F32),
                        pltpu.VMEM((ROW_TILE, LANES), F32)] * 2 + [pltpu.VMEM((8, LANES), F32)],
        compiler_params=_params(),
    )(h, o1, l1, o4, l4, o16, l16, *consts)


def _attention_layer(h, w_qkv, w_o, cos, sin, consts, batch, seq_len):
    t = h.shape[0]
    qkv1, qkv4, qkv16 = _qkv_rope(h, w_qkv.astype(BF16), cos, sin, batch, seq_len)
    outs = []
    for dil, qkv in zip(DILATIONS, (qkv1, qkv4, qkv16)):
        sub_len = seq_len // dil
        o, lse = _attn_pattern(qkv.reshape(batch * dil, sub_len, 3 * D_MODEL))
        if dil == 1:
            outs.append((o.reshape(t, D_MODEL), lse.reshape(t, LANES)))
        else:
            outs.append((o.reshape(batch, dil, sub_len, D_MODEL),
                         lse.reshape(batch, dil, sub_len, LANES)))
    head_of_lane = jnp.arange(D_MODEL, dtype=jnp.int32) // HEAD_DIM
    expand = (jnp.arange(LANES, dtype=jnp.int32)[:, None] == head_of_lane[None, :]).astype(BF16)
    expand = jnp.concatenate([expand, expand], axis=0)
    return _odd_post(h, outs, [expand, w_o.astype(BF16), *consts], seq_len)


def _moe_kernel(be_ref, nused_ref, x_ref, wgu_ref, bgu_ref, wd_ref, bd_ref, y_ref,
                wgu_bf_ref, wd_bf_ref):
    i = pl.program_id(0)
    new_expert = jnp.logical_or(i == 0, be_ref[i] != be_ref[jnp.maximum(i - 1, 0)])

    @pl.when(new_expert)
    def _():
        wgu_bf_ref[...] = wgu_ref[0, 0].astype(BF16)
        wd_bf_ref[...] = wd_ref[0, 0].astype(BF16)

    @pl.when(i < nused_ref[0])
    def _():
        hid = jnp.dot(x_ref[...], wgu_bf_ref[...], preferred_element_type=F32) + bgu_ref[0]
        gate = jnp.minimum(hid[:, :D_MODEL], SWIGLU_LIMIT)
        lin = jnp.clip(hid[:, D_MODEL:], -SWIGLU_LIMIT, SWIGLU_LIMIT)
        act = (lin + 1.0) * (gate * _sigmoid(SWIGLU_ALPHA * gate))
        y = jnp.dot(act.astype(BF16), wd_bf_ref[...], preferred_element_type=F32) + bd_ref[0]
        y_ref[...] = y.astype(BF16)

    @pl.when(i >= nused_ref[0])
    def _():
        y_ref[...] = jnp.zeros_like(y_ref)


def _moe_experts(xs, block_e, n_used, layer, w_gu, b_gu, w_down, b_down):
    n_rows = xs.shape[0]
    grid_spec = pltpu.PrefetchScalarGridSpec(
        num_scalar_prefetch=2,
        grid=(n_rows // MOE_ROWS,),
        in_specs=[pl.BlockSpec((MOE_ROWS, D_MODEL), lambda i, be, nu: (i, 0)),
                  pl.BlockSpec((1, 1, D_MODEL, 2 * D_MODEL), lambda i, be, nu: (layer, be[i], 0, 0)),
                  pl.BlockSpec((1, 1, 2 * D_MODEL), lambda i, be, nu: (be[i], 0, 0)),
                  pl.BlockSpec((1, 1, D_MODEL, D_MODEL), lambda i, be, nu: (layer, be[i], 0, 0)),
                  pl.BlockSpec((1, 1, D_MODEL), lambda i, be, nu: (be[i], 0, 0))],
        out_specs=pl.BlockSpec((MOE_ROWS, D_MODEL), lambda i, be, nu: (i, 0)),
        scratch_shapes=[pltpu.VMEM((D_MODEL, 2 * D_MODEL), BF16), pltpu.VMEM((D_MODEL, D_MODEL), BF16)],
    )
    return pl.pallas_call(
        _moe_kernel,
        grid_spec=grid_spec,
        out_shape=jax.ShapeDtypeStruct((n_rows, D_MODEL), BF16),
        compiler_params=_params(),
    )(block_e, n_used, xs, w_gu, b_gu[:, None, :], w_down, b_down[:, None, :])


def _routing_tables(idx_tile, counts, t):
    n_assign = t * TOP_K
    top_idx = idx_tile[:TOP_K]
    rank = idx_tile[TOP_K:]
    counts = counts[0, :N_EXPERTS].astype(jnp.int32)
    padded = (counts + MOE_ROWS - 1) // MOE_ROWS * MOE_ROWS
    pad_end = jnp.cumsum(padded)
    pad_start = pad_end - padded
    start_of = jnp.zeros_like(rank)
    for e in range(N_EXPERTS):
        start_of = jnp.where(top_idx == e, pad_start[e], start_of)
    dest = (start_of + rank).reshape(-1)
    n_blocks = n_assign // MOE_ROWS + N_EXPERTS
    block_start = jnp.arange(n_blocks, dtype=jnp.int32) * MOE_ROWS
    block_e = jnp.minimum(jnp.sum((block_start[:, None] >= pad_end[None, :]).astype(jnp.int32), axis=1),
                          N_EXPERTS - 1)
    n_used = (pad_end[-1:] // MOE_ROWS).astype(jnp.int32)
    keys = top_idx * t + jnp.arange(t, dtype=jnp.int32)[None, :]
    tok_sorted = lax.sort(keys.reshape(-1)) % t
    starts = jnp.cumsum(counts) - counts
    rows = jnp.arange(n_blocks * MOE_ROWS, dtype=jnp.int32).reshape(n_blocks, MOE_ROWS)
    shift = (starts - pad_start)[block_e][:, None]
    live_end = (pad_start + counts)[block_e][:, None]
    compact = jnp.clip(rows + shift, 0, n_assign - 1)
    row_tok = jnp.where(rows < live_end, tok_sorted.at[compact].get(mode="promise_in_bounds"),
                        rows % t).reshape(-1)
    return row_tok, dest, block_e, n_used


def _ffn_ln_kernel(h_ref, y4_ref, gate_ref, g_ref, b_ref, o_ref):
    gates = gate_ref[...]
    acc = DN_ALPHA * h_ref[...]
    for k in range(TOP_K):
        acc = acc + gates[:, k:k + 1] * y4_ref[k].astype(F32)
    o_ref[...] = _layer_norm(acc, g_ref[...], b_ref[...])


def _ffn_ln(h, y4, gates, g, b):
    t = h.shape[0]
    return pl.pallas_call(
        _ffn_ln_kernel,
        grid=(t // ROW_TILE,),
        in_specs=[_rows(D_MODEL), pl.BlockSpec((TOP_K, ROW_TILE, D_MODEL), lambda i: (0, i, 0)),
                  _rows(LANES), _full(g.shape), _full(b.shape)],
        out_specs=_rows(D_MODEL),
        out_shape=jax.ShapeDtypeStruct((t, D_MODEL), F32),
        compiler_params=_params(),
    )(h, y4, gates, g, b)


def _moe_layer(hn, hn_bf, idx_tile, gate_tile, counts, layer, w_gu, b_gu, w_down, b_down, ln_g, ln_b):
    t = hn.shape[0]
    row_tok, dest, block_e, n_used = _routing_tables(idx_tile, counts, t)
    xs = hn_bf.at[row_tok].get(mode="promise_in_bounds")
    ys = _moe_experts(xs, block_e, n_used, layer, w_gu, b_gu, w_down, b_down)
    y4 = ys.at[dest].get(mode="promise_in_bounds").reshape(TOP_K, t, D_MODEL)
    return _ffn_ln(hn, y4, gate_tile, ln_g[None, :], ln_b[None, :])


def _router_consts(router_w, router_b):
    w = jnp.zeros((D_MODEL, LANES), F32).at[:, :N_EXPERTS].set(router_w)
    w_hi, w_lo = _split_bf16(w)
    b = jnp.full((1, LANES), NEG_BIG, F32).at[0, :N_EXPERTS].set(router_b)
    r = jnp.arange(ROW_TILE, dtype=jnp.int32)
    tri = (r[None, :] < r[:, None]).astype(BF16)
    return jnp.concatenate([w_hi, w_lo], axis=1), w_hi, b, tri


def kernel(x, hy_w_in, conv_w, ssm_a_re, ssm_a_im, ssm_log_dt, ssm_b_re, ssm_b_im, ssm_c_re,
           ssm_c_im, ssm_d, ssm_w_glu, ssm_b_glu, hy_w_out, att_w_qkv, att_w_o, ln_mix_g,
           ln_mix_b, ln_ffn_g, ln_ffn_b, router_w, router_b, expert_w_gu, expert_b_gu,
           expert_w_down, expert_b_down):
    batch, seq_len, _ = x.shape
    t = batch * seq_len
    h = x.reshape(t, D_MODEL)
    cos, sin = _rope_tables(seq_len)
    for layer in range(DEPTH):
        i = layer // 2
        consts = [ln_mix_g[layer][None, :], ln_mix_b[layer][None, :],
                  *_router_consts(router_w[layer], router_b[layer])]
        if layer % 2 == 0:
            y_conv, u, u_sg = _inproj_conv(h, hy_w_in[i].astype(BF16), conv_w[i], seq_len)
            tables = _s5_tables(ssm_a_re[i], ssm_a_im[i], ssm_log_dt[i], ssm_b_re[i], ssm_b_im[i],
                                ssm_c_re[i], ssm_c_im[i])
            y_sg = _s5_scan(u_sg, tables, batch, seq_len // SSM_CHUNK)
            w_out = hy_w_out[i].astype(BF16)
            hn, hn_bf, idx_tile, gate_tile, counts = _even_post(
                h, y_conv, y_sg, u,
                [ssm_d[i].reshape(1, SSM_WIDTH), ssm_w_glu[i].astype(BF16), ssm_b_glu[i][None, :],
                 w_out[:CONV_CH], w_out[CONV_CH:], *consts])
        else:
            hn, hn_bf, idx_tile, gate_tile, counts = _attention_layer(
                h, att_w_qkv[i], att_w_o[i], cos, sin, consts, batch, seq_len)
        h = _moe_layer(hn, hn_bf, idx_tile, gate_tile, counts, layer, expert_w_gu, expert_b_gu[layer],
                       expert_w_down, expert_b_down[layer], ln_ffn_g[layer], ln_ffn_b[layer])
    return h.reshape(batch, seq_len, D_MODEL)
```

```python
import functools

import jax
import jax.numpy as jnp
from jax import lax
from jax.experimental import pallas as pl
from jax.experimental.pallas import tpu as pltpu

F32 = jnp.float32
BF16 = jnp.bfloat16

D_MODEL = 1024
DEPTH = 4
CONV_CH = 512
SSM_WIDTH = 512
SSM_GROUP = 16
SSM_GROUPS = 32
SSM_STATE = 64
N_HEADS = 16
HEAD_DIM = 64
ROPE_THETA = 10000.0
DILATIONS = (1, 4, 16)
ATT_BLOCK = 128
ATT_BLOCKS_PER_STEP = 4
N_EXPERTS = 32
TOP_K = 4
SWIGLU_LIMIT = 7.0
SWIGLU_ALPHA = 1.702
DN_ALPHA = (2 * DEPTH) ** 0.25
LN_EPS = 1e-5

LANES = 128
ROW_TILE = 512
QKV_TILE = 256
SSM_CHUNK = 8
MOE_ROWS = 512
VMEM_LIMIT = 56 * 1024 * 1024
NEG_BIG = -1e30


def _params(n_axes=1):
    return pltpu.CompilerParams(dimension_semantics=("arbitrary",) * n_axes,
                                vmem_limit_bytes=VMEM_LIMIT)


def _full(shape):
    return pl.BlockSpec(shape, lambda *_: (0,) * len(shape))


def _rows(width, tile=ROW_TILE):
    return pl.BlockSpec((tile, width), lambda i: (i, 0))


def _layer_norm(x, g, b):
    mu = jnp.mean(x, axis=-1, keepdims=True)
    xc = x - mu
    var = jnp.mean(xc * xc, axis=-1, keepdims=True)
    return xc * lax.rsqrt(var + LN_EPS) * g + b


def _sigmoid(x):
    return 1.0 / (1.0 + jnp.exp(-x))


def _split_bf16(x):
    hi = x.astype(BF16)
    return hi, (x - hi.astype(F32)).astype(BF16)


def _route(hn, rw_both_ref, rw_hi_ref, rb_ref, tri_ref, idx_ref, gate_ref, cnt_ref, seen_ref):
    @pl.when(pl.program_id(0) == 0)
    def _():
        seen_ref[...] = jnp.zeros_like(seen_ref)

    hi, lo = _split_bf16(hn)
    both = jnp.dot(hi, rw_both_ref[...], preferred_element_type=F32)
    logits = (both[:, :LANES] + both[:, LANES:]
              + jnp.dot(lo, rw_hi_ref[...], preferred_element_type=F32)
              + rb_ref[...])
    lane = lax.broadcasted_iota(jnp.int32, logits.shape, 1)
    lane_f = lane.astype(F32)
    idx_f = jnp.zeros(logits.shape, F32)
    val_tile = jnp.zeros(logits.shape, F32)
    top0 = None
    den = None
    firsts = []
    for k in range(TOP_K):
        mx = jnp.max(logits, axis=-1, keepdims=True)
        first = jnp.min(jnp.where(logits == mx, lane_f, float(LANES)), axis=-1, keepdims=True)
        if k == 0:
            top0 = mx
        e = jnp.exp(mx - top0)
        den = e if k == 0 else den + e
        idx_f = jnp.where(lane == k, first, idx_f)
        val_tile = jnp.where(lane == k, e, val_tile)
        logits = jnp.where(lane_f == first, -jnp.inf, logits)
        firsts.append(first)
    gate_ref[...] = val_tile / den
    picked = jnp.where(logits == -jnp.inf, 1.0, 0.0)
    seen = seen_ref[0:1, :]
    before = jnp.dot(tri_ref[...], picked.astype(BF16), preferred_element_type=F32) + seen
    for k, first in enumerate(firsts):
        rank = jnp.sum(jnp.where(lane_f == first, before, 0.0), axis=-1, keepdims=True)
        idx_f = jnp.where(lane == TOP_K + k, rank, idx_f)
    idx_ref[...] = idx_f.T[:2 * TOP_K, :].astype(jnp.int32)
    seen = seen + jnp.sum(picked, axis=0, keepdims=True)
    seen_ref[...] = jnp.broadcast_to(seen, seen_ref.shape)
    cnt_ref[...] = jnp.broadcast_to(seen, cnt_ref.shape)


def _post_outputs(t):
    specs = [_rows(D_MODEL), _rows(D_MODEL), pl.BlockSpec((2 * TOP_K, ROW_TILE), lambda i: (0, i)),
             _rows(LANES), _full((8, LANES))]
    shapes = [jax.ShapeDtypeStruct((t, D_MODEL), F32), jax.ShapeDtypeStruct((t, D_MODEL), BF16),
              jax.ShapeDtypeStruct((2 * TOP_K, t), jnp.int32), jax.ShapeDtypeStruct((t, LANES), F32),
              jax.ShapeDtypeStruct((8, LANES), F32)]
    return specs, shapes


def _norm_and_route(pre, g_ref, b_ref, route_refs, hn_ref, hnb_ref, idx_ref, gate_ref, cnt_ref,
                    seen_ref):
    hn = _layer_norm(pre, g_ref[...], b_ref[...])
    hn_ref[...] = hn
    hnb_ref[...] = hn.astype(BF16)
    _route(hn, *route_refs, idx_ref, gate_ref, cnt_ref, seen_ref)


def _chunk_perm(rows):
    j = jnp.arange(rows, dtype=jnp.int32)
    n = rows // SSM_CHUNK
    src = (j % n) * SSM_CHUNK + j // n
    return (src[:, None] == j[None, :]).astype(BF16)


def _inproj_conv_kernel(x_ref, w_ref, cw_ref, perm_ref, yconv_ref, u_ref, usg_ref, carry_ref,
                        wbf_ref, *, tiles_per_seq):
    @pl.when(pl.program_id(0) == 0)
    def _():
        wbf_ref[...] = w_ref[...].astype(BF16)

    @pl.when(pl.program_id(0) % tiles_per_seq == 0)
    def _():
        carry_ref[...] = jnp.zeros_like(carry_ref)

    proj = jnp.dot(x_ref[...].astype(BF16), wbf_ref[...], preferred_element_type=F32)
    gate_b = proj[:, :CONV_CH]
    gate_c = proj[:, CONV_CH:2 * CONV_CH]
    hid = proj[:, 2 * CONV_CH:3 * CONV_CH]
    v = gate_c * hid
    rows = v.shape[0]
    row = lax.broadcasted_iota(jnp.int32, v.shape, 0)
    prev1 = carry_ref[7:8, :]
    prev2 = carry_ref[6:7, :]
    vm1 = jnp.where(row == 0, prev1, pltpu.roll(v, 1, 0))
    vm2 = jnp.where(row == 0, prev2, jnp.where(row == 1, prev1, pltpu.roll(v, 2, 0)))
    conv = cw_ref[0:1, :] * vm2 + cw_ref[1:2, :] * vm1 + cw_ref[2:3, :] * v
    yconv_ref[...] = gate_b * conv
    u = proj[:, 3 * CONV_CH:]
    u_ref[...] = u
    carry_ref[...] = v[rows - 8:, :]
    by_pos = jnp.dot(perm_ref[...], u.astype(BF16), preferred_element_type=F32).astype(BF16)
    n = rows // SSM_CHUNK
    for sg in range(SSM_WIDTH // LANES):
        for s in range(SSM_CHUNK):
            usg_ref[sg, :, s * LANES:(s + 1) * LANES] = by_pos[s * n:(s + 1) * n,
                                                               sg * LANES:(sg + 1) * LANES]


def _inproj_conv(h, w_in, conv_w, seq_len):
    t = h.shape[0]
    n_sg = SSM_WIDTH // LANES
    perm = _chunk_perm(ROW_TILE)
    return pl.pallas_call(
        functools.partial(_inproj_conv_kernel, tiles_per_seq=seq_len // ROW_TILE),
        grid=(t // ROW_TILE,),
        in_specs=[_rows(D_MODEL), _full(w_in.shape), _full(conv_w.shape), _full(perm.shape)],
        out_specs=[_rows(CONV_CH), _rows(SSM_WIDTH),
                   pl.BlockSpec((n_sg, ROW_TILE // SSM_CHUNK, SSM_CHUNK * LANES), lambda i: (0, i, 0))],
        out_shape=[jax.ShapeDtypeStruct((t, CONV_CH), F32),
                   jax.ShapeDtypeStruct((t, SSM_WIDTH), F32),
                   jax.ShapeDtypeStruct((n_sg, t // SSM_CHUNK, SSM_CHUNK * LANES), BF16)],
        scratch_shapes=[pltpu.VMEM((8, CONV_CH), F32), pltpu.VMEM(w_in.shape, BF16)],
        compiler_params=_params(),
    )(h, w_in, conv_w, perm)


def _s5_tables(a_re, a_im, log_dt, b_re, b_im, c_re, c_im):
    q = SSM_CHUNK
    per_sg = LANES // SSM_GROUP
    n_sg = SSM_GROUPS // per_sg
    lam_re = jnp.minimum(a_re, -1e-4)
    lam_im = a_im
    dt = jnp.exp(log_dt)[:, None]
    mag = jnp.exp(lam_re * dt)
    ab_re = mag * jnp.cos(lam_im * dt)
    ab_im = mag * jnp.sin(lam_im * dt)
    nr, ni = ab_re - 1.0, ab_im
    den = lam_re * lam_re + lam_im * lam_im
    coef_re = ((nr * lam_re + ni * lam_im) / den)[..., None]
    coef_im = ((ni * lam_re - nr * lam_im) / den)[..., None]
    bb_re = coef_re * b_re - coef_im * b_im
    bb_im = coef_re * b_im + coef_im * b_re
    j = jnp.arange(q + 1, dtype=F32)
    pmag = jnp.exp((lam_re * dt)[..., None] * j)
    pang = (lam_im * dt)[..., None] * j
    pw_re = pmag * jnp.cos(pang)
    pw_im = pmag * jnp.sin(pang)
    ca_re = c_re[..., None] * pw_re[:, None] - c_im[..., None] * pw_im[:, None]
    ca_im = c_re[..., None] * pw_im[:, None] + c_im[..., None] * pw_re[:, None]
    kern = (jnp.einsum("gapj,gph->gjah", ca_re, bb_re, precision="highest")
            - jnp.einsum("gapj,gph->gjah", ca_im, bb_im, precision="highest"))[:, :q]
    lag = jnp.arange(q)[None, :] - jnp.arange(q)[:, None]
    toep = kern[:, jnp.clip(lag, 0, q - 1)]
    toep = jnp.where((lag >= 0)[None, :, :, None, None], toep, 0.0).transpose(0, 1, 4, 2, 3)
    rev = q - 1 - jnp.arange(q)
    pr = pw_re[:, :, rev]
    pi = pw_im[:, :, rev]
    inj_re = pr[..., None] * bb_re[:, :, None] - pi[..., None] * bb_im[:, :, None]
    inj_im = pr[..., None] * bb_im[:, :, None] + pi[..., None] * bb_re[:, :, None]
    inj_re = inj_re.transpose(0, 2, 3, 1)
    inj_im = inj_im.transpose(0, 2, 3, 1)
    out_re = ca_re[..., 1:].transpose(0, 2, 3, 1)
    out_im = (-ca_im[..., 1:]).transpose(0, 2, 3, 1)
    width = q * LANES
    n_state = per_sg * SSM_STATE

    def member_second(a):
        return jnp.moveaxis(a.reshape((n_sg, per_sg) + a.shape[1:]), 1, 2)

    toep_c = member_second(toep.reshape(SSM_GROUPS, q, SSM_GROUP, q * SSM_GROUP)).reshape(
        n_sg, width, LANES)
    inj_c = member_second(jnp.concatenate([inj_re, inj_im], axis=-1)).reshape(
        n_sg, width, 2 * SSM_STATE)
    out_c = jnp.concatenate([a.reshape(n_sg, n_state, q * SSM_GROUP) for a in (out_re, out_im)],
                            axis=1)
    aq = jnp.stack([pw_re[..., q].reshape(n_sg, n_state), pw_im[..., q].reshape(n_sg, n_state)], axis=1)
    col = jnp.arange(width, dtype=jnp.int32)
    src_out = (col // LANES) * SSM_GROUP + col % SSM_GROUP
    src_state = (col // n_state) * SSM_STATE + col % SSM_STATE
    rows = jnp.arange(LANES, dtype=jnp.int32)[:, None]
    spread_out = (rows == src_out[None, :]).astype(BF16)
    spread_state = (rows == src_state[None, :]).astype(BF16)
    return (toep_c.astype(BF16), inj_c.astype(BF16), out_c.astype(BF16), aq,
            spread_out, spread_state)


def _s5_kernel(u_ref, toepc_ref, injc_ref, outc_ref, aq_ref, spo_ref, sps_ref, y_ref,
               inj_scr, xs_scr, toep_scr, injm_scr, outm_scr, *, n_chunks):
    n_state = aq_ref.shape[2]
    width = toep_scr.shape[0]

    @pl.when(pl.program_id(1) == 0)
    def _():
        r = lax.broadcasted_iota(jnp.int32, (width, width), 0)
        c = lax.broadcasted_iota(jnp.int32, (width, width), 1)
        per_sg = LANES // SSM_GROUP
        chan_member_r = (r // SSM_GROUP) % per_sg
        chan_member_c = (c // SSM_GROUP) % per_sg
        state_member_r = (r % n_state) // SSM_STATE
        state_member_c = (c % n_state) // SSM_STATE
        full = jnp.dot(toepc_ref[0], spo_ref[...], preferred_element_type=F32)
        toep_scr[...] = jnp.where(chan_member_r == chan_member_c, full, 0.0).astype(BF16)
        full = jnp.dot(injc_ref[0], sps_ref[...], preferred_element_type=F32)
        injm_scr[...] = jnp.where(chan_member_r == state_member_c, full, 0.0).astype(BF16)
        full = jnp.dot(outc_ref[0], spo_ref[...], preferred_element_type=F32)
        outm_scr[...] = jnp.where(state_member_r == chan_member_c, full, 0.0).astype(BF16)

    u = u_ref[0]
    inj_scr[...] = jnp.dot(u, injm_scr[...], preferred_element_type=F32)
    aq_re = jnp.broadcast_to(aq_ref[0, 0:1, :], (8, n_state))
    aq_im = jnp.broadcast_to(aq_ref[0, 1:2, :], (8, n_state))
    sub = lax.broadcasted_iota(jnp.int32, (8, n_state), 0)

    def block(blk, carry):
        re, im = carry
        start = pl.multiple_of(blk * 8, 8)
        inj = inj_scr[pl.ds(start, 8), :]
        xs_re = jnp.zeros((8, n_state), F32)
        xs_im = jnp.zeros((8, n_state), F32)
        for j in range(8):
            xs_re = jnp.where(sub == j, re, xs_re)
            xs_im = jnp.where(sub == j, im, xs_im)
            in_re = jnp.broadcast_to(inj[j:j + 1, :n_state], (8, n_state))
            in_im = jnp.broadcast_to(inj[j:j + 1, n_state:], (8, n_state))
            re, im = aq_re * re - aq_im * im + in_re, aq_re * im + aq_im * re + in_im
        xs_scr[pl.ds(start, 8), :n_state] = xs_re
        xs_scr[pl.ds(start, 8), n_state:] = xs_im
        return re, im

    zero = jnp.zeros((8, n_state), F32)
    lax.fori_loop(0, n_chunks // 8, block, (zero, zero))
    y = jnp.dot(u, toep_scr[...], preferred_element_type=F32)
    y_ref[0] = y + jnp.dot(xs_scr[...].astype(BF16), outm_scr[...], preferred_element_type=F32)


def _s5_scan(u_sg, tables, batch, n_chunks):
    toep_c, inj_c, out_c, aq, spread_out, spread_state = tables
    n_sg, _, width = u_sg.shape
    n_state2 = spread_state.shape[1]

    def per_sg(shape):
        return pl.BlockSpec((1,) + shape, lambda g, b: (g, 0, 0))

    return pl.pallas_call(
        functools.partial(_s5_kernel, n_chunks=n_chunks),
        grid=(n_sg, batch),
        in_specs=[pl.BlockSpec((1, n_chunks, width), lambda g, b: (g, b, 0)),
                  per_sg(toep_c.shape[1:]), per_sg(inj_c.shape[1:]), per_sg(out_c.shape[1:]),
                  per_sg(aq.shape[1:]), _full(spread_out.shape), _full(spread_state.shape)],
        out_specs=pl.BlockSpec((1, n_chunks, width), lambda g, b: (g, b, 0)),
        out_shape=jax.ShapeDtypeStruct(u_sg.shape, F32),
        scratch_shapes=[pltpu.VMEM((n_chunks, n_state2), F32)] * 2
        + [pltpu.VMEM((width, width), BF16), pltpu.VMEM((width, n_state2), BF16),
           pltpu.VMEM((n_state2, width), BF16)],
        compiler_params=_params(2),
    )(u_sg, toep_c, inj_c, out_c, aq, spread_out, spread_state)


def _even_post_kernel(h_ref, yc_ref, ysg_ref, u_ref, perm_ref, d_ref, wglu_ref, bglu_ref, woc_ref,
                      wos_ref, g_ref, b_ref, rwh_ref, rwl_ref, rb_ref, tri_ref, *out_and_scratch):
    n_sg = ysg_ref.shape[0]
    by_pos = jnp.concatenate(
        [jnp.concatenate([ysg_ref[sg, :, s * LANES:(s + 1) * LANES] for sg in range(n_sg)], axis=1)
         for s in range(SSM_CHUNK)], axis=0)
    hi, lo = _split_bf16(by_pos)
    y_scan = (jnp.dot(perm_ref[...], hi, preferred_element_type=F32)
              + jnp.dot(perm_ref[...], lo, preferred_element_type=F32))
    y = y_scan + d_ref[...] * u_ref[...]
    z = jax.nn.gelu(y)
    glu = jnp.dot(z.astype(BF16), wglu_ref[...], preferred_element_type=F32) + bglu_ref[...]
    z = z * _sigmoid(glu)
    mix = (jnp.dot(yc_ref[...].astype(BF16), woc_ref[...], preferred_element_type=F32)
           + jnp.dot(z.astype(BF16), wos_ref[...], preferred_element_type=F32))
    _norm_and_route(DN_ALPHA * h_ref[...] + mix, g_ref, b_ref, (rwh_ref, rwl_ref, rb_ref, tri_ref),
                    *out_and_scratch)


def _even_post(h, y_conv, y_sg, u, consts):
    t = h.shape[0]
    n_sg = y_sg.shape[0]
    out_specs, out_shape = _post_outputs(t)
    consts = [_chunk_perm(ROW_TILE).T, *consts]
    return pl.pallas_call(
        _even_post_kernel,
        grid=(t // ROW_TILE,),
        in_specs=[_rows(D_MODEL), _rows(CONV_CH),
                  pl.BlockSpec((n_sg, ROW_TILE // SSM_CHUNK, SSM_CHUNK * LANES), lambda i: (0, i, 0)),
                  _rows(SSM_WIDTH)] + [_full(a.shape) for a in consts],
        out_specs=out_specs, out_shape=out_shape,
        scratch_shapes=[pltpu.VMEM((8, LANES), F32)],
        compiler_params=_params(),
    )(h, y_conv, y_sg, u, *consts)


def _qkv_rope_kernel(x_ref, w_ref, cos_ref, sin_ref, p4_ref, p16_ref, o1_ref, o4_ref, o16_ref,
                     wbf_ref):
    @pl.when(pl.program_id(0) == 0)
    def _():
        wbf_ref[...] = w_ref[...].astype(BF16)

    qkv = jnp.dot(x_ref[...].astype(BF16), wbf_ref[...], preferred_element_type=F32)
    rows = qkv.shape[0]
    cos = jnp.concatenate([cos_ref[...]] * (D_MODEL // LANES), axis=1)
    sin = jnp.concatenate([sin_ref[...]] * (D_MODEL // LANES), axis=1)
    lane = lax.broadcasted_iota(jnp.int32, (rows, D_MODEL), 1)
    low_half = (lane % HEAD_DIM) < (HEAD_DIM // 2)

    def rope(xs):
        up = pltpu.roll(xs, D_MODEL - HEAD_DIM // 2, 1)
        down = pltpu.roll(xs, HEAD_DIM // 2, 1)
        return xs * cos + jnp.where(low_half, up, down) * sin

    parts = (rope(qkv[:, :D_MODEL]) * (HEAD_DIM ** -0.5), rope(qkv[:, D_MODEL:2 * D_MODEL]),
             qkv[:, 2 * D_MODEL:])
    for which, part in enumerate(parts):
        cols = slice(which * D_MODEL, (which + 1) * D_MODEL)
        part = part.astype(BF16)
        o1_ref[:, cols] = part
        for dil, perm_ref, ref in ((DILATIONS[1], p4_ref, o4_ref), (DILATIONS[2], p16_ref, o16_ref)):
            split = jnp.dot(perm_ref[...], part, preferred_element_type=F32).astype(BF16)
            n = rows // dil
            for r in range(dil):
                ref[0, r, :, cols] = split[r * n:(r + 1) * n, :]


def _rope_tables(seq_len):
    half = HEAD_DIM // 2
    inv = ROPE_THETA ** (-jnp.arange(half, dtype=F32) / half)
    ang = jnp.arange(seq_len, dtype=F32)[:, None] * inv[None, :]
    cos = jnp.tile(jnp.cos(ang), (1, LANES // half))
    sin = jnp.sin(ang)
    sin = jnp.tile(jnp.concatenate([-sin, sin], axis=1), (1, LANES // HEAD_DIM))
    return cos, sin


def _residue_spec(dil, rows, width, tiles_per_seq):
    return pl.BlockSpec((1, dil, rows // dil, width),
                        lambda i: (i // tiles_per_seq, 0, i % tiles_per_seq, 0))


def _qkv_rope(h, w_qkv, cos, sin, batch, seq_len):
    t = h.shape[0]
    tps = seq_len // QKV_TILE
    d4, d16 = DILATIONS[1], DILATIONS[2]
    width = 3 * D_MODEL
    perms = []
    for dil in (d4, d16):
        j = jnp.arange(QKV_TILE, dtype=jnp.int32)
        src = (j % (QKV_TILE // dil)) * dil + j // (QKV_TILE // dil)
        perms.append((src[:, None] == j[None, :]).astype(BF16))
    return pl.pallas_call(
        _qkv_rope_kernel,
        grid=(t // QKV_TILE,),
        in_specs=[_rows(D_MODEL, QKV_TILE), _full(w_qkv.shape),
                  pl.BlockSpec((QKV_TILE, LANES), lambda i: (i % tps, 0)),
                  pl.BlockSpec((QKV_TILE, LANES), lambda i: (i % tps, 0)),
                  _full((QKV_TILE, QKV_TILE)), _full((QKV_TILE, QKV_TILE))],
        out_specs=[_rows(width, QKV_TILE), _residue_spec(d4, QKV_TILE, width, tps),
                   _residue_spec(d16, QKV_TILE, width, tps)],
        out_shape=[jax.ShapeDtypeStruct((t, width), BF16),
                   jax.ShapeDtypeStruct((batch, d4, seq_len // d4, width), BF16),
                   jax.ShapeDtypeStruct((batch, d16, seq_len // d16, width), BF16)],
        scratch_shapes=[pltpu.VMEM(w_qkv.shape, BF16)],
        compiler_params=_params(),
    )(h, w_qkv, cos, sin, *perms)


def _attn_kernel(q_ref, kp_ref, kc_ref, vp_ref, vc_ref, o_ref, lse_ref):
    blk = ATT_BLOCK
    qi = lax.broadcasted_iota(jnp.int32, (blk, 2 * blk), 0)
    kj = lax.broadcasted_iota(jnp.int32, (blk, 2 * blk), 1)
    dist = blk + qi - kj
    band = jnp.where(dist >= 0, jnp.where(dist <= blk, 0.0, NEG_BIG), NEG_BIG)
    k_min = jnp.where(pl.program_id(1) == 0, blk, 0)
    lane = lax.broadcasted_iota(jnp.int32, (blk, LANES), 1)
    low = lane < HEAD_DIM
    for j in range(ATT_BLOCKS_PER_STEP):
        rows = slice(j * blk, (j + 1) * blk)
        before = slice((j - 1) * blk, j * blk)
        bias = jnp.where(kj >= k_min, band, NEG_BIG) if j == 0 else band
        lse_tile = jnp.zeros((blk, LANES), F32)
        for hp in range(N_HEADS // 2):
            cols = slice(hp * LANES, (hp + 1) * LANES)
            q2 = q_ref[0, rows, cols]
            k_prev = kp_ref[0, :, cols] if j == 0 else kc_ref[0, before, cols]
            v_prev = vp_ref[0, :, cols] if j == 0 else vc_ref[0, before, cols]
            kk = jnp.concatenate([k_prev, kc_ref[0, rows, cols]], axis=0)
            vv = jnp.concatenate([v_prev, vc_ref[0, rows, cols]], axis=0)
            halves = []
            for hh in range(2):
                head = 2 * hp + hh
                qm = jnp.where(low if hh == 0 else jnp.logical_not(low), q2, jnp.zeros_like(q2))
                s = lax.dot_general(qm, kk, (((1,), (1,)), ((), ())),
                                    preferred_element_type=F32) + bias
                m = jnp.max(s, axis=-1, keepdims=True)
                p = jnp.exp(s - m)
                den = jnp.sum(p, axis=-1, keepdims=True)
                pv = jnp.dot(p.astype(BF16), vv, preferred_element_type=F32)
                halves.append(pv / den)
                lse_tile = jnp.where(lane == head, m + jnp.log(den), lse_tile)
            o_ref[0, rows, cols] = jnp.where(low, halves[0], halves[1]).astype(BF16)
        lse_ref[0, rows, :] = lse_tile


def _attn_pattern(qkv_sub):
    n_sub, sub_len, _ = qkv_sub.shape

    step = ATT_BLOCKS_PER_STEP * ATT_BLOCK

    def cur(which):
        return pl.BlockSpec((1, step, D_MODEL), lambda s, n: (s, n, which))

    def prev(which):
        return pl.BlockSpec((1, ATT_BLOCK, D_MODEL),
                            lambda s, n: (s, jnp.maximum(n * ATT_BLOCKS_PER_STEP - 1, 0), which))

    return pl.pallas_call(
        _attn_kernel,
        grid=(n_sub, sub_len // step),
        in_specs=[cur(0), prev(1), cur(1), prev(2), cur(2)],
        out_specs=[pl.BlockSpec((1, step, D_MODEL), lambda s, n: (s, n, 0)),
                   pl.BlockSpec((1, step, LANES), lambda s, n: (s, n, 0))],
        out_shape=[jax.ShapeDtypeStruct((n_sub, sub_len, D_MODEL), BF16),
                   jax.ShapeDtypeStruct((n_sub, sub_len, LANES), F32)],
        compiler_params=_params(2),
    )(*([qkv_sub] * 5))


def _odd_post_kernel(h_ref, o1_ref, l1_ref, o4_ref, l4_ref, o16_ref, l16_ref, expand_ref, wo_ref,
                     g_ref, b_ref, rwh_ref, rwl_ref, rb_ref, tri_ref, hn_ref, hnb_ref, idx_ref,
                     gate_ref, cnt_ref, os4_ref, ls4_ref, os16_ref, ls16_ref, seen_ref):
    rows = h_ref.shape[0]
    n_col = D_MODEL // LANES
    for dil, o_ref, l_ref, os_ref, ls_ref in ((DILATIONS[1], o4_ref, l4_ref, os4_ref, ls4_ref),
                                              (DILATIONS[2], o16_ref, l16_ref, os16_ref, ls16_ref)):
        for r in range(dil):
            sel = pl.ds(r, rows // dil, stride=dil)
            ls_ref[sel, :] = l_ref[0, r]
            for c in range(n_col):
                os_ref[c, sel, :] = o_ref[0, r, :, c * LANES:(c + 1) * LANES].astype(F32)
    lses = (l1_ref[...], ls4_ref[...], ls16_ref[...])
    outs = (o1_ref[...].astype(F32),
            jnp.concatenate([os4_ref[c] for c in range(n_col)], axis=1),
            jnp.concatenate([os16_ref[c] for c in range(n_col)], axis=1))
    mx = jnp.maximum(jnp.maximum(lses[0], lses[1]), lses[2])
    es = [jnp.exp(l - mx) for l in lses]
    den = es[0] + es[1] + es[2]
    o = jnp.zeros((rows, D_MODEL), F32)
    for e, out in zip(es, outs):
        hi, lo = _split_bf16(e / den)
        w = jnp.dot(jnp.concatenate([hi, lo], axis=1), expand_ref[...], preferred_element_type=F32)
        o = o + w * out
    mix = jnp.dot(o.astype(BF16), wo_ref[...], preferred_element_type=F32)
    _norm_and_route(DN_ALPHA * h_ref[...] + mix, g_ref, b_ref, (rwh_ref, rwl_ref, rb_ref, tri_ref),
                    hn_ref, hnb_ref, idx_ref, gate_ref, cnt_ref, seen_ref)


def _odd_post(h, pattern_outs, consts, seq_len):
    t = h.shape[0]
    tps = seq_len // ROW_TILE
    (o1, l1), (o4, l4), (o16, l16) = pattern_outs
    d4, d16 = DILATIONS[1], DILATIONS[2]
    out_specs, out_shape = _post_outputs(t)
    return pl.pallas_call(
        _odd_post_kernel,
        grid=(t // ROW_TILE,),
        in_specs=[_rows(D_MODEL), _rows(D_MODEL), _rows(LANES),
                  _residue_spec(d4, ROW_TILE, D_MODEL, tps), _residue_spec(d4, ROW_TILE, LANES, tps),
                  _residue_spec(d16, ROW_TILE, D_MODEL, tps), _residue_spec(d16, ROW_TILE, LANES, tps)]
        + [_full(a.shape) for a in consts],
        out_specs=out_specs, out_shape=out_shape,
        scratch_shapes=[pltpu.VMEM((D_MODEL // LANES, ROW_TILE, LANES), F32),
                        pltpu.VMEM((ROW_TILE, LANES), F32)] * 2 + [pltpu.VMEM((8, LANES), F32)],
        compiler_params=_params(),
    )(h, o1, l1, o4, l4, o16, l16, *consts)


def _attention_layer(h, w_qkv, w_o, cos, sin, consts, batch, seq_len):
    t = h.shape[0]
    qkv1, qkv4, qkv16 = _qkv_rope(h, w_qkv, cos, sin, batch, seq_len)
    outs = []
    for dil, qkv in zip(DILATIONS, (qkv1, qkv4, qkv16)):
        sub_len = seq_len // dil
        o, lse = _attn_pattern(qkv.reshape(batch * dil, sub_len, 3 * D_MODEL))
        if dil == 1:
            outs.append((o.reshape(t, D_MODEL), lse.reshape(t, LANES)))
        else:
            outs.append((o.reshape(batch, dil, sub_len, D_MODEL),
                         lse.reshape(batch, dil, sub_len, LANES)))
    head_of_lane = jnp.arange(D_MODEL, dtype=jnp.int32) // HEAD_DIM
    expand = (jnp.arange(LANES, dtype=jnp.int32)[:, None] == head_of_lane[None, :]).astype(BF16)
    expand = jnp.concatenate([expand, expand], axis=0)
    return _odd_post(h, outs, [expand, w_o.astype(BF16), *consts], seq_len)


def _moe_kernel(be_ref, nused_ref, x_ref, wgu_ref, bgu_ref, wd_ref, bd_ref, y_ref,
                wgu_bf_ref, wd_bf_ref):
    i = pl.program_id(0)
    new_expert = jnp.logical_or(i == 0, be_ref[i] != be_ref[jnp.maximum(i - 1, 0)])

    @pl.when(new_expert)
    def _():
        wgu_bf_ref[...] = wgu_ref[0, 0].astype(BF16)
        wd_bf_ref[...] = wd_ref[0, 0].astype(BF16)

    @pl.when(i < nused_ref[0])
    def _():
        hid = jnp.dot(x_ref[...], wgu_bf_ref[...], preferred_element_type=F32) + bgu_ref[0]
        gate = jnp.minimum(hid[:, :D_MODEL], SWIGLU_LIMIT)
        lin = jnp.clip(hid[:, D_MODEL:], -SWIGLU_LIMIT, SWIGLU_LIMIT)
        act = (lin + 1.0) * (gate * _sigmoid(SWIGLU_ALPHA * gate))
        y = jnp.dot(act.astype(BF16), wd_bf_ref[...], preferred_element_type=F32) + bd_ref[0]
        y_ref[...] = y.astype(BF16)

    @pl.when(i >= nused_ref[0])
    def _():
        y_ref[...] = jnp.zeros_like(y_ref)


def _moe_experts(xs, block_e, n_used, layer, w_gu, b_gu, w_down, b_down):
    n_rows = xs.shape[0]
    grid_spec = pltpu.PrefetchScalarGridSpec(
        num_scalar_prefetch=2,
        grid=(n_rows // MOE_ROWS,),
        in_specs=[pl.BlockSpec((MOE_ROWS, D_MODEL), lambda i, be, nu: (i, 0)),
                  pl.BlockSpec((1, 1, D_MODEL, 2 * D_MODEL), lambda i, be, nu: (layer, be[i], 0, 0)),
                  pl.BlockSpec((1, 1, 2 * D_MODEL), lambda i, be, nu: (be[i], 0, 0)),
                  pl.BlockSpec((1, 1, D_MODEL, D_MODEL), lambda i, be, nu: (layer, be[i], 0, 0)),
                  pl.BlockSpec((1, 1, D_MODEL), lambda i, be, nu: (be[i], 0, 0))],
        out_specs=pl.BlockSpec((MOE_ROWS, D_MODEL), lambda i, be, nu: (i, 0)),
        scratch_shapes=[pltpu.VMEM((D_MODEL, 2 * D_MODEL), BF16), pltpu.VMEM((D_MODEL, D_MODEL), BF16)],
    )
    return pl.pallas_call(
        _moe_kernel,
        grid_spec=grid_spec,
        out_shape=jax.ShapeDtypeStruct((n_rows, D_MODEL), BF16),
        compiler_params=_params(),
    )(block_e, n_used, xs, w_gu, b_gu[:, None, :], w_down, b_down[:, None, :])


def _routing_tables(idx_tile, counts, t):
    n_assign = t * TOP_K
    top_idx = idx_tile[:TOP_K]
    rank = idx_tile[TOP_K:]
    counts = counts[0, :N_EXPERTS].astype(jnp.int32)
    padded = (counts + MOE_ROWS - 1) // MOE_ROWS * MOE_ROWS
    pad_end = jnp.cumsum(padded)
    pad_start = pad_end - padded
    start_of = jnp.zeros_like(rank)
    for e in range(N_EXPERTS):
        start_of = jnp.where(top_idx == e, pad_start[e], start_of)
    dest = (start_of + rank).reshape(-1)
    n_blocks = n_assign // MOE_ROWS + N_EXPERTS
    block_start = jnp.arange(n_blocks, dtype=jnp.int32) * MOE_ROWS
    block_e = jnp.minimum(jnp.sum((block_start[:, None] >= pad_end[None, :]).astype(jnp.int32), axis=1),
                          N_EXPERTS - 1)
    n_used = (pad_end[-1:] // MOE_ROWS).astype(jnp.int32)
    keys = top_idx * t + jnp.arange(t, dtype=jnp.int32)[None, :]
    tok_sorted = lax.sort(keys.reshape(-1)) % t
    starts = jnp.cumsum(counts) - counts
    rows = jnp.arange(n_blocks * MOE_ROWS, dtype=jnp.int32).reshape(n_blocks, MOE_ROWS)
    shift = (starts - pad_start)[block_e][:, None]
    live_end = (pad_start + counts)[block_e][:, None]
    compact = jnp.clip(rows + shift, 0, n_assign - 1)
    row_tok = jnp.where(rows < live_end, tok_sorted.at[compact].get(mode="promise_in_bounds"),
                        rows % t).reshape(-1)
    return row_tok, dest, block_e, n_used


def _ffn_ln_kernel(h_ref, y4_ref, gate_ref, g_ref, b_ref, o_ref):
    gates = gate_ref[...]
    acc = DN_ALPHA * h_ref[...]
    for k in range(TOP_K):
        acc = acc + gates[:, k:k + 1] * y4_ref[k].astype(F32)
    o_ref[...] = _layer_norm(acc, g_ref[...], b_ref[...])


def _ffn_ln(h, y4, gates, g, b):
    t = h.shape[0]
    return pl.pallas_call(
        _ffn_ln_kernel,
        grid=(t // ROW_TILE,),
        in_specs=[_rows(D_MODEL), pl.BlockSpec((TOP_K, ROW_TILE, D_MODEL), lambda i: (0, i, 0)),
                  _rows(LANES), _full(g.shape), _full(b.shape)],
        out_specs=_rows(D_MODEL),
        out_shape=jax.ShapeDtypeStruct((t, D_MODEL), F32),
        compiler_params=_params(),
    )(h, y4, gates, g, b)


def _moe_layer(hn, hn_bf, idx_tile, gate_tile, counts, layer, w_gu, b_gu, w_down, b_down, ln_g, ln_b):
    t = hn.shape[0]
    row_tok, dest, block_e, n_used = _routing_tables(idx_tile, counts, t)
    xs = hn_bf.at[row_tok].get(mode="promise_in_bounds")
    ys = _moe_experts(xs, block_e, n_used, layer, w_gu, b_gu, w_down, b_down)
    y4 = ys.at[dest].get(mode="promise_in_bounds").reshape(TOP_K, t, D_MODEL)
    return _ffn_ln(hn, y4, gate_tile, ln_g[None, :], ln_b[None, :])


def _router_consts(router_w, router_b):
    w = jnp.zeros((D_MODEL, LANES), F32).at[:, :N_EXPERTS].set(router_w)
    w_hi, w_lo = _split_bf16(w)
    b = jnp.full((1, LANES), NEG_BIG, F32).at[0, :N_EXPERTS].set(router_b)
    r = jnp.arange(ROW_TILE, dtype=jnp.int32)
    tri = (r[None, :] < r[:, None]).astype(BF16)
    return jnp.concatenate([w_hi, w_lo], axis=1), w_hi, b, tri


def kernel(x, hy_w_in, conv_w, ssm_a_re, ssm_a_im, ssm_log_dt, ssm_b_re, ssm_b_im, ssm_c_re,
           ssm_c_im, ssm_d, ssm_w_glu, ssm_b_glu, hy_w_out, att_w_qkv, att_w_o, ln_mix_g,
           ln_mix_b, ln_ffn_g, ln_ffn_b, router_w, router_b, expert_w_gu, expert_b_gu,
           expert_w_down, expert_b_down):
    batch, seq_len, _ = x.shape
    t = batch * seq_len
    h = x.reshape(t, D_MODEL)
    cos, sin = _rope_tables(seq_len)
    for layer in range(DEPTH):
        i = layer // 2
        consts = [ln_mix_g[layer][None, :], ln_mix_b[layer][None, :],
                  *_router_consts(router_w[layer], router_b[layer])]
        if layer % 2 == 0:
            y_conv, u, u_sg = _inproj_conv(h, hy_w_in[i], conv_w[i], seq_len)
            tables = _s5_tables(ssm_a_re[i], ssm_a_im[i], ssm_log_dt[i], ssm_b_re[i], ssm_b_im[i],
                                ssm_c_re[i], ssm_c_im[i])
            y_sg = _s5_scan(u_sg, tables, batch, seq_len // SSM_CHUNK)
            w_out = hy_w_out[i].astype(BF16)
            hn, hn_bf, idx_tile, gate_tile, counts = _even_post(
                h, y_conv, y_sg, u,
                [ssm_d[i].reshape(1, SSM_WIDTH), ssm_w_glu[i].astype(BF16), ssm_b_glu[i][None, :],
                 w_out[:CONV_CH], w_out[CONV_CH:], *consts])
        else:
            hn, hn_bf, idx_tile, gate_tile, counts = _attention_layer(
                h, att_w_qkv[i], att_w_o[i], cos, sin, consts, batch, seq_len)
        h = _moe_layer(hn, hn_bf, idx_tile, gate_tile, counts, layer, expert_w_gu, expert_b_gu[layer],
                       expert_w_down, expert_b_down[layer], ln_ffn_g[layer], ln_ffn_b[layer])
    return h.reshape(batch, seq_len, D_MODEL)
```
